```python
import jax
import jax.numpy as jnp
from jax import lax
import numpy as np

D_MODEL = 1024
BATCH = 8
SEQ = 4096
DEPTH = 2

GRID_W = 64
CTX_LEN = 256
EPS = 1e-6

HGRN_HEADS = 4
HGRN_DK = 128
HGRN_DV = 128
HGRN_W = HGRN_HEADS * HGRN_DK
HGRN_VW = HGRN_HEADS * HGRN_DV
HGRN_CHUNK = 64
SGU_HEADS = 4
SGU_HEAD_DIM = 128
SGU_W = SGU_HEADS * SGU_HEAD_DIM
SGU_CHUNK = 128
ROWS_PER_CHUNK = SGU_CHUNK // GRID_W
MIX_W = HGRN_VW + SGU_W
IN_W = 3 * HGRN_W + 2 * HGRN_VW + 2 * SGU_W
SPLIT_POINTS = (HGRN_W, 2 * HGRN_W, 3 * HGRN_W, 3 * HGRN_W + HGRN_VW,
                3 * HGRN_W + 2 * HGRN_VW, 3 * HGRN_W + 2 * HGRN_VW + SGU_W)
N_GROUPS = 4
EXPERTS_PER_GROUP = 8
N_EXPERTS = N_GROUPS * EXPERTS_PER_GROUP
TOP_K = 2
D_EXPERT = 512
MOE_BLOCK = 128

kernel_name = 'hybrid_hgrn2_sgu_hmoe_dit'


def rmsnorm(x, gain):
    xf = x.astype(jnp.float32)
    y = xf * lax.rsqrt(jnp.mean(xf * xf, axis=-1, keepdims=True) + EPS)
    return (y * gain.astype(jnp.float32)).astype(x.dtype)


def modulate(h, shift, scale):
    return h * (1 + scale) + shift


def flip(t):
    return jnp.flip(t, axis=1)


def _heads(t, n_heads):
    return t.astype(jnp.float32).reshape(t.shape[0], t.shape[1], n_heads, -1)


def forget_gate(z, lb):
    zf = z.astype(jnp.float32)
    lb = lb.astype(jnp.float32)
    log_f = jnp.logaddexp(jnp.log(lb), jnp.log1p(-lb) + jax.nn.log_sigmoid(zf))
    k = (1.0 - lb) * jax.nn.sigmoid(-zf)
    return log_f, k


def split_mixer_inputs(p, lb):
    q, z_fwd, z_bwd, i, g, u, v = jnp.split(p, SPLIT_POINTS, axis=-1)
    lf_fwd, k_fwd = forget_gate(z_fwd, lb[0])
    lf_bwd, k_bwd = forget_gate(z_bwd, lb[1])
    h = HGRN_HEADS
    return (_heads(q, h), _heads(i, h), _heads(lf_fwd, h), _heads(k_fwd, h),
            _heads(lf_bwd, h), _heads(k_bwd, h), g, u, v)


def hgrn_chunk(q, k, v, log_f, s0):
    b, l = q.shape[0], q.shape[1]
    n = l // HGRN_CHUNK

    def blocks(t):
        return t.reshape(b, n, HGRN_CHUNK, t.shape[2], t.shape[3])

    q, k, v, log_f = blocks(q), blocks(k), blocks(v), blocks(log_f)
    cum = jnp.cumsum(log_f, axis=2)
    ref = cum[:, :, HGRN_CHUNK // 2 - 1][:, :, None]
    tot = cum[:, :, -1]
    scores = jnp.einsum('bnihd,bnjhd->bnhij', q * jnp.exp(cum - ref), k * jnp.exp(ref - cum))
    lower_tri = jnp.tril(jnp.ones((HGRN_CHUNK, HGRN_CHUNK), dtype=bool))
    scores = jnp.where(lower_tri, scores, 0.0)
    o_intra = jnp.einsum('bnhij,bnjhe->bnihe', scores, v)
    u = jnp.einsum('bnjhd,bnjhe->nbhde', k * jnp.exp(tot[:, :, None] - cum), v)
    decay = jnp.exp(tot).transpose(1, 0, 2, 3)

    def step(s, inp):
        d, un = inp
        return d[..., None] * s + un, s

    s_final, s_start = lax.scan(step, s0, (decay, u))
    o_inter = jnp.einsum('bnihd,nbhde->bnihe', q * jnp.exp(cum), s_start)
    return (o_intra + o_inter).reshape(b, l, q.shape[3], v.shape[4]), s_final


def hgrn_final_state(k, v, log_f):
    cum = jnp.cumsum(log_f, axis=1)
    return jnp.einsum('blhd,blhe->bhde', k * jnp.exp(cum[:, -1:] - cum), v)


def hgrn_bidirectional(q, i, lf_f, k_f, lf_b, k_b, s0_f, s0_b):
    o_f, s_f = hgrn_chunk(q, k_f, i, lf_f, s0_f)
    o_b, s_b = hgrn_chunk(flip(q), flip(k_b), flip(i), flip(lf_b), s0_b)
    return o_f + flip(o_b), s_f, s_b


def head_rmsnorm(o, gain):
    o = o * lax.rsqrt(jnp.mean(o * o, axis=-1, keepdims=True) + EPS)
    return o.reshape(o.shape[0], o.shape[1], -1) * gain.astype(jnp.float32)


def spatial_gating(u, v, gain, w_s, b_s, n_chunks):
    b = u.shape[0]
    v = rmsnorm(v, gain).reshape(b, n_chunks, SGU_CHUNK, SGU_HEADS, SGU_HEAD_DIM)
    mixed = jnp.einsum('hij,bnjhd->bnihd', w_s, v) + b_s.T[None, None, :, :, None]
    return u * mixed.reshape(u.shape)


def mixer_output(o, g, u, v, n_chunks, hgrn_gain, sgu_gain, w_s, b_s, w_o):
    hg = (head_rmsnorm(o, hgrn_gain) * jax.nn.silu(g.astype(jnp.float32))).astype(g.dtype)
    sg = spatial_gating(jax.nn.gelu(u), jax.nn.gelu(v), sgu_gain, w_s, b_s, n_chunks)
    return jnp.concatenate([hg, sg], axis=-1) @ w_o


def hier_moe(h, w_group, b_group, w_router, b_router, w_gate, w_up, w_down):
    t, d = h.shape
    logits_g = (h @ w_group).astype(jnp.float32) + b_group.astype(jnp.float32)
    p_group = jax.nn.softmax(logits_g, axis=-1)
    g_sel = jnp.argmax(logits_g, axis=-1)
    p_sel = jnp.take_along_axis(p_group, g_sel[:, None], axis=-1)
    logits_e = ((h @ w_router).astype(jnp.float32) + b_router.astype(jnp.float32))
    logits_e = logits_e.reshape(t, N_GROUPS, EXPERTS_PER_GROUP)
    logits_in = jnp.take_along_axis(logits_e, g_sel[:, None, None], axis=1)[:, 0]
    top_logits, top_idx = lax.top_k(logits_in, TOP_K)
    combine = p_sel * jax.nn.softmax(top_logits, axis=-1)
    expert = (g_sel[:, None] * EXPERTS_PER_GROUP + top_idx).reshape(-1)
    n_slots = t * TOP_K
    order = jnp.argsort(expert)
    e_sorted = expert[order]
    tok_sorted = order // TOP_K
    counts = jax.ops.segment_sum(jnp.ones((n_slots,), jnp.int32), expert, num_segments=N_EXPERTS)
    padded = (counts + MOE_BLOCK - 1) // MOE_BLOCK * MOE_BLOCK
    pad_end = jnp.cumsum(padded)
    pad_start = pad_end - padded
    start = jnp.cumsum(counts) - counts
    dest = pad_start[e_sorted] + jnp.arange(n_slots) - start[e_sorted]
    n_blocks = -(-n_slots // MOE_BLOCK) + N_EXPERTS
    buf = jnp.zeros((n_blocks * MOE_BLOCK, d), h.dtype).at[dest].set(h[tok_sorted])
    block_expert = jnp.minimum(
        jnp.searchsorted(pad_end, jnp.arange(n_blocks) * MOE_BLOCK, side='right'), N_EXPERTS - 1)

    def expert_block(args):
        xb, e = args
        return (jax.nn.silu(xb @ w_gate[e]) * (xb @ w_up[e])) @ w_down[e]

    out = lax.map(expert_block, (buf.reshape(n_blocks, MOE_BLOCK, d), block_expert))
    y_sorted = out.reshape(-1, d)[dest] * combine.reshape(-1)[order][:, None].astype(h.dtype)
    return jax.ops.segment_sum(y_sorted, tok_sorted, num_segments=t)


def setup_inputs(seed: int = 0) -> dict:
    key = jax.random.key(seed)
    ks = jax.random.split(key, 24)
    d = D_MODEL

    def nrm(k, shape, scale):
        return jax.random.normal(k, shape, jnp.float32) * scale

    return {
        'x': nrm(ks[0], (BATCH, SEQ, d), 1.0),
        'c': nrm(ks[1], (BATCH, d), 1.0),
        'ctx': nrm(ks[2], (BATCH, CTX_LEN, d), 1.0),
        'c_ctx': nrm(ks[3], (d,), 1.0),
        'norm1': 1.0 + nrm(ks[4], (DEPTH, d), 0.02),
        'norm2': 1.0 + nrm(ks[5], (DEPTH, d), 0.02),
        'w_mod': nrm(ks[6], (DEPTH, d, 6 * d), 0.5 * d ** -0.5),
        'b_mod': nrm(ks[7], (DEPTH, 6 * d), 0.02),
        'w_in': nrm(ks[8], (DEPTH, d, IN_W), d ** -0.5),
        'lb_logits': nrm(ks[9], (DEPTH, 2, HGRN_W), 0.5),
        'hgrn_norm': 1.0 + nrm(ks[10], (DEPTH, HGRN_VW), 0.02),
        'sgu_norm': 1.0 + nrm(ks[11], (DEPTH, SGU_W), 0.02),
        'sgu_w': nrm(ks[12], (DEPTH, SGU_HEADS, SGU_CHUNK, SGU_CHUNK), SGU_CHUNK ** -0.5),
        'sgu_b': 1.0 + nrm(ks[13], (DEPTH, SGU_HEADS, SGU_CHUNK), 0.02),
        'w_out': nrm(ks[14], (DEPTH, MIX_W, d), MIX_W ** -0.5),
        'w_group': nrm(ks[15], (DEPTH, d, N_GROUPS), d ** -0.5),
        'b_group': nrm(ks[16], (DEPTH, N_GROUPS), 0.01),
        'w_router': nrm(ks[17], (DEPTH, d, N_EXPERTS), d ** -0.5),
        'b_router': nrm(ks[18], (DEPTH, N_EXPERTS), 0.01),
        'w_gate': nrm(ks[19], (DEPTH, N_EXPERTS, d, D_EXPERT), d ** -0.5),
        'w_up': nrm(ks[20], (DEPTH, N_EXPERTS, d, D_EXPERT), d ** -0.5),
        'w_down': nrm(ks[21], (DEPTH, N_EXPERTS, D_EXPERT, d), D_EXPERT ** -0.5),
        'final_norm': 1.0 + nrm(ks[22], (d,), 0.02),
    }


def reference(x, c, ctx, c_ctx, norm1, norm2, w_mod, b_mod, w_in, lb_logits, hgrn_norm,
              sgu_norm, sgu_w, sgu_b, w_out, w_group, b_group, w_router, b_router,
              w_gate, w_up, w_down, final_norm):
    b, l, d = x.shape
    rows = l // GRID_W
    n_lat_chunks = rows // ROWS_PER_CHUNK
    n_ctx_chunks = ctx.shape[1] // SGU_CHUNK
    lb_cum = jnp.cumsum(jax.nn.softmax(lb_logits.astype(jnp.float32), axis=0), axis=0)
    lower_bound = jnp.maximum(lb_cum - lb_cum[0:1], 0.0)
    s_zero = jnp.zeros((b, HGRN_HEADS, HGRN_DK, HGRN_DV), jnp.float32)
    y = ctx
    for layer in range(DEPTH):
        last = layer == DEPTH - 1
        mod_lat = jax.nn.silu(c) @ w_mod[layer] + b_mod[layer]
        mod_ctx = jax.nn.silu(c_ctx) @ w_mod[layer] + b_mod[layer]
        sh1, sc1, g1, sh2, sc2, g2 = jnp.split(mod_lat[:, None, :], 6, axis=-1)
        csh1, csc1, cg1, csh2, csc2, cg2 = jnp.split(mod_ctx[None, None, :], 6, axis=-1)

        p_lat = modulate(rmsnorm(x, norm1[layer]), sh1, sc1) @ w_in[layer]
        p_ctx = modulate(rmsnorm(y, norm1[layer]), csh1, csc1) @ w_in[layer]
        q, i, lf_f, k_f, lf_b, k_b, g, u, v = split_mixer_inputs(p_lat, lower_bound[layer])
        cq, ci, clf_f, ck_f, clf_b, ck_b, cg, cu, cv = split_mixer_inputs(p_ctx, lower_bound[layer])
        if last:
            s_f = hgrn_final_state(ck_f, ci, clf_f)
            s_b = hgrn_final_state(flip(ck_b), flip(ci), flip(clf_b))
        else:
            co, s_f, s_b = hgrn_bidirectional(cq, ci, clf_f, ck_f, clf_b, ck_b, s_zero, s_zero)
            out_ctx = mixer_output(co, cg, cu, cv, n_ctx_chunks, hgrn_norm[layer], sgu_norm[layer],
                                   sgu_w[layer], sgu_b[layer], w_out[layer])
            y = y + cg1 * out_ctx
        lo, _, _ = hgrn_bidirectional(q, i, lf_f, k_f, lf_b, k_b, s_f, s_b)
        out_lat = mixer_output(lo, g, u, v, n_lat_chunks, hgrn_norm[layer], sgu_norm[layer],
                               sgu_w[layer], sgu_b[layer], w_out[layer])
        x = x + g1 * out_lat

        moe_w = (w_group[layer], b_group[layer], w_router[layer], b_router[layer],
                 w_gate[layer], w_up[layer], w_down[layer])
        h2_lat = modulate(rmsnorm(x, norm2[layer]), sh2, sc2).reshape(-1, d)
        if last:
            f_lat = hier_moe(h2_lat, *moe_w).reshape(b, l, d)
        else:
            h2_ctx = modulate(rmsnorm(y, norm2[layer]), csh2, csc2).reshape(-1, d)
            f_all = hier_moe(jnp.concatenate([h2_lat, h2_ctx], axis=0), *moe_w)
            f_lat = f_all[: b * l].reshape(b, l, d)
            y = y + cg2 * f_all[b * l:].reshape(y.shape)
        x = x + g2 * f_lat
    return rmsnorm(x, final_norm)
```

```python
import functools

import jax
import jax.numpy as jnp
from jax import lax
from jax.experimental import pallas as pl
from jax.experimental.pallas import tpu as pltpu

F32 = jnp.float32
BF16 = jnp.bfloat16

EPS = 1e-6
HEADS = 4
HEAD_DIM = 128
HW = HEADS * HEAD_DIM
HGRN_CHUNK = 64
SGU_CHUNK = 128
N_GROUPS = 4
EXPERTS_PER_GROUP = 8
N_EXPERTS = N_GROUPS * EXPERTS_PER_GROUP
TOP_K = 2
N_MOD = 6
ROUTE_LANES = 128
MOD_ROWS = 16

VMEM_LIMIT = 48 * 1024 * 1024


def _cparams(*sem):
    return pltpu.CompilerParams(dimension_semantics=sem, vmem_limit_bytes=VMEM_LIMIT)


def _split2(a):
    hi = a.astype(BF16)
    lo = (a - hi.astype(F32)).astype(BF16)
    return hi, lo


def _dot(a, b):
    return jnp.dot(a, b, preferred_element_type=F32)


def _dot_nt(a, b):
    return lax.dot_general(a, b, (((1,), (1,)), ((), ())), preferred_element_type=F32)


def _dot_tn(a, b):
    return lax.dot_general(a, b, (((0,), (0,)), ((), ())), preferred_element_type=F32)


def _dot3(a, b):
    ah, al = _split2(a)
    bh, bl = _split2(b)
    return _dot(ah, bh) + (_dot(al, bh) + _dot(ah, bl))


def _silu(x):
    return x / (1.0 + jnp.exp(-x))


def _rms(x):
    return x * lax.rsqrt(jnp.mean(x * x, axis=-1, keepdims=True) + EPS)


def _mod_kernel(c_ref, w_ref, b_ref, o_ref):
    o_ref[...] = _dot3(_silu(c_ref[...]), w_ref[...]) + b_ref[...]


def _modulation(cc, w_mod, b_mod):
    depth, d, n = w_mod.shape
    tn = 1536
    return pl.pallas_call(
        _mod_kernel,
        out_shape=jax.ShapeDtypeStruct((depth, MOD_ROWS, n), F32),
        grid=(depth, n // tn),
        in_specs=[
            pl.BlockSpec((MOD_ROWS, d), lambda l, j: (0, 0)),
            pl.BlockSpec((None, d, tn), lambda l, j: (l, 0, j)),
            pl.BlockSpec((None, 1, tn), lambda l, j: (l, 0, j)),
        ],
        out_specs=pl.BlockSpec((None, MOD_ROWS, tn), lambda l, j: (l, 0, j)),
        compiler_params=_cparams("arbitrary", "arbitrary"),
        name="modulation",
    )(cc, w_mod, b_mod.reshape(depth, 1, n))


def _log_forget(z, lb):
    ls = jnp.minimum(z, 0.0) - jnp.log1p(jnp.exp(-jnp.abs(z)))
    a = jnp.log1p(-lb) + ls
    b = jnp.log(lb)
    return jnp.maximum(a, b) + jnp.log1p(jnp.exp(-jnp.abs(a - b)))


def _inproj_kernel(x_ref, sh_ref, sc_ref, n1_ref, w_ref, lb_ref, sgn_ref,
                   q_ref, i_ref, lff_ref, lfb_ref, sg_ref, gu_ref, vn_ref):
    h = _rms(x_ref[...]) * n1_ref[...]
    h = h * (1.0 + sc_ref[...]) + sh_ref[...]
    hb = h.astype(BF16)

    def proj(j):
        return _dot(hb, w_ref[:, j * HW:(j + 1) * HW])

    q_ref[...] = proj(0).astype(BF16)
    lff_ref[...] = _log_forget(proj(1), lb_ref[0:1, :])
    lfb_ref[...] = _log_forget(proj(2), lb_ref[1:2, :])
    i_ref[...] = proj(3).astype(BF16)
    sg_ref[...] = _silu(proj(4)).astype(BF16)
    gu_ref[...] = jax.nn.gelu(proj(5)).astype(BF16)
    vn_ref[...] = (_rms(jax.nn.gelu(proj(6))) * sgn_ref[...]).astype(BF16)


def _inproj(x2d, mod3, tokens_per_mod_row, n1, w_in, lb, sgu_gain):
    t, d = x2d.shape
    tm = min(512, t)
    tpr = tokens_per_mod_row // tm
    row = lambda j: pl.BlockSpec((None, 1, d), lambda i: (i // tpr, 0, j))
    full = lambda a: pl.BlockSpec(a.shape, lambda i: (0,) * a.ndim)
    tok = pl.BlockSpec((tm, HW), lambda i: (i, 0))
    return pl.pallas_call(
        _inproj_kernel,
        out_shape=[jax.ShapeDtypeStruct((t, HW), dt) for dt in (BF16, BF16, F32, F32, BF16, BF16, BF16)],
        grid=(t // tm,),
        in_specs=[pl.BlockSpec((tm, d), lambda i: (i, 0)), row(0), row(1),
                  full(n1), full(w_in), full(lb), full(sgu_gain)],
        out_specs=[tok] * 7,
        compiler_params=_cparams("arbitrary"),
        name="inproj",
    )(x2d, mod3, mod3, n1, w_in, lb, sgu_gain)


def _hgrn_kernel(q_ref, i_ref, lf_ref, s0_ref, o_ref, sfin_ref, st_ref, *, reverse, tt):
    t = pl.program_id(1)

    @pl.when(t == 0)
    def _():
        st_ref[...] = s0_ref[...]

    c = HGRN_CHUNK
    nchunks = tt // c
    rows = lax.broadcasted_iota(jnp.int32, (c, c), 0)
    cols = lax.broadcasted_iota(jnp.int32, (c, c), 1)
    incl = (cols >= rows) if reverse else (cols <= rows)
    tri = jnp.where(incl, 1.0, 0.0).astype(BF16)
    ref_row = c // 2 if reverse else c // 2 - 1
    tot_row = 0 if reverse else c - 1

    def chunk(ci, carry):
        cidx = (nchunks - 1 - ci) if reverse else ci
        r0 = pl.multiple_of(cidx * c, c)
        lf = lf_ref[pl.ds(r0, c), :]
        hi = lf.astype(BF16)
        r1 = lf - hi.astype(F32)
        mid = r1.astype(BF16)
        lo = (r1 - mid.astype(F32)).astype(BF16)
        cum = _dot(tri, hi) + (_dot(tri, mid) + _dot(tri, lo))
        ref = cum[ref_row:ref_row + 1, :]
        tot = cum[tot_row:tot_row + 1, :]
        k = 1.0 - jnp.exp(lf)
        qf = q_ref[pl.ds(r0, c), :].astype(F32)
        iv = i_ref[pl.ds(r0, c), :]
        q_in = (qf * jnp.exp(cum - ref)).astype(BF16)
        k_in = (k * jnp.exp(ref - cum)).astype(BF16)
        k_st = (k * jnp.exp(tot - cum)).astype(BF16)
        q_st = (qf * jnp.exp(cum)).astype(BF16)
        dec = jnp.exp(tot)
        for h in range(HEADS):
            sl = slice(h * HEAD_DIM, (h + 1) * HEAD_DIM)
            sc = _dot_nt(q_in[:, sl], k_in[:, sl])
            sc = jnp.where(incl, sc, 0.0).astype(BF16)
            st = st_ref[h]
            o_h = _dot(sc, iv[:, sl]) + _dot_nt(q_st[:, sl], st.astype(BF16))
            o_ref[pl.ds(r0, c), sl] = o_h
            st_ref[h] = st * dec[:, sl] + _dot_tn(iv[:, sl], k_st[:, sl])
        return carry

    lax.fori_loop(0, nchunks, chunk, 0)

    @pl.when(t == pl.num_programs(1) - 1)
    def _():
        sfin_ref[...] = st_ref[...]


def _hgrn(q, i, lf, s0, batch, reverse):
    t = q.shape[0]
    seq = t // batch
    tt = min(512, seq)
    nt = seq // tt
    q3, i3, lf3 = (a.reshape(batch, seq, HW) for a in (q, i, lf))
    tile = (lambda b, s: (b, nt - 1 - s, 0)) if reverse else (lambda b, s: (b, s, 0))
    tok = pl.BlockSpec((None, tt, HW), tile)
    state = pl.BlockSpec((None, HEADS, HEAD_DIM, HEAD_DIM), lambda b, s: (b, 0, 0, 0))
    o, sfin = pl.pallas_call(
        functools.partial(_hgrn_kernel, reverse=reverse, tt=tt),
        out_shape=[jax.ShapeDtypeStruct((batch, seq, HW), F32),
                   jax.ShapeDtypeStruct((batch, HEADS, HEAD_DIM, HEAD_DIM), F32)],
        grid=(batch, nt),
        in_specs=[tok, tok, tok, state],
        out_specs=[tok, state],
        scratch_shapes=[pltpu.VMEM((HEADS, HEAD_DIM, HEAD_DIM), F32)],
        compiler_params=_cparams("arbitrary", "arbitrary"),
        name="hgrn_bwd" if reverse else "hgrn_fwd",
    )(q3, i3, lf3, s0)
    return o.reshape(t, HW), sfin


def _route(logits):
    lane = lax.broadcasted_iota(jnp.int32, logits.shape, 1).astype(F32)
    neg = -jnp.inf
    is_group = lane < N_GROUPS
    gl = jnp.where(is_group, logits, neg)
    gmax = jnp.max(gl, axis=-1, keepdims=True)
    g_sel = jnp.min(jnp.where(gl == gmax, lane, float(ROUTE_LANES)), axis=-1, keepdims=True)
    den = jnp.sum(jnp.where(is_group, jnp.exp(logits - gmax), 0.0), axis=-1, keepdims=True)
    p_sel = 1.0 / den
    first = N_GROUPS + EXPERTS_PER_GROUP * g_sel
    el = jnp.where((lane >= first) & (lane < first + EXPERTS_PER_GROUP), logits, neg)
    t1 = jnp.max(el, axis=-1, keepdims=True)
    i1 = jnp.min(jnp.where(el == t1, lane, float(ROUTE_LANES)), axis=-1, keepdims=True)
    el2 = jnp.where(lane == i1, neg, el)
    t2 = jnp.max(el2, axis=-1, keepdims=True)
    i2 = jnp.min(jnp.where(el2 == t2, lane, float(ROUTE_LANES)), axis=-1, keepdims=True)
    e2 = jnp.exp(t2 - t1)
    w1 = p_sel / (1.0 + e2)
    w2 = p_sel * e2 / (1.0 + e2)
    rec = jnp.where(lane == 0.0, i1 - N_GROUPS, 0.0)
    rec = jnp.where(lane == 1.0, i2 - N_GROUPS, rec)
    rec = jnp.where(lane == 2.0, w1, rec)
    return jnp.where(lane == 3.0, w2, rec)


def _mixer_kernel(of_ref, ob_ref, sg_ref, gu_ref, vn_ref, x_ref, hgain_ref, ws_ref, bs_ref, wo_ref,
                  g1_ref, n2_ref, sh2_ref, sc2_ref, wrh_ref, wrl_ref, br_ref,
                  xo_ref, h2_ref, rt_ref, cat_ref, *, tm):
    o = of_ref[...] + ob_ref[...]
    for h in range(HEADS):
        sl = slice(h * HEAD_DIM, (h + 1) * HEAD_DIM)
        hg = _rms(o[:, sl]) * hgain_ref[:, sl] * sg_ref[:, sl].astype(F32)
        cat_ref[:, sl] = hg.astype(BF16)
    for cc in range(tm // SGU_CHUNK):
        rows = slice(cc * SGU_CHUNK, (cc + 1) * SGU_CHUNK)
        for h in range(HEADS):
            sl = slice(h * HEAD_DIM, (h + 1) * HEAD_DIM)
            mixed = _dot(ws_ref[h], vn_ref[rows, sl]) + bs_ref[h]
            cat_ref[rows, HW + h * HEAD_DIM:HW + (h + 1) * HEAD_DIM] = (
                gu_ref[rows, sl].astype(F32) * mixed).astype(BF16)
    xn = x_ref[...] + g1_ref[...] * _dot(cat_ref[...], wo_ref[...])
    xo_ref[...] = xn
    h2 = _rms(xn) * n2_ref[...]
    h2 = h2 * (1.0 + sc2_ref[...]) + sh2_ref[...]
    h2_ref[...] = h2
    hi, lo = _split2(h2)
    logits = _dot(hi, wrh_ref[...]) + (_dot(lo, wrh_ref[...]) + _dot(hi, wrl_ref[...])) + br_ref[...]
    rt_ref[...] = _route(logits)


def _mixer(o_f, o_b, sg, gu, vn, x2d, mod3, tokens_per_mod_row, hgain, w_s, b_s, w_out, n2, wr_hi, wr_lo, br):
    t, d = x2d.shape
    tm = min(256, t)
    tpr = tokens_per_mod_row // tm
    row = lambda j: pl.BlockSpec((None, 1, d), lambda i: (i // tpr, 0, j))
    full = lambda a: pl.BlockSpec(a.shape, lambda i: (0,) * a.ndim)
    tok = pl.BlockSpec((tm, HW), lambda i: (i, 0))
    wide = pl.BlockSpec((tm, d), lambda i: (i, 0))
    rt = pl.BlockSpec((tm, ROUTE_LANES), lambda i: (i, 0))
    return pl.pallas_call(
        functools.partial(_mixer_kernel, tm=tm),
        out_shape=[jax.ShapeDtypeStruct((t, d), F32), jax.ShapeDtypeStruct((t, d), F32),
                   jax.ShapeDtypeStruct((t, ROUTE_LANES), F32)],
        grid=(t // tm,),
        in_specs=[tok, tok, tok, tok, tok, wide, full(hgain), full(w_s), full(b_s), full(w_out),
                  row(2), full(n2), row(3), row(4), full(wr_hi), full(wr_lo), full(br)],
        out_specs=[wide, wide, rt],
        scratch_shapes=[pltpu.VMEM((tm, 2 * HW), BF16)],
        compiler_params=_cparams("arbitrary"),
        name="mixer",
    )(o_f, o_b, sg, gu, vn, x2d, hgain, w_s, b_s, w_out, mod3, n2, mod3, mod3, wr_hi, wr_lo, br)


MOE_ROWS = 256
IDX_SLOTS = 3


def _moe_kernel(nvalid_ref, bexp_ref, rows_hbm, h_hbm, wg_ref, wu_ref, wd_ref, y_hbm,
                idx_ref, xbuf, ybuf, sem_idx, sem_g, sem_s):
    del bexp_ref
    i = pl.program_id(0)
    n = pl.num_programs(0)

    def idx_copy(blk):
        s = blk % IDX_SLOTS
        return pltpu.make_async_copy(rows_hbm.at[blk], idx_ref.at[s], sem_idx.at[s])

    def gather_copy(s, r, tok):
        return pltpu.make_async_copy(h_hbm.at[pl.ds(tok, 1)], xbuf.at[s, pl.ds(r, 1)], sem_g.at[s])

    def scatter_copy(s, r, slot):
        return pltpu.make_async_copy(ybuf.at[s, pl.ds(r, 1)], y_hbm.at[pl.ds(slot, 1)], sem_s.at[s])

    def gather_start(blk):
        s, si = blk % 2, blk % IDX_SLOTS

        def body(r, c):
            gather_copy(s, r, lax.shift_right_logical(idx_ref[si, r], 1)).start()
            return c

        lax.fori_loop(0, nvalid_ref[blk], body, 0)

    def gather_wait(blk):
        s = blk % 2

        def body(r, c):
            gather_copy(s, r, 0).wait()
            return c

        lax.fori_loop(0, nvalid_ref[blk], body, 0)

    def scatter_start(blk):
        s, si = blk % 2, blk % IDX_SLOTS

        def body(r, c):
            scatter_copy(s, r, idx_ref[si, r]).start()
            return c

        lax.fori_loop(0, nvalid_ref[blk], body, 0)

    def scatter_wait(blk):
        s = blk % 2

        def body(r, c):
            scatter_copy(s, r, 0).wait()
            return c

        lax.fori_loop(0, nvalid_ref[blk], body, 0)

    @pl.when(i == 0)
    def _():
        xbuf[...] = jnp.zeros_like(xbuf)
        idx_copy(0).start()
        idx_copy(0).wait()
        gather_start(0)

        @pl.when(n > 1)
        def _():
            idx_copy(1).start()

    @pl.when(i + 2 < n)
    def _():
        idx_copy(i + 2).start()

    @pl.when(i + 1 < n)
    def _():
        idx_copy(i + 1).wait()
        gather_start(i + 1)

    gather_wait(i)

    @pl.when(i >= 2)
    def _():
        scatter_wait(i - 2)

    s = i % 2

    @pl.when(nvalid_ref[i] > 0)
    def _():
        xb = xbuf[s].astype(BF16)
        hmid = (_silu(_dot(xb, wg_ref[...])) * _dot(xb, wu_ref[...])).astype(BF16)
        ybuf[s] = _dot(hmid, wd_ref[...])

    scatter_start(i)

    @pl.when(i == n - 1)
    def _():
        @pl.when(n > 1)
        def _():
            scatter_wait(i - 1)

        scatter_wait(i)


def _moe(h2, rows, nvalid, bexp, w_gate, w_up, w_down):
    ttot, d = h2.shape
    n_blocks = rows.shape[0]
    de = w_gate.shape[-1]
    grid_spec = pltpu.PrefetchScalarGridSpec(
        num_scalar_prefetch=2,
        grid=(n_blocks,),
        in_specs=[
            pl.BlockSpec(memory_space=pl.ANY),
            pl.BlockSpec(memory_space=pl.ANY),
            pl.BlockSpec((None, d, de), lambda i, nv, be: (be[i], 0, 0)),
            pl.BlockSpec((None, d, de), lambda i, nv, be: (be[i], 0, 0)),
            pl.BlockSpec((None, de, d), lambda i, nv, be: (be[i], 0, 0)),
        ],
        out_specs=pl.BlockSpec(memory_space=pl.ANY),
        scratch_shapes=[
            pltpu.SMEM((IDX_SLOTS, MOE_ROWS), jnp.int32),
            pltpu.VMEM((2, MOE_ROWS, d), F32),
            pltpu.VMEM((2, MOE_ROWS, d), F32),
            pltpu.SemaphoreType.DMA((IDX_SLOTS,)),
            pltpu.SemaphoreType.DMA((2,)),
            pltpu.SemaphoreType.DMA((2,)),
        ],
    )
    return pl.pallas_call(
        _moe_kernel,
        out_shape=jax.ShapeDtypeStruct((ttot * TOP_K, d), F32),
        grid_spec=grid_spec,
        compiler_params=_cparams("arbitrary"),
        name="moe",
    )(nvalid, bexp, rows, h2, w_gate, w_up, w_down)


def _dispatch_tables(expert_flat):
    n_slots = expert_flat.shape[0]
    n_blocks = -(-n_slots // MOE_ROWS) + N_EXPERTS
    e_sorted, order = lax.sort_key_val(expert_flat, lax.iota(jnp.int32, n_slots))
    del e_sorted
    counts = jnp.sum(expert_flat[:, None] == jnp.arange(N_EXPERTS, dtype=jnp.int32)[None, :], axis=0,
                     dtype=jnp.int32)
    padded = (counts + MOE_ROWS - 1) // MOE_ROWS * MOE_ROWS
    pad_end = jnp.cumsum(padded)
    pad_start = pad_end - padded
    start = jnp.cumsum(counts) - counts
    blk_row0 = jnp.arange(n_blocks, dtype=jnp.int32) * MOE_ROWS
    bexp = jnp.minimum(jnp.searchsorted(pad_end, blk_row0, side="right"), N_EXPERTS - 1).astype(jnp.int32)
    off0 = blk_row0 - pad_start[bexp]
    nvalid = jnp.clip(counts[bexp] - off0, 0, MOE_ROWS).astype(jnp.int32)
    off = off0[:, None] + jnp.arange(MOE_ROWS, dtype=jnp.int32)[None, :]
    src = jnp.clip(start[bexp][:, None] + off, 0, n_slots - 1)
    rows = jnp.where(off < counts[bexp][:, None], order[src], 0).astype(jnp.int32)
    return rows, nvalid, bexp


def _combine_kernel(x_ref, y_ref, rt_ref, g2_ref, fn_ref, o_ref, *, final, d):
    w = rt_ref[...]
    f = w[:, 2:3] * y_ref[:, :d] + w[:, 3:4] * y_ref[:, d:]
    xn = x_ref[...] + g2_ref[...] * f
    if final:
        xn = _rms(xn) * fn_ref[...]
    o_ref[...] = xn


def _combine(x2d, y2, y_row0, route, mod3, tokens_per_mod_row, final_norm, final):
    t, d = x2d.shape
    tm = min(256, t)
    tpr = tokens_per_mod_row // tm
    yb0 = y_row0 // tm
    return pl.pallas_call(
        functools.partial(_combine_kernel, final=final, d=d),
        out_shape=jax.ShapeDtypeStruct((t, d), F32),
        grid=(t // tm,),
        in_specs=[pl.BlockSpec((tm, d), lambda i: (i, 0)),
                  pl.BlockSpec((tm, TOP_K * d), lambda i: (i + yb0, 0)),
                  pl.BlockSpec((tm, ROUTE_LANES), lambda i: (i, 0)),
                  pl.BlockSpec((None, 1, d), lambda i: (i // tpr, 0, 5)),
                  pl.BlockSpec(final_norm.shape, lambda i: (0, 0))],
        out_specs=pl.BlockSpec((tm, d), lambda i: (i, 0)),
        compiler_params=_cparams("arbitrary"),
        name="combine",
    )(x2d, y2, route, mod3, final_norm)


def kernel(x, c, ctx, c_ctx, norm1, norm2, w_mod, b_mod, w_in, lb_logits, hgrn_norm, sgu_norm, sgu_w, sgu_b,
           w_out, w_group, b_group, w_router, b_router, w_gate, w_up, w_down, final_norm):
    b, l, d = x.shape
    lc = ctx.shape[1]
    depth = w_mod.shape[0]
    t_lat, t_ctx = b * l, b * lc

    lb_cum = jnp.cumsum(jax.nn.softmax(lb_logits.astype(F32), axis=0), axis=0)
    lower_bound = jnp.maximum(lb_cum - lb_cum[0:1], 0.0)

    cc = jnp.zeros((MOD_ROWS, d), F32).at[:b].set(c).at[b].set(c_ctx)
    mod = _modulation(cc, w_mod, b_mod)

    w_route = jnp.concatenate([w_group, w_router], axis=-1)
    w_route = jnp.pad(w_route, ((0, 0), (0, 0), (0, ROUTE_LANES - w_route.shape[-1])))
    wr_hi = w_route.astype(BF16)
    wr_lo = (w_route - wr_hi.astype(F32)).astype(BF16)
    b_route = jnp.concatenate([b_group, b_router], axis=-1)
    b_route = jnp.pad(b_route, ((0, 0), (0, ROUTE_LANES - b_route.shape[-1])))[:, None, :]
    b_s = jnp.broadcast_to(sgu_b[..., None], sgu_b.shape + (HEAD_DIM,)).astype(F32)

    w_in_b, w_out_b, sgu_w_b = w_in.astype(BF16), w_out.astype(BF16), sgu_w.astype(BF16)
    w_gate_b, w_up_b, w_down_b = w_gate.astype(BF16), w_up.astype(BF16), w_down.astype(BF16)

    xs = x.reshape(t_lat, d)
    ys = ctx.reshape(t_ctx, d)
    s_zero = jnp.zeros((b, HEADS, HEAD_DIM, HEAD_DIM), F32)
    fn = final_norm.reshape(1, d)

    for layer in range(depth):
        last = layer == depth - 1
        mod_lat = mod[layer, :b].reshape(b, 1, N_MOD * d)
        mod_ctx = mod[layer, b:b + 1].reshape(1, 1, N_MOD * d)
        n1 = norm1[layer].reshape(1, d)
        n2 = norm2[layer].reshape(1, d)
        lb = lower_bound[layer]
        sgn = sgu_norm[layer].reshape(1, HW)
        hgain = hgrn_norm[layer].reshape(1, HW)
        mix_w = (hgain, sgu_w_b[layer], b_s[layer], w_out_b[layer], n2, wr_hi[layer], wr_lo[layer],
                 b_route[layer])

        cq, ci, clf_f, clf_b, csg, cgu, cvn = _inproj(ys, mod_ctx, t_ctx, n1, w_in_b[layer], lb, sgn)
        co_f, s_f = _hgrn(cq, ci, clf_f, s_zero, b, reverse=False)
        co_b, s_b = _hgrn(cq, ci, clf_b, s_zero, b, reverse=True)

        q, i, lf_f, lf_b, sg, gu, vn = _inproj(xs, mod_lat, l, n1, w_in_b[layer], lb, sgn)
        o_f, _ = _hgrn(q, i, lf_f, s_f, b, reverse=False)
        o_b, _ = _hgrn(q, i, lf_b, s_b, b, reverse=True)
        xs, h2, route = _mixer(o_f, o_b, sg, gu, vn, xs, mod_lat, l, *mix_w)

        if not last:
            ys, h2c, route_c = _mixer(co_f, co_b, csg, cgu, cvn, ys, mod_ctx, t_ctx, *mix_w)
            h2 = jnp.concatenate([h2, h2c], axis=0)
            route_all = jnp.concatenate([route, route_c], axis=0)
        else:
            route_all = route

        expert_flat = route_all[:, :TOP_K].astype(jnp.int32).reshape(-1)
        rows, nvalid, bexp = _dispatch_tables(expert_flat)
        y2 = _moe(h2, rows, nvalid, bexp, w_gate_b[layer], w_up_b[layer], w_down_b[layer])
        y2 = y2.reshape(-1, TOP_K * d)

        xs = _combine(xs, y2, 0, route, mod_lat, l, fn, final=last)
        if not last:
            ys = _combine(ys, y2, t_lat, route_c, mod_ctx, t_ctx, fn, final=False)

    return xs.reshape(b, l, d)
```

```python
import functools

import jax
import jax.numpy as jnp
from jax import lax
from jax.experimental import pallas as pl
from jax.experimental.pallas import tpu as pltpu

F32 = jnp.float32
BF16 = jnp.bfloat16

EPS = 1e-6
HEADS = 4
HEAD_DIM = 128
HW = HEADS * HEAD_DIM
HGRN_CHUNK = 64
SGU_CHUNK = 128
N_GROUPS = 4
EXPERTS_PER_GROUP = 8
N_EXPERTS = N_GROUPS * EXPERTS_PER_GROUP
TOP_K = 2
N_MOD = 6
ROUTE_LANES = 128
MOD_ROWS = 16

VMEM_LIMIT = 48 * 1024 * 1024


def _cparams(*sem):
    return pltpu.CompilerParams(dimension_semantics=sem, vmem_limit_bytes=VMEM_LIMIT)


def _split2(a):
    hi = a.astype(BF16)
    lo = (a - hi.astype(F32)).astype(BF16)
    return hi, lo


def _dot(a, b):
    return jnp.dot(a, b, preferred_element_type=F32)


def _dot_nt(a, b):
    return lax.dot_general(a, b, (((1,), (1,)), ((), ())), preferred_element_type=F32)


def _dot_tn(a, b):
    return lax.dot_general(a, b, (((0,), (0,)), ((), ())), preferred_element_type=F32)


def _dot3(a, b):
    ah, al = _split2(a)
    bh, bl = _split2(b)
    return _dot(ah, bh) + (_dot(al, bh) + _dot(ah, bl))


def _silu(x):
    return x / (1.0 + jnp.exp(-x))


def _rms(x):
    return x * lax.rsqrt(jnp.mean(x * x, axis=-1, keepdims=True) + EPS)


def _mod_kernel(c_ref, w_ref, b_ref, o_ref):
    o_ref[...] = _dot3(_silu(c_ref[...]), w_ref[...]) + b_ref[...]


def _modulation(cc, w_mod, b_mod):
    depth, d, n = w_mod.shape
    tn = 1536
    return pl.pallas_call(
        _mod_kernel,
        out_shape=jax.ShapeDtypeStruct((depth, MOD_ROWS, n), F32),
        grid=(depth, n // tn),
        in_specs=[
            pl.BlockSpec((MOD_ROWS, d), lambda l, j: (0, 0)),
            pl.BlockSpec((None, d, tn), lambda l, j: (l, 0, j)),
            pl.BlockSpec((None, 1, tn), lambda l, j: (l, 0, j)),
        ],
        out_specs=pl.BlockSpec((None, MOD_ROWS, tn), lambda l, j: (l, 0, j)),
        compiler_params=_cparams("arbitrary", "arbitrary"),
        name="modulation",
    )(cc, w_mod, b_mod.reshape(depth, 1, n))


def _log_forget(z, lb):
    ls = jnp.minimum(z, 0.0) - jnp.log1p(jnp.exp(-jnp.abs(z)))
    a = jnp.log1p(-lb) + ls
    b = jnp.log(lb)
    return jnp.maximum(a, b) + jnp.log1p(jnp.exp(-jnp.abs(a - b)))


def _inproj_kernel(x_ref, sh_ref, sc_ref, n1_ref, w_ref, lb_ref, sgn_ref,
                   q_ref, i_ref, lff_ref, lfb_ref, sg_ref, gu_ref, vn_ref):
    h = _rms(x_ref[...]) * n1_ref[...]
    h = h * (1.0 + sc_ref[...]) + sh_ref[...]
    hb = h.astype(BF16)

    def proj(j):
        return _dot(hb, w_ref[:, j * HW:(j + 1) * HW])

    q_ref[...] = proj(0).astype(BF16)
    lff_ref[...] = _log_forget(proj(1), lb_ref[0:1, :])
    lfb_ref[...] = _log_forget(proj(2), lb_ref[1:2, :])
    i_ref[...] = proj(3).astype(BF16)
    sg_ref[...] = _silu(proj(4)).astype(BF16)
    gu_ref[...] = jax.nn.gelu(proj(5)).astype(BF16)
    vn_ref[...] = (_rms(jax.nn.gelu(proj(6))) * sgn_ref[...]).astype(BF16)


def _inproj(x2d, mod3, tokens_per_mod_row, n1, w_in, lb, sgu_gain):
    t, d = x2d.shape
    tm = min(512, t)
    tpr = tokens_per_mod_row // tm
    row = lambda j: pl.BlockSpec((None, 1, d), lambda i: (i // tpr, 0, j))
    full = lambda a: pl.BlockSpec(a.shape, lambda i: (0,) * a.ndim)
    tok = pl.BlockSpec((tm, HW), lambda i: (i, 0))
    return pl.pallas_call(
        _inproj_kernel,
        out_shape=[jax.ShapeDtypeStruct((t, HW), dt) for dt in (BF16, BF16, F32, F32, BF16, BF16, BF16)],
        grid=(t // tm,),
        in_specs=[pl.BlockSpec((tm, d), lambda i: (i, 0)), row(0), row(1),
                  full(n1), full(w_in), full(lb), full(sgu_gain)],
        out_specs=[tok] * 7,
        compiler_params=_cparams("arbitrary"),
        name="inproj",
    )(x2d, mod3, mod3, n1, w_in, lb, sgu_gain)


def _hgrn_kernel(q_ref, i_ref, lf_ref, s0_ref, o_ref, sfin_ref, st_ref, *, reverse, tt):
    t = pl.program_id(1)

    @pl.when(t == 0)
    def _():
        st_ref[...] = s0_ref[...]

    c = HGRN_CHUNK
    nchunks = tt // c
    rows = lax.broadcasted_iota(jnp.int32, (c, c), 0)
    cols = lax.broadcasted_iota(jnp.int32, (c, c), 1)
    incl = (cols >= rows) if reverse else (cols <= rows)
    tri = jnp.where(incl, 1.0, 0.0).astype(BF16)
    ref_row = c // 2 if reverse else c // 2 - 1
    tot_row = 0 if reverse else c - 1

    def chunk(ci, carry):
        cidx = (nchunks - 1 - ci) if reverse else ci
        r0 = pl.multiple_of(cidx * c, c)
        lf = lf_ref[pl.ds(r0, c), :]
        hi = lf.astype(BF16)
        r1 = lf - hi.astype(F32)
        mid = r1.astype(BF16)
        lo = (r1 - mid.astype(F32)).astype(BF16)
        cum = _dot(tri, hi) + (_dot(tri, mid) + _dot(tri, lo))
        ref = cum[ref_row:ref_row + 1, :]
        tot = cum[tot_row:tot_row + 1, :]
        k = 1.0 - jnp.exp(lf)
        qf = q_ref[pl.ds(r0, c), :].astype(F32)
        iv = i_ref[pl.ds(r0, c), :]
        q_in = (qf * jnp.exp(cum - ref)).astype(BF16)
        k_in = (k * jnp.exp(ref - cum)).astype(BF16)
        k_st = (k * jnp.exp(tot - cum)).astype(BF16)
        q_st = (qf * jnp.exp(cum)).astype(BF16)
        dec = jnp.exp(tot)
        for h in range(HEADS):
            sl = slice(h * HEAD_DIM, (h + 1) * HEAD_DIM)
            sc = _dot_nt(q_in[:, sl], k_in[:, sl])
            sc = jnp.where(incl, sc, 0.0).astype(BF16)
            st = st_ref[h]
            o_h = _dot(sc, iv[:, sl]) + _dot_nt(q_st[:, sl], st.astype(BF16))
            o_ref[pl.ds(r0, c), sl] = o_h
            st_ref[h] = st * dec[:, sl] + _dot_tn(iv[:, sl], k_st[:, sl])
        return carry

    lax.fori_loop(0, nchunks, chunk, 0)

    @pl.when(t == pl.num_programs(1) - 1)
    def _():
        sfin_ref[...] = st_ref[...]


def _hgrn(q, i, lf, s0, batch, reverse):
    t = q.shape[0]
    seq = t // batch
    tt = min(512, seq)
    nt = seq // tt
    q3, i3, lf3 = (a.reshape(batch, seq, HW) for a in (q, i, lf))
    tile = (lambda b, s: (b, nt - 1 - s, 0)) if reverse else (lambda b, s: (b, s, 0))
    tok = pl.BlockSpec((None, tt, HW), tile)
    state = pl.BlockSpec((None, HEADS, HEAD_DIM, HEAD_DIM), lambda b, s: (b, 0, 0, 0))
    o, sfin = pl.pallas_call(
        functools.partial(_hgrn_kernel, reverse=reverse, tt=tt),
        out_shape=[jax.ShapeDtypeStruct((batch, seq, HW), F32),
                   jax.ShapeDtypeStruct((batch, HEADS, HEAD_DIM, HEAD_DIM), F32)],
        grid=(batch, nt),
        in_specs=[tok, tok, tok, state],
        out_specs=[tok, state],
        scratch_shapes=[pltpu.VMEM((HEADS, HEAD_DIM, HEAD_DIM), F32)],
        compiler_params=_cparams("arbitrary", "arbitrary"),
        name="hgrn_bwd" if reverse else "hgrn_fwd",
    )(q3, i3, lf3, s0)
    return o.reshape(t, HW), sfin


def _route(logits):
    lane = lax.broadcasted_iota(jnp.int32, logits.shape, 1).astype(F32)
    neg = -jnp.inf
    is_group = lane < N_GROUPS
    gl = jnp.where(is_group, logits, neg)
    gmax = jnp.max(gl, axis=-1, keepdims=True)
    g_sel = jnp.min(jnp.where(gl == gmax, lane, float(ROUTE_LANES)), axis=-1, keepdims=True)
    den = jnp.sum(jnp.where(is_group, jnp.exp(logits - gmax), 0.0), axis=-1, keepdims=True)
    p_sel = 1.0 / den
    first = N_GROUPS + EXPERTS_PER_GROUP * g_sel
    el = jnp.where((lane >= first) & (lane < first + EXPERTS_PER_GROUP), logits, neg)
    t1 = jnp.max(el, axis=-1, keepdims=True)
    i1 = jnp.min(jnp.where(el == t1, lane, float(ROUTE_LANES)), axis=-1, keepdims=True)
    el2 = jnp.where(lane == i1, neg, el)
    t2 = jnp.max(el2, axis=-1, keepdims=True)
    i2 = jnp.min(jnp.where(el2 == t2, lane, float(ROUTE_LANES)), axis=-1, keepdims=True)
    e2 = jnp.exp(t2 - t1)
    w1 = p_sel / (1.0 + e2)
    w2 = p_sel * e2 / (1.0 + e2)
    rec = jnp.where(lane == 0.0, i1 - N_GROUPS, 0.0)
    rec = jnp.where(lane == 1.0, i2 - N_GROUPS, rec)
    rec = jnp.where(lane == 2.0, w1, rec)
    return jnp.where(lane == 3.0, w2, rec)


def _mixer_kernel(of_ref, ob_ref, sg_ref, gu_ref, vn_ref, x_ref, hgain_ref, ws_ref, bs_ref, wo_ref,
                  g1_ref, n2_ref, sh2_ref, sc2_ref, wrh_ref, wrl_ref, br_ref,
                  xo_ref, h2_ref, rt_ref, cat_ref, *, tm):
    o = of_ref[...] + ob_ref[...]
    for h in range(HEADS):
        sl = slice(h * HEAD_DIM, (h + 1) * HEAD_DIM)
        hg = _rms(o[:, sl]) * hgain_ref[:, sl] * sg_ref[:, sl].astype(F32)
        cat_ref[:, sl] = hg.astype(BF16)
    for cc in range(tm // SGU_CHUNK):
        rows = slice(cc * SGU_CHUNK, (cc + 1) * SGU_CHUNK)
        for h in range(HEADS):
            sl = slice(h * HEAD_DIM, (h + 1) * HEAD_DIM)
            mixed = _dot(ws_ref[h], vn_ref[rows, sl]) + bs_ref[h]
            cat_ref[rows, HW + h * HEAD_DIM:HW + (h + 1) * HEAD_DIM] = (
                gu_ref[rows, sl].astype(F32) * mixed).astype(BF16)
    xn = x_ref[...] + g1_ref[...] * _dot(cat_ref[...], wo_ref[...])
    xo_ref[...] = xn
    h2 = _rms(xn) * n2_ref[...]
    h2 = h2 * (1.0 + sc2_ref[...]) + sh2_ref[...]
    h2_ref[...] = h2
    hi, lo = _split2(h2)
    logits = _dot(hi, wrh_ref[...]) + (_dot(lo, wrh_ref[...]) + _dot(hi, wrl_ref[...])) + br_ref[...]
    rt_ref[...] = _route(logits)


def _mixer(o_f, o_b, sg, gu, vn, x2d, mod3, tokens_per_mod_row, hgain, w_s, b_s, w_out, n2, wr_hi, wr_lo, br):
    t, d = x2d.shape
    tm = min(256, t)
    tpr = tokens_per_mod_row // tm
    row = lambda j: pl.BlockSpec((None, 1, d), lambda i: (i // tpr, 0, j))
    full = lambda a: pl.BlockSpec(a.shape, lambda i: (0,) * a.ndim)
    tok = pl.BlockSpec((tm, HW), lambda i: (i, 0))
    wide = pl.BlockSpec((tm, d), lambda i: (i, 0))
    rt = pl.BlockSpec((tm, ROUTE_LANES), lambda i: (i, 0))
    return pl.pallas_call(
        functools.partial(_mixer_kernel, tm=tm),
        out_shape=[jax.ShapeDtypeStruct((t, d), F32), jax.ShapeDtypeStruct((t, d), F32),
                   jax.ShapeDtypeStruct((t, ROUTE_LANES), F32)],
        grid=(t // tm,),
        in_specs=[tok, tok, tok, tok, tok, wide, full(hgain), full(w_s), full(b_s), full(w_out),
                  row(2), full(n2), row(3), row(4), full(wr_hi), full(wr_lo), full(br)],
        out_specs=[wide, wide, rt],
        scratch_shapes=[pltpu.VMEM((tm, 2 * HW), BF16)],
        compiler_params=_cparams("arbitrary"),
        name="mixer",
    )(o_f, o_b, sg, gu, vn, x2d, hgain, w_s, b_s, w_out, mod3, n2, mod3, mod3, wr_hi, wr_lo, br)


MOE_ROWS = 256
MOE_LAG = 3
RING = 3
DMA_GROUPS = 8


def _moe_kernel(last_ref, sexp_ref, tab_hbm, h_hbm, wg_ref, wu_ref, wd_ref, y_hbm,
                idx_ref, xbuf, ybuf, zbuf, xb_ref, hm_ref, wgb, wub, wdb, sem_idx, sem_g, sem_s, sem_z):
    i = pl.program_id(0)
    last = last_ref[0]
    bm = MOE_ROWS
    de = wgb.shape[1]
    d = wgb.shape[0]

    def idx_copy(step):
        s = step % 2
        return pltpu.make_async_copy(tab_hbm.at[step], idx_ref.at[s], sem_idx.at[s])

    def gathered(slot):
        return pltpu.make_async_copy(h_hbm.at[pl.ds(0, bm)], xbuf.at[slot], sem_g.at[slot])

    def scattered(slot):
        return pltpu.make_async_copy(ybuf.at[slot], y_hbm.at[pl.ds(0, bm)], sem_s.at[slot])

    @pl.when(i <= last)
    def _():
        islot = i % 2
        gslot = i % RING
        cslot = (i + 1) % RING
        sslot = i % RING

        @pl.when(i == 0)
        def _():
            xbuf[...] = jnp.zeros_like(xbuf)
            ybuf[...] = jnp.zeros_like(ybuf)
            zbuf[...] = jnp.zeros_like(zbuf)
            idx_copy(0).start()

        idx_copy(i).wait()

        @pl.when(i < last)
        def _():
            idx_copy(i + 1).start()

        @pl.when(i >= 2)
        def _():
            gathered(cslot).wait()
            scattered(cslot).wait()

        @pl.when((i == 0) | (sexp_ref[i] != sexp_ref[jnp.maximum(i - 1, 0)]))
        def _():
            wgb[...] = wg_ref[...].astype(BF16)
            wub[...] = wu_ref[...].astype(BF16)
            wdb[...] = wd_ref[...].astype(BF16)

        per = bm // (DMA_GROUPS // 2)

        def scatter_group(g):
            for r in range(g * per, (g + 1) * per):
                pltpu.make_async_copy(ybuf.at[sslot, pl.ds(r, 1)], y_hbm.at[pl.ds(idx_ref[islot, 1, r], 1)],
                                      sem_s.at[sslot]).start()

        def gather_group(g):
            for r in range(g * per, (g + 1) * per):
                pltpu.make_async_copy(h_hbm.at[pl.ds(idx_ref[islot, 0, r], 1)], xbuf.at[gslot, pl.ds(r, 1)],
                                      sem_g.at[gslot]).start()

        dma_groups = [functools.partial(scatter_group, g) for g in range(DMA_GROUPS // 2)]
        dma_groups += [functools.partial(gather_group, g) for g in range(DMA_GROUPS // 2)]

        xb_ref[...] = xbuf[cslot].astype(BF16)
        nh = 2
        for j in range(nh):
            cs = slice(j * de // nh, (j + 1) * de // nh)
            dma_groups.pop(0)()
            gate = _dot(xb_ref[...], wgb[:, cs])
            dma_groups.pop(0)()
            hm_ref[:, cs] = (_silu(gate) * _dot(xb_ref[...], wub[:, cs])).astype(BF16)
        nd = 4
        for j in range(nd):
            cs = slice(j * d // nd, (j + 1) * d // nd)
            dma_groups.pop(0)()
            ybuf[cslot, :, cs] = _dot(hm_ref[...], wdb[:, cs])
        assert not dma_groups

        @pl.when(i == last)
        def _():
            gathered(gslot).wait()
            gathered((i + 2) % RING).wait()
            scattered(sslot).wait()
            scattered((i + 2) % RING).wait()

    @pl.when(i > last)
    def _():
        fill = pltpu.make_async_copy(zbuf, y_hbm.at[pl.ds((i - MOE_LAG) * bm, bm)], sem_z.at[0])
        fill.start()
        fill.wait()


def _moe(h2, tab, last, sexp, n_out_rows, w_gate, w_up, w_down):
    ttot, d = h2.shape
    n_steps = tab.shape[0]
    de = w_gate.shape[-1]
    grid_spec = pltpu.PrefetchScalarGridSpec(
        num_scalar_prefetch=2,
        grid=(n_steps,),
        in_specs=[
            pl.BlockSpec(memory_space=pl.ANY),
            pl.BlockSpec(memory_space=pl.ANY),
            pl.BlockSpec((None, d, de), lambda i, la, se: (se[i], 0, 0)),
            pl.BlockSpec((None, d, de), lambda i, la, se: (se[i], 0, 0)),
            pl.BlockSpec((None, de, d), lambda i, la, se: (se[i], 0, 0)),
        ],
        out_specs=pl.BlockSpec(memory_space=pl.ANY),
        scratch_shapes=[
            pltpu.SMEM((2, 2, MOE_ROWS), jnp.int32),
            pltpu.VMEM((RING, MOE_ROWS, d), F32),
            pltpu.VMEM((RING, MOE_ROWS, d), F32),
            pltpu.VMEM((MOE_ROWS, d), F32),
            pltpu.VMEM((MOE_ROWS, d), BF16),
            pltpu.VMEM((MOE_ROWS, de), BF16),
            pltpu.VMEM((d, de), BF16),
            pltpu.VMEM((d, de), BF16),
            pltpu.VMEM((de, d), BF16),
            pltpu.SemaphoreType.DMA((2,)),
            pltpu.SemaphoreType.DMA((RING,)),
            pltpu.SemaphoreType.DMA((RING,)),
            pltpu.SemaphoreType.DMA((1,)),
        ],
    )
    return pl.pallas_call(
        _moe_kernel,
        out_shape=jax.ShapeDtypeStruct((n_out_rows, d), F32),
        grid_spec=grid_spec,
        compiler_params=_cparams("arbitrary"),
        name="moe",
    )(last, sexp, tab, h2, w_gate, w_up, w_down)


def _dispatch_tables(expert_flat, ttot):
    bm = MOE_ROWS
    n_slots = expert_flat.shape[0]
    n_blocks = -(-n_slots // bm) + N_EXPERTS
    n_steps = n_blocks + MOE_LAG
    _, order = lax.sort_key_val(expert_flat, lax.iota(jnp.int32, n_slots))
    counts = jnp.sum(expert_flat[:, None] == jnp.arange(N_EXPERTS, dtype=jnp.int32)[None, :], axis=0,
                     dtype=jnp.int32)
    padded = (counts + bm - 1) // bm * bm
    pad_end = jnp.cumsum(padded)
    pad_start = pad_end - padded
    start = jnp.cumsum(counts) - counts
    blk_row0 = jnp.arange(n_blocks, dtype=jnp.int32) * bm
    bexp = jnp.minimum(jnp.sum(pad_end[None, :] <= blk_row0[:, None], axis=1), N_EXPERTS - 1).astype(jnp.int32)
    lane = jnp.arange(bm, dtype=jnp.int32)[None, :]
    off = (blk_row0 - pad_start[bexp])[:, None] + lane
    valid = off < counts[bexp][:, None]
    slot = order[jnp.clip(start[bexp][:, None] + off, 0, n_slots - 1)]
    pad_rank = blk_row0[:, None] + lane - (start[bexp] + counts[bexp])[:, None]
    gsrc = jnp.where(valid, slot % ttot, 0)
    sdst = jnp.where(valid, slot, n_slots + pad_rank)
    spare = n_blocks * bm + jnp.arange(MOE_LAG * bm, dtype=jnp.int32).reshape(MOE_LAG, bm)
    gtab = jnp.concatenate([gsrc, jnp.zeros((MOE_LAG, bm), jnp.int32)], axis=0)
    stab = jnp.concatenate([spare, sdst], axis=0)
    tab = jnp.stack([gtab, stab], axis=1).astype(jnp.int32)
    n_used = jnp.sum(padded) // bm
    last = (n_used + MOE_LAG - 1).astype(jnp.int32).reshape(1)
    sexp = bexp[jnp.clip(jnp.arange(n_steps) - (MOE_LAG - 1), 0, n_blocks - 1)]
    return tab, last, sexp, n_steps * bm


def _combine_kernel(x_ref, y0_ref, y1_ref, rt_ref, g2_ref, fn_ref, o_ref, *, final):
    w = rt_ref[...]
    f = w[:, 2:3] * y0_ref[...] + w[:, 3:4] * y1_ref[...]
    xn = x_ref[...] + g2_ref[...] * f
    if final:
        xn = _rms(xn) * fn_ref[...]
    o_ref[...] = xn


def _combine(x2d, y2, row0, ttot, route, mod3, tokens_per_mod_row, final_norm, final):
    t, d = x2d.shape
    tm = min(256, t)
    tpr = tokens_per_mod_row // tm
    b0, b1 = row0 // tm, (ttot + row0) // tm
    return pl.pallas_call(
        functools.partial(_combine_kernel, final=final),
        out_shape=jax.ShapeDtypeStruct((t, d), F32),
        grid=(t // tm,),
        in_specs=[pl.BlockSpec((tm, d), lambda i: (i, 0)),
                  pl.BlockSpec((tm, d), lambda i: (i + b0, 0)),
                  pl.BlockSpec((tm, d), lambda i: (i + b1, 0)),
                  pl.BlockSpec((tm, ROUTE_LANES), lambda i: (i, 0)),
                  pl.BlockSpec((None, 1, d), lambda i: (i // tpr, 0, 5)),
                  pl.BlockSpec(final_norm.shape, lambda i: (0, 0))],
        out_specs=pl.BlockSpec((tm, d), lambda i: (i, 0)),
        compiler_params=_cparams("arbitrary"),
        name="combine",
    )(x2d, y2, y2, route, mod3, final_norm)


def kernel(x, c, ctx, c_ctx, norm1, norm2, w_mod, b_mod, w_in, lb_logits, hgrn_norm, sgu_norm, sgu_w, sgu_b,
           w_out, w_group, b_group, w_router, b_router, w_gate, w_up, w_down, final_norm):
    b, l, d = x.shape
    lc = ctx.shape[1]
    depth = w_mod.shape[0]
    t_lat, t_ctx = b * l, b * lc

    lb_cum = jnp.cumsum(jax.nn.softmax(lb_logits.astype(F32), axis=0), axis=0)
    lower_bound = jnp.maximum(lb_cum - lb_cum[0:1], 0.0)

    cc = jnp.zeros((MOD_ROWS, d), F32).at[:b].set(c).at[b].set(c_ctx)
    mod = _modulation(cc, w_mod, b_mod)

    w_route = jnp.concatenate([w_group, w_router], axis=-1)
    w_route = jnp.pad(w_route, ((0, 0), (0, 0), (0, ROUTE_LANES - w_route.shape[-1])))
    wr_hi = w_route.astype(BF16)
    wr_lo = (w_route - wr_hi.astype(F32)).astype(BF16)
    b_route = jnp.concatenate([b_group, b_router], axis=-1)
    b_route = jnp.pad(b_route, ((0, 0), (0, ROUTE_LANES - b_route.shape[-1])))[:, None, :]
    b_s = jnp.broadcast_to(sgu_b[..., None], sgu_b.shape + (HEAD_DIM,)).astype(F32)

    w_in_b, w_out_b, sgu_w_b = w_in.astype(BF16), w_out.astype(BF16), sgu_w.astype(BF16)

    xs = x.reshape(t_lat, d)
    ys = ctx.reshape(t_ctx, d)
    s_zero = jnp.zeros((b, HEADS, HEAD_DIM, HEAD_DIM), F32)
    fn = final_norm.reshape(1, d)

    for layer in range(depth):
        last = layer == depth - 1
        mod_lat = mod[layer, :b].reshape(b, 1, N_MOD * d)
        mod_ctx = mod[layer, b:b + 1].reshape(1, 1, N_MOD * d)
        n1 = norm1[layer].reshape(1, d)
        n2 = norm2[layer].reshape(1, d)
        lb = lower_bound[layer]
        sgn = sgu_norm[layer].reshape(1, HW)
        hgain = hgrn_norm[layer].reshape(1, HW)
        mix_w = (hgain, sgu_w_b[layer], b_s[layer], w_out_b[layer], n2, wr_hi[layer], wr_lo[layer],
                 b_route[layer])

        cq, ci, clf_f, clf_b, csg, cgu, cvn = _inproj(ys, mod_ctx, t_ctx, n1, w_in_b[layer], lb, sgn)
        co_f, s_f = _hgrn(cq, ci, clf_f, s_zero, b, reverse=False)
        co_b, s_b = _hgrn(cq, ci, clf_b, s_zero, b, reverse=True)

        q, i, lf_f, lf_b, sg, gu, vn = _inproj(xs, mod_lat, l, n1, w_in_b[layer], lb, sgn)
        o_f, _ = _hgrn(q, i, lf_f, s_f, b, reverse=False)
        o_b, _ = _hgrn(q, i, lf_b, s_b, b, reverse=True)
        xs, h2, route = _mixer(o_f, o_b, sg, gu, vn, xs, mod_lat, l, *mix_w)

        if not last:
            ys, h2c, route_c = _mixer(co_f, co_b, csg, cgu, cvn, ys, mod_ctx, t_ctx, *mix_w)
            h2 = jnp.concatenate([h2, h2c], axis=0)
            route_all = jnp.concatenate([route, route_c], axis=0)
        else:
            route_all = route

        ttot = h2.shape[0]
        expert_flat = route_all[:, :TOP_K].astype(jnp.int32).T.reshape(-1)
        tab, last_step, sexp, n_out_rows = _dispatch_tables(expert_flat, ttot)
        y2 = _moe(h2, tab, last_step, sexp, n_out_rows, w_gate[layer], w_up[layer], w_down[layer])

        xs = _combine(xs, y2, 0, ttot, route, mod_lat, l, fn, final=last)
        if not last:
            ys = _combine(ys, y2, t_lat, ttot, route_c, mod_ctx, t_ctx, fn, final=False)

    return xs.reshape(b, l, d)
```

```python
import functools

import jax
import jax.numpy as jnp
from jax import lax
from jax.experimental import pallas as pl
from jax.experimental.pallas import tpu as pltpu

F32 = jnp.float32
BF16 = jnp.bfloat16

EPS = 1e-6
HEADS = 4
HEAD_DIM = 128
HW = HEADS * HEAD_DIM
HGRN_CHUNK = 64
SGU_CHUNK = 128
N_GROUPS = 4
EXPERTS_PER_GROUP = 8
N_EXPERTS = N_GROUPS * EXPERTS_PER_GROUP
TOP_K = 2
N_MOD = 6
LANES = 128
ROUTE_LANES = LANES
MOD_ROWS = 16

VMEM_LIMIT = 48 * 1024 * 1024


def _cparams(*sem):
    return pltpu.CompilerParams(dimension_semantics=sem, vmem_limit_bytes=VMEM_LIMIT)


def _split2(a):
    hi = a.astype(BF16)
    lo = (a - hi.astype(F32)).astype(BF16)
    return hi, lo


def _dot(a, b):
    return jnp.dot(a, b, preferred_element_type=F32)


def _dot_nt(a, b):
    return lax.dot_general(a, b, (((1,), (1,)), ((), ())), preferred_element_type=F32)


def _dot_tn(a, b):
    return lax.dot_general(a, b, (((0,), (0,)), ((), ())), preferred_element_type=F32)


def _dot3(a, b):
    ah, al = _split2(a)
    bh, bl = _split2(b)
    return _dot(ah, bh) + (_dot(al, bh) + _dot(ah, bl))


def _silu(x):
    return x / (1.0 + jnp.exp(-x))


def _store_token_tiles_cols(ref, x, j0, nd):
    rows, w = x.shape
    for j in range(w // LANES):
        ref[pl.ds(j0 + j, rows, stride=nd), :] = x[:, j * LANES:(j + 1) * LANES]


def _store_token_tiles(ref, x):
    _store_token_tiles_cols(ref, x, 0, x.shape[1] // LANES)


def _load_token_tiles(ref, rows, nd):
    return jnp.concatenate([ref[pl.ds(j, rows, stride=nd), :] for j in range(nd)], axis=-1)


def _rms(x):
    return x * lax.rsqrt(jnp.mean(x * x, axis=-1, keepdims=True) + EPS)


def _mod_kernel(c_ref, w_ref, b_ref, o_ref):
    o_ref[...] = _dot3(_silu(c_ref[...]), w_ref[...]) + b_ref[...]


def _modulation(cc, w_mod, b_mod):
    depth, d, n = w_mod.shape
    tn = 1536
    return pl.pallas_call(
        _mod_kernel,
        out_shape=jax.ShapeDtypeStruct((depth, MOD_ROWS, n), F32),
        grid=(depth, n // tn),
        in_specs=[
            pl.BlockSpec((MOD_ROWS, d), lambda l, j: (0, 0)),
            pl.BlockSpec((None, d, tn), lambda l, j: (l, 0, j)),
            pl.BlockSpec((None, 1, tn), lambda l, j: (l, 0, j)),
        ],
        out_specs=pl.BlockSpec((None, MOD_ROWS, tn), lambda l, j: (l, 0, j)),
        compiler_params=_cparams("arbitrary", "arbitrary"),
        name="modulation",
    )(cc, w_mod, b_mod.reshape(depth, 1, n))


def _log_forget(z, lb, lb_is_zero):
    ls = jnp.minimum(z, 0.0) - jnp.log(1.0 + jnp.exp(-jnp.abs(z)))
    if lb_is_zero:
        return ls
    a = jnp.log(1.0 - lb) + ls
    b = jnp.log(lb)
    return jnp.maximum(a, b) + jnp.log(1.0 + jnp.exp(-jnp.abs(a - b)))


def _inproj_kernel(x_ref, sh_ref, sc_ref, n1_ref, w_ref, lb_ref, sgn_ref,
                   q_ref, i_ref, lff_ref, lfb_ref, sg_ref, gu_ref, vn_ref, *, lb_is_zero):
    h = _rms(x_ref[...]) * n1_ref[...]
    h = h * (1.0 + sc_ref[...]) + sh_ref[...]
    hb = h.astype(BF16)

    def proj(j):
        return _dot(hb, w_ref[:, j * HW:(j + 1) * HW])

    q_ref[...] = proj(0).astype(BF16)
    lff_ref[...] = _log_forget(proj(1), lb_ref[0:1, :], lb_is_zero)
    lfb_ref[...] = _log_forget(proj(2), lb_ref[1:2, :], lb_is_zero)
    i_ref[...] = proj(3).astype(BF16)
    sg_ref[...] = _silu(proj(4)).astype(BF16)
    gu_ref[...] = jax.nn.gelu(proj(5)).astype(BF16)
    vn_ref[...] = (_rms(jax.nn.gelu(proj(6))) * sgn_ref[...]).astype(BF16)


def _inproj(x2d, mod3, tokens_per_mod_row, n1, w_in, lb, sgu_gain, lb_is_zero):
    t, d = x2d.shape
    tm = min(512, t)
    tpr = tokens_per_mod_row // tm
    row = lambda j: pl.BlockSpec((None, 1, d), lambda i: (i // tpr, 0, j))
    full = lambda a: pl.BlockSpec(a.shape, lambda i: (0,) * a.ndim)
    tok = pl.BlockSpec((tm, HW), lambda i: (i, 0))
    return pl.pallas_call(
        functools.partial(_inproj_kernel, lb_is_zero=lb_is_zero),
        out_shape=[jax.ShapeDtypeStruct((t, HW), dt) for dt in (BF16, BF16, F32, F32, BF16, BF16, BF16)],
        grid=(t // tm,),
        in_specs=[pl.BlockSpec((tm, d), lambda i: (i, 0)), row(0), row(1),
                  full(n1), full(w_in), full(lb), full(sgu_gain)],
        out_specs=[tok] * 7,
        compiler_params=_cparams("arbitrary"),
        name="inproj",
    )(x2d, mod3, mod3, n1, w_in, lb, sgu_gain)


def _hgrn_kernel(q_ref, i_ref, lf_ref, s0_ref, o_ref, sfin_ref, st_ref, *, reverse, tt):
    t = pl.program_id(1)

    @pl.when(t == 0)
    def _():
        st_ref[...] = s0_ref[...]

    c = HGRN_CHUNK
    nchunks = tt // c
    rows = lax.broadcasted_iota(jnp.int32, (c, c), 0)
    cols = lax.broadcasted_iota(jnp.int32, (c, c), 1)
    incl = (cols >= rows) if reverse else (cols <= rows)
    tri = jnp.where(incl, 1.0, 0.0).astype(BF16)
    ref_row = c // 2 if reverse else c // 2 - 1
    tot_row = 0 if reverse else c - 1

    for ci in range(nchunks):
        r0 = ((nchunks - 1 - ci) if reverse else ci) * c
        lf = lf_ref[pl.ds(r0, c), :]
        hi = lf.astype(BF16)
        r1 = lf - hi.astype(F32)
        mid = r1.astype(BF16)
        lo = (r1 - mid.astype(F32)).astype(BF16)
        cum = _dot(tri, hi) + (_dot(tri, mid) + _dot(tri, lo))
        ref = cum[ref_row:ref_row + 1, :]
        tot = cum[tot_row:tot_row + 1, :]
        k = 1.0 - jnp.exp(lf)
        qf = q_ref[pl.ds(r0, c), :].astype(F32)
        iv = i_ref[pl.ds(r0, c), :]
        q_in = (qf * jnp.exp(cum - ref)).astype(BF16)
        k_in = (k * jnp.exp(ref - cum)).astype(BF16)
        k_st = (k * jnp.exp(tot - cum)).astype(BF16)
        q_st = (qf * jnp.exp(cum)).astype(BF16)
        dec = jnp.exp(tot)
        for h in range(HEADS):
            sl = slice(h * HEAD_DIM, (h + 1) * HEAD_DIM)
            sc = _dot_nt(q_in[:, sl], k_in[:, sl])
            sc = jnp.where(incl, sc, 0.0).astype(BF16)
            st = st_ref[h]
            o_h = _dot(sc, iv[:, sl]) + _dot_nt(q_st[:, sl], st.astype(BF16))
            o_ref[pl.ds(r0, c), sl] = o_h
            st_ref[h] = st * dec[:, sl] + _dot_tn(iv[:, sl], k_st[:, sl])

    @pl.when(t == pl.num_programs(1) - 1)
    def _():
        sfin_ref[...] = st_ref[...]


def _hgrn(q, i, lf, s0, batch, reverse):
    t = q.shape[0]
    seq = t // batch
    tt = min(512, seq)
    nt = seq // tt
    q3, i3, lf3 = (a.reshape(batch, seq, HW) for a in (q, i, lf))
    tile = (lambda b, s: (b, nt - 1 - s, 0)) if reverse else (lambda b, s: (b, s, 0))
    tok = pl.BlockSpec((None, tt, HW), tile)
    state = pl.BlockSpec((None, HEADS, HEAD_DIM, HEAD_DIM), lambda b, s: (b, 0, 0, 0))
    o, sfin = pl.pallas_call(
        functools.partial(_hgrn_kernel, reverse=reverse, tt=tt),
        out_shape=[jax.ShapeDtypeStruct((batch, seq, HW), F32),
                   jax.ShapeDtypeStruct((batch, HEADS, HEAD_DIM, HEAD_DIM), F32)],
        grid=(batch, nt),
        in_specs=[tok, tok, tok, state],
        out_specs=[tok, state],
        scratch_shapes=[pltpu.VMEM((HEADS, HEAD_DIM, HEAD_DIM), F32)],
        compiler_params=_cparams("arbitrary", "arbitrary"),
        name="hgrn_bwd" if reverse else "hgrn_fwd",
    )(q3, i3, lf3, s0)
    return o.reshape(t, HW), sfin


def _route(logits):
    lane = lax.broadcasted_iota(jnp.int32, logits.shape, 1).astype(F32)
    neg = -jnp.inf
    is_group = lane < N_GROUPS
    gl = jnp.where(is_group, logits, neg)
    gmax = jnp.max(gl, axis=-1, keepdims=True)
    g_sel = jnp.min(jnp.where(gl == gmax, lane, float(ROUTE_LANES)), axis=-1, keepdims=True)
    den = jnp.sum(jnp.where(is_group, jnp.exp(logits - gmax), 0.0), axis=-1, keepdims=True)
    p_sel = 1.0 / den
    first = N_GROUPS + EXPERTS_PER_GROUP * g_sel
    el = jnp.where((lane >= first) & (lane < first + EXPERTS_PER_GROUP), logits, neg)
    t1 = jnp.max(el, axis=-1, keepdims=True)
    i1 = jnp.min(jnp.where(el == t1, lane, float(ROUTE_LANES)), axis=-1, keepdims=True)
    el2 = jnp.where(lane == i1, neg, el)
    t2 = jnp.max(el2, axis=-1, keepdims=True)
    i2 = jnp.min(jnp.where(el2 == t2, lane, float(ROUTE_LANES)), axis=-1, keepdims=True)
    e2 = jnp.exp(t2 - t1)
    w1 = p_sel / (1.0 + e2)
    w2 = p_sel * e2 / (1.0 + e2)
    rec = jnp.where(lane == 0.0, i1 - N_GROUPS, 0.0)
    rec = jnp.where(lane == 1.0, i2 - N_GROUPS, rec)
    rec = jnp.where(lane == 2.0, w1, rec)
    return jnp.where(lane == 3.0, w2, rec)


def _mixer_kernel(of_ref, ob_ref, sg_ref, gu_ref, vn_ref, x_ref, hgain_ref, ws_ref, bs_ref, wo_ref,
                  g1_ref, n2_ref, sh2_ref, sc2_ref, wrh_ref, wrl_ref, br_ref,
                  xo_ref, h2_ref, rt_ref, cat_ref, *, tm):
    o = of_ref[...] + ob_ref[...]
    for h in range(HEADS):
        sl = slice(h * HEAD_DIM, (h + 1) * HEAD_DIM)
        hg = _rms(o[:, sl]) * hgain_ref[:, sl] * sg_ref[:, sl].astype(F32)
        cat_ref[:, sl] = hg.astype(BF16)
    for cc in range(tm // SGU_CHUNK):
        rows = slice(cc * SGU_CHUNK, (cc + 1) * SGU_CHUNK)
        for h in range(HEADS):
            sl = slice(h * HEAD_DIM, (h + 1) * HEAD_DIM)
            mixed = _dot(ws_ref[h], vn_ref[rows, sl]) + bs_ref[h]
            cat_ref[rows, HW + h * HEAD_DIM:HW + (h + 1) * HEAD_DIM] = (
                gu_ref[rows, sl].astype(F32) * mixed).astype(BF16)
    xn = x_ref[...] + g1_ref[...] * _dot(cat_ref[...], wo_ref[...])
    xo_ref[...] = xn
    h2 = _rms(xn) * n2_ref[...]
    h2 = h2 * (1.0 + sc2_ref[...]) + sh2_ref[...]
    _store_token_tiles(h2_ref, h2)
    hi, lo = _split2(h2)
    logits = _dot(hi, wrh_ref[...]) + (_dot(lo, wrh_ref[...]) + _dot(hi, wrl_ref[...])) + br_ref[...]
    rt_ref[...] = _route(logits)


def _mixer(o_f, o_b, sg, gu, vn, x2d, mod3, tokens_per_mod_row, hgain, w_s, b_s, w_out, n2, wr_hi, wr_lo, br):
    t, d = x2d.shape
    tm = min(256, t)
    tpr = tokens_per_mod_row // tm
    row = lambda j: pl.BlockSpec((None, 1, d), lambda i: (i // tpr, 0, j))
    full = lambda a: pl.BlockSpec(a.shape, lambda i: (0,) * a.ndim)
    tok = pl.BlockSpec((tm, HW), lambda i: (i, 0))
    wide = pl.BlockSpec((tm, d), lambda i: (i, 0))
    rt = pl.BlockSpec((tm, ROUTE_LANES), lambda i: (i, 0))
    nd = d // LANES
    tiles = pl.BlockSpec((tm * nd, LANES), lambda i: (i, 0))
    return pl.pallas_call(
        functools.partial(_mixer_kernel, tm=tm),
        out_shape=[jax.ShapeDtypeStruct((t, d), F32), jax.ShapeDtypeStruct((t * nd, LANES), F32),
                   jax.ShapeDtypeStruct((t, ROUTE_LANES), F32)],
        grid=(t // tm,),
        in_specs=[tok, tok, tok, tok, tok, wide, full(hgain), full(w_s), full(b_s), full(w_out),
                  row(2), full(n2), row(3), row(4), full(wr_hi), full(wr_lo), full(br)],
        out_specs=[wide, tiles, rt],
        scratch_shapes=[pltpu.VMEM((tm, 2 * HW), BF16)],
        compiler_params=_cparams("arbitrary"),
        name="mixer",
    )(o_f, o_b, sg, gu, vn, x2d, hgain, w_s, b_s, w_out, mod3, n2, mod3, mod3, wr_hi, wr_lo, br)


MOE_ROWS = 256
MOE_LAG = 3
RING = 3
DMA_GROUPS = 8


def _moe_kernel(last_ref, sexp_ref, tab_hbm, h_hbm, wg_ref, wu_ref, wd_ref, y_hbm,
                idx_ref, xbuf, ybuf, zbuf, xb_ref, hm_ref, wgb, wub, wdb, sem_idx, sem_g, sem_s, sem_z):
    i = pl.program_id(0)
    last = last_ref[0]
    bm = MOE_ROWS
    de = wgb.shape[1]
    d = wgb.shape[0]
    nd = d // LANES

    def idx_copy(step):
        s = step % 2
        return pltpu.make_async_copy(tab_hbm.at[step], idx_ref.at[s], sem_idx.at[s])

    def gathered(slot):
        return pltpu.make_async_copy(h_hbm.at[pl.ds(0, bm * nd)], xbuf.at[slot], sem_g.at[slot])

    def scattered(slot):
        return pltpu.make_async_copy(ybuf.at[slot], y_hbm.at[pl.ds(0, bm * nd)], sem_s.at[slot])

    @pl.when(i <= last)
    def _():
        islot = i % 2
        gslot = i % RING
        cslot = (i + 1) % RING
        sslot = i % RING

        @pl.when(i == 0)
        def _():
            xbuf[...] = jnp.zeros_like(xbuf)
            ybuf[...] = jnp.zeros_like(ybuf)
            zbuf[...] = jnp.zeros_like(zbuf)
            idx_copy(0).start()

        idx_copy(i).wait()

        @pl.when(i < last)
        def _():
            idx_copy(i + 1).start()

        @pl.when(i >= 2)
        def _():
            gathered(cslot).wait()
            scattered(cslot).wait()

        @pl.when((i == 0) | (sexp_ref[i] != sexp_ref[jnp.maximum(i - 1, 0)]))
        def _():
            wgb[...] = wg_ref[...].astype(BF16)
            wub[...] = wu_ref[...].astype(BF16)
            wdb[...] = wd_ref[...].astype(BF16)

        per = bm // (DMA_GROUPS // 2)

        def scatter_group(g):
            for r in range(g * per, (g + 1) * per):
                dst = pl.multiple_of(idx_ref[islot, 1, r], nd)
                pltpu.make_async_copy(ybuf.at[sslot, pl.ds(r * nd, nd)], y_hbm.at[pl.ds(dst, nd)],
                                      sem_s.at[sslot]).start()

        def gather_group(g):
            for r in range(g * per, (g + 1) * per):
                src = pl.multiple_of(idx_ref[islot, 0, r], nd)
                pltpu.make_async_copy(h_hbm.at[pl.ds(src, nd)], xbuf.at[gslot, pl.ds(r * nd, nd)],
                                      sem_g.at[gslot]).start()

        dma_groups = [functools.partial(scatter_group, g) for g in range(DMA_GROUPS // 2)]
        dma_groups += [functools.partial(gather_group, g) for g in range(DMA_GROUPS // 2)]

        xsrc = xbuf.at[cslot]
        for j in range(nd):
            xb_ref[:, j * LANES:(j + 1) * LANES] = xsrc[pl.ds(j, bm, stride=nd), :].astype(BF16)
        def issue_some():
            if dma_groups:
                dma_groups.pop(0)()

        nh = 2
        for j in range(nh):
            cs = slice(j * de // nh, (j + 1) * de // nh)
            issue_some()
            gate = _dot(xb_ref[...], wgb[:, cs])
            issue_some()
            hm_ref[:, cs] = (_silu(gate) * _dot(xb_ref[...], wub[:, cs])).astype(BF16)
        ydst = ybuf.at[cslot]
        n_down = min(4, nd)
        for j in range(n_down):
            issue_some()
            _store_token_tiles_cols(ydst, _dot(hm_ref[...], wdb[:, j * d // n_down:(j + 1) * d // n_down]),
                                    j * nd // n_down, nd)
        while dma_groups:
            issue_some()

        @pl.when(i == last)
        def _():
            gathered(gslot).wait()
            gathered((i + 2) % RING).wait()
            scattered(sslot).wait()
            scattered((i + 2) % RING).wait()

    @pl.when(i > last)
    def _():
        fill = pltpu.make_async_copy(zbuf, y_hbm.at[pl.ds((i - MOE_LAG) * (bm * nd), bm * nd)], sem_z.at[0])
        fill.start()
        fill.wait()


def _moe(h2, tab, last, sexp, n_out_rows, w_gate, w_up, w_down, layer):
    d, de = w_gate.shape[-2:]
    nd = d // LANES
    n_steps = tab.shape[0]
    grid_spec = pltpu.PrefetchScalarGridSpec(
        num_scalar_prefetch=2,
        grid=(n_steps,),
        in_specs=[
            pl.BlockSpec(memory_space=pl.ANY),
            pl.BlockSpec(memory_space=pl.ANY),
            pl.BlockSpec((None, None, d, de), lambda i, la, se: (layer, se[i], 0, 0)),
            pl.BlockSpec((None, None, d, de), lambda i, la, se: (layer, se[i], 0, 0)),
            pl.BlockSpec((None, None, de, d), lambda i, la, se: (layer, se[i], 0, 0)),
        ],
        out_specs=pl.BlockSpec(memory_space=pl.ANY),
        scratch_shapes=[
            pltpu.SMEM((2, 2, MOE_ROWS), jnp.int32),
            pltpu.VMEM((RING, MOE_ROWS * nd, LANES), F32),
            pltpu.VMEM((RING, MOE_ROWS * nd, LANES), F32),
            pltpu.VMEM((MOE_ROWS * nd, LANES), F32),
            pltpu.VMEM((MOE_ROWS, d), BF16),
            pltpu.VMEM((MOE_ROWS, de), BF16),
            pltpu.VMEM((d, de), BF16),
            pltpu.VMEM((d, de), BF16),
            pltpu.VMEM((de, d), BF16),
            pltpu.SemaphoreType.DMA((2,)),
            pltpu.SemaphoreType.DMA((RING,)),
            pltpu.SemaphoreType.DMA((RING,)),
            pltpu.SemaphoreType.DMA((1,)),
        ],
    )
    return pl.pallas_call(
        _moe_kernel,
        out_shape=jax.ShapeDtypeStruct((n_out_rows * nd, LANES), F32),
        grid_spec=grid_spec,
        compiler_params=_cparams("arbitrary"),
        name="moe",
    )(last, sexp, tab, h2, w_gate, w_up, w_down)


def _dispatch_tables(expert_flat, ttot, nd):
    bm = MOE_ROWS
    n_slots = expert_flat.shape[0]
    n_blocks = -(-n_slots // bm) + N_EXPERTS
    n_steps = n_blocks + MOE_LAG
    _, order = lax.sort_key_val(expert_flat, lax.iota(jnp.int32, n_slots))
    counts = jnp.sum(expert_flat[:, None] == jnp.arange(N_EXPERTS, dtype=jnp.int32)[None, :], axis=0,
                     dtype=jnp.int32)
    padded = (counts + bm - 1) // bm * bm
    pad_end = jnp.cumsum(padded)
    pad_start = pad_end - padded
    start = jnp.cumsum(counts) - counts
    blk_row0 = jnp.arange(n_blocks, dtype=jnp.int32) * bm
    bexp = jnp.minimum(jnp.sum(pad_end[None, :] <= blk_row0[:, None], axis=1), N_EXPERTS - 1).astype(jnp.int32)
    lane = jnp.arange(bm, dtype=jnp.int32)[None, :]
    off = (blk_row0 - pad_start[bexp])[:, None] + lane
    valid = off < counts[bexp][:, None]
    slot = order[jnp.clip(start[bexp][:, None] + off, 0, n_slots - 1)]
    pad_rank = blk_row0[:, None] + lane - (start[bexp] + counts[bexp])[:, None]
    gsrc = jnp.where(valid, slot % ttot, 0)
    sdst = jnp.where(valid, slot, n_slots + pad_rank)
    spare = n_blocks * bm + jnp.arange(MOE_LAG * bm, dtype=jnp.int32).reshape(MOE_LAG, bm)
    gtab = jnp.concatenate([gsrc, jnp.zeros((MOE_LAG, bm), jnp.int32)], axis=0)
    stab = jnp.concatenate([spare, sdst], axis=0)
    tab = (jnp.stack([gtab, stab], axis=1) * nd).astype(jnp.int32)
    n_used = jnp.sum(padded) // bm
    last = (n_used + MOE_LAG - 1).astype(jnp.int32).reshape(1)
    sexp = bexp[jnp.clip(jnp.arange(n_steps) - (MOE_LAG - 1), 0, n_blocks - 1)]
    return tab, last, sexp, n_steps * bm


def _combine_kernel(x_ref, y0_ref, y1_ref, rt_ref, g2_ref, fn_ref, o_ref, *, final):
    w = rt_ref[...]
    tm, d = x_ref.shape
    nd = d // LANES
    f = w[:, 2:3] * _load_token_tiles(y0_ref, tm, nd) + w[:, 3:4] * _load_token_tiles(y1_ref, tm, nd)
    xn = x_ref[...] + g2_ref[...] * f
    if final:
        xn = _rms(xn) * fn_ref[...]
    o_ref[...] = xn


def _combine(x2d, y2, row0, ttot, route, mod3, tokens_per_mod_row, final_norm, final):
    t, d = x2d.shape
    tm = min(256, t)
    tpr = tokens_per_mod_row // tm
    b0, b1 = row0 // tm, (ttot + row0) // tm
    nd = d // LANES
    return pl.pallas_call(
        functools.partial(_combine_kernel, final=final),
        out_shape=jax.ShapeDtypeStruct((t, d), F32),
        grid=(t // tm,),
        in_specs=[pl.BlockSpec((tm, d), lambda i: (i, 0)),
                  pl.BlockSpec((tm * nd, LANES), lambda i: (i + b0, 0)),
                  pl.BlockSpec((tm * nd, LANES), lambda i: (i + b1, 0)),
                  pl.BlockSpec((tm, ROUTE_LANES), lambda i: (i, 0)),
                  pl.BlockSpec((None, 1, d), lambda i: (i // tpr, 0, 5)),
                  pl.BlockSpec(final_norm.shape, lambda i: (0, 0))],
        out_specs=pl.BlockSpec((tm, d), lambda i: (i, 0)),
        compiler_params=_cparams("arbitrary"),
        name="combine",
    )(x2d, y2, y2, route, mod3, final_norm)


def kernel(x, c, ctx, c_ctx, norm1, norm2, w_mod, b_mod, w_in, lb_logits, hgrn_norm, sgu_norm, sgu_w, sgu_b,
           w_out, w_group, b_group, w_router, b_router, w_gate, w_up, w_down, final_norm):
    b, l, d = x.shape
    lc = ctx.shape[1]
    depth = w_mod.shape[0]
    t_lat, t_ctx = b * l, b * lc

    lb_cum = jnp.cumsum(jax.nn.softmax(lb_logits.astype(F32), axis=0), axis=0)
    lower_bound = jnp.maximum(lb_cum - lb_cum[0:1], 0.0)

    cc = jnp.zeros((MOD_ROWS, d), F32).at[:b].set(c).at[b].set(c_ctx)
    mod = _modulation(cc, w_mod, b_mod)

    w_route = jnp.concatenate([w_group, w_router], axis=-1)
    w_route = jnp.pad(w_route, ((0, 0), (0, 0), (0, ROUTE_LANES - w_route.shape[-1])))
    wr_hi = w_route.astype(BF16)
    wr_lo = (w_route - wr_hi.astype(F32)).astype(BF16)
    b_route = jnp.concatenate([b_group, b_router], axis=-1)
    b_route = jnp.pad(b_route, ((0, 0), (0, ROUTE_LANES - b_route.shape[-1])))[:, None, :]
    b_s = jnp.broadcast_to(sgu_b[..., None], sgu_b.shape + (HEAD_DIM,)).astype(F32)

    w_in_b, w_out_b, sgu_w_b = w_in.astype(BF16), w_out.astype(BF16), sgu_w.astype(BF16)

    xs = x.reshape(t_lat, d)
    ys = ctx.reshape(t_ctx, d)
    s_zero = jnp.zeros((b, HEADS, HEAD_DIM, HEAD_DIM), F32)
    fn = final_norm.reshape(1, d)

    for layer in range(depth):
        last = layer == depth - 1
        mod_lat = mod[layer, :b].reshape(b, 1, N_MOD * d)
        mod_ctx = mod[layer, b:b + 1].reshape(1, 1, N_MOD * d)
        n1 = norm1[layer].reshape(1, d)
        n2 = norm2[layer].reshape(1, d)
        lb = lower_bound[layer]
        sgn = sgu_norm[layer].reshape(1, HW)
        hgain = hgrn_norm[layer].reshape(1, HW)
        mix_w = (hgain, sgu_w_b[layer], b_s[layer], w_out_b[layer], n2, wr_hi[layer], wr_lo[layer],
                 b_route[layer])

        cq, ci, clf_f, clf_b, csg, cgu, cvn = _inproj(ys, mod_ctx, t_ctx, n1, w_in_b[layer], lb, sgn, layer == 0)
        co_f, s_f = _hgrn(cq, ci, clf_f, s_zero, b, reverse=False)
        co_b, s_b = _hgrn(cq, ci, clf_b, s_zero, b, reverse=True)

        q, i, lf_f, lf_b, sg, gu, vn = _inproj(xs, mod_lat, l, n1, w_in_b[layer], lb, sgn, layer == 0)
        o_f, _ = _hgrn(q, i, lf_f, s_f, b, reverse=False)
        o_b, _ = _hgrn(q, i, lf_b, s_b, b, reverse=True)
        xs, h2, route = _mixer(o_f, o_b, sg, gu, vn, xs, mod_lat, l, *mix_w)

        if not last:
            ys, h2c, route_c = _mixer(co_f, co_b, csg, cgu, cvn, ys, mod_ctx, t_ctx, *mix_w)
            h2 = jnp.concatenate([h2, h2c], axis=0)
            route_all = jnp.concatenate([route, route_c], axis=0)
        else:
            route_all = route

        ttot = route_all.shape[0]
        expert_flat = route_all[:, :TOP_K].astype(jnp.int32).T.reshape(-1)
        tab, last_step, sexp, n_out_rows = _dispatch_tables(expert_flat, ttot, d // LANES)
        y2 = _moe(h2, tab, last_step, sexp, n_out_rows, w_gate, w_up, w_down, layer)

        xs = _combine(xs, y2, 0, ttot, route, mod_lat, l, fn, final=last)
        if not last:
            ys = _combine(ys, y2, t_lat, ttot, route_c, mod_ctx, t_ctx, fn, final=False)

    return xs.reshape(b, l, d)
```

```python
import functools

import jax
import jax.numpy as jnp
from jax import lax
from jax.experimental import pallas as pl
from jax.experimental.pallas import tpu as pltpu

F32 = jnp.float32
BF16 = jnp.bfloat16

EPS = 1e-6
HEADS = 4
HEAD_DIM = 128
HW = HEADS * HEAD_DIM
HGRN_CHUNK = 64
SGU_CHUNK = 128
N_GROUPS = 4
EXPERTS_PER_GROUP = 8
N_EXPERTS = N_GROUPS * EXPERTS_PER_GROUP
TOP_K = 2
N_MOD = 6
LANES = 128
SUBLANES = 8
ROUTE_LANES = LANES
MOD_ROWS = 16

PROJ_ROWS = 512
SCAN_ROWS = 256
MIX_ROWS = 256

VMEM_LIMIT = 48 * 1024 * 1024


def _cparams(*sem):
    return pltpu.CompilerParams(dimension_semantics=sem, vmem_limit_bytes=VMEM_LIMIT)


def _split2(a):
    hi = a.astype(BF16)
    lo = (a - hi.astype(F32)).astype(BF16)
    return hi, lo


def _dot(a, b):
    return jnp.dot(a, b, preferred_element_type=F32)


def _dot_nt(a, b):
    return lax.dot_general(a, b, (((1,), (1,)), ((), ())), preferred_element_type=F32)


def _dot_tn(a, b):
    return lax.dot_general(a, b, (((0,), (0,)), ((), ())), preferred_element_type=F32)


def _dot3(a, b):
    ah, al = _split2(a)
    bh, bl = _split2(b)
    return _dot(ah, bh) + (_dot(al, bh) + _dot(ah, bl))


def _silu(x):
    return x / (1.0 + jnp.exp(-x))


def _rms(x):
    return x * lax.rsqrt(jnp.mean(x * x, axis=-1, keepdims=True) + EPS)


def _full(a):
    return pl.BlockSpec(a.shape, lambda *_: (0,) * a.ndim)


def _stream_inputs(xa, xb, n_a, tm, d):
    if xb is None:
        return [xa], [pl.BlockSpec((tm, d), lambda i: (i, 0))]
    return [xa, xb], [pl.BlockSpec((tm, d), lambda i: (jnp.minimum(i, n_a - 1), 0)),
                      pl.BlockSpec((tm, d), lambda i: (jnp.maximum(i - n_a, 0), 0))]


def _mod_row(d, tiles_per_row, n_rows, j):
    return pl.BlockSpec((None, 1, d), lambda i: (jnp.minimum(i // tiles_per_row, n_rows - 1), 0, j))


def _store_token_tiles_cols(ref, x, j0, nd):
    rows, w = x.shape
    for j in range(w // LANES):
        ref[pl.ds(j0 + j, rows, stride=nd), :] = x[:, j * LANES:(j + 1) * LANES]


def _store_token_tiles(ref, x):
    _store_token_tiles_cols(ref, x, 0, x.shape[1] // LANES)


def _load_token_tiles(ref, rows, nd):
    return jnp.concatenate([ref[pl.ds(j, rows, stride=nd), :] for j in range(nd)], axis=-1)


def _mod_kernel(c_ref, w_ref, b_ref, o_ref):
    o_ref[...] = _dot3(_silu(c_ref[...]), w_ref[...]) + b_ref[...]


def _modulation(cc, w_mod, b_mod):
    depth, d, n = w_mod.shape
    tn = 1536
    return pl.pallas_call(
        _mod_kernel,
        out_shape=jax.ShapeDtypeStruct((depth, MOD_ROWS, n), F32),
        grid=(depth, n // tn),
        in_specs=[
            pl.BlockSpec((MOD_ROWS, d), lambda l, j: (0, 0)),
            pl.BlockSpec((None, d, tn), lambda l, j: (l, 0, j)),
            pl.BlockSpec((None, 1, tn), lambda l, j: (l, 0, j)),
        ],
        out_specs=pl.BlockSpec((None, MOD_ROWS, tn), lambda l, j: (l, 0, j)),
        compiler_params=_cparams("arbitrary", "arbitrary"),
        name="modulation",
    )(cc, w_mod, b_mod.reshape(depth, 1, n))


def _log_forget(z, lb, lb_is_zero):
    ls = jnp.minimum(z, 0.0) - jnp.log(1.0 + jnp.exp(-jnp.abs(z)))
    if lb_is_zero:
        return ls
    a = jnp.log(1.0 - lb) + ls
    b = jnp.log(lb)
    return jnp.maximum(a, b) + jnp.log(1.0 + jnp.exp(-jnp.abs(a - b)))


def _stream_tile(xa_ref, xb_ref, n_a_tiles):
    if xb_ref is None:
        return xa_ref[...]
    return jnp.where(pl.program_id(0) < n_a_tiles, xa_ref[...], xb_ref[...])


def _inproj_kernel(*refs, lb_is_zero, n_a_tiles, two_inputs):
    xa_ref, xb_ref = (refs[0], refs[1]) if two_inputs else (refs[0], None)
    (sh_ref, sc_ref, n1_ref, w_ref, lb_ref, sgn_ref,
     q_ref, i_ref, lff_ref, lfb_ref, sg_ref, gu_ref, vn_ref) = refs[2 if two_inputs else 1:]
    h = _rms(_stream_tile(xa_ref, xb_ref, n_a_tiles)) * n1_ref[...]
    hb = (h * (1.0 + sc_ref[...]) + sh_ref[...]).astype(BF16)

    def proj(j):
        return _dot(hb, w_ref[:, j * HW:(j + 1) * HW])

    q_ref[...] = proj(0).astype(BF16)
    lff_ref[...] = _log_forget(proj(1), lb_ref[0:1, :], lb_is_zero)
    lfb_ref[...] = _log_forget(proj(2), lb_ref[1:2, :], lb_is_zero)
    i_ref[...] = proj(3).astype(BF16)
    sg_ref[...] = _silu(proj(4)).astype(BF16)
    gu_ref[...] = jax.nn.gelu(proj(5)).astype(BF16)
    vn_ref[...] = (_rms(jax.nn.gelu(proj(6))) * sgn_ref[...]).astype(BF16)


def _inproj(xa, xb, t_a, t_b, mod3, tokens_per_mod_row, n1, w_in, lb, sgu_gain, lb_is_zero):
    d = xa.shape[1]
    tm = PROJ_ROWS
    n_a, n_b = t_a // tm, t_b // tm
    t = t_a + t_b
    tok = pl.BlockSpec((tm, HW), lambda i: (i, 0))
    row = functools.partial(_mod_row, d, tokens_per_mod_row // tm, mod3.shape[0])
    x_in, x_spec = _stream_inputs(xa, xb, n_a, tm, d)
    return pl.pallas_call(
        functools.partial(_inproj_kernel, lb_is_zero=lb_is_zero, n_a_tiles=n_a, two_inputs=xb is not None),
        out_shape=[jax.ShapeDtypeStruct((t, HW), dt) for dt in (BF16, BF16, F32, F32, BF16, BF16, BF16)],
        grid=(n_a + n_b,),
        in_specs=x_spec + [row(0), row(1), _full(n1), _full(w_in), _full(lb), _full(sgu_gain)],
        out_specs=[tok] * 7,
        compiler_params=_cparams("arbitrary"),
        name="inproj",
    )(*x_in, mod3, mod3, n1, w_in, lb, sgu_gain)


def _chunk_cumsum(x, reverse):
    c, w = x.shape
    g = c // SUBLANES
    x3 = x.reshape(g, SUBLANES, w)
    sub = lax.broadcasted_iota(jnp.int32, x3.shape, 1)
    for s in (1, 2, 4):
        if reverse:
            x3 = x3 + jnp.where(sub < SUBLANES - s, pltpu.roll(x3, SUBLANES - s, axis=1), 0.0)
        else:
            x3 = x3 + jnp.where(sub >= s, pltpu.roll(x3, s, axis=1), 0.0)
    edge = 0 if reverse else SUBLANES - 1
    tot = x3[:, edge:edge + 1, :]
    offs = [None] * g
    acc = jnp.zeros((1, w), F32)
    for gi in (reversed(range(g)) if reverse else range(g)):
        offs[gi] = acc
        acc = acc + tot[gi]
    x3 = x3 + jnp.stack(offs, axis=0)
    return x3.reshape(c, w)


def _scan_chunk(q_ref, i_ref, lf_ref, o_ref, st_ref, r0, reverse):
    c = HGRN_CHUNK
    rows = lax.broadcasted_iota(jnp.int32, (c, c), 0)
    cols = lax.broadcasted_iota(jnp.int32, (c, c), 1)
    incl = (cols >= rows) if reverse else (cols <= rows)
    ref_row = c // 2 if reverse else c // 2 - 1
    tot_row = 0 if reverse else c - 1
    lf = lf_ref[pl.ds(r0, c), :]
    cum = _chunk_cumsum(lf, reverse)
    ref = cum[ref_row:ref_row + 1, :]
    tot = cum[tot_row:tot_row + 1, :]
    k = 1.0 - jnp.exp(lf)
    qf = q_ref[pl.ds(r0, c), :].astype(F32)
    iv = i_ref[pl.ds(r0, c), :]
    q_in = (qf * jnp.exp(cum - ref)).astype(BF16)
    k_in = (k * jnp.exp(ref - cum)).astype(BF16)
    k_st = (k * jnp.exp(tot - cum)).astype(BF16)
    q_st = (qf * jnp.exp(cum)).astype(BF16)
    dec = jnp.exp(tot)
    for h in range(HEADS):
        sl = slice(h * HEAD_DIM, (h + 1) * HEAD_DIM)
        sc = _dot_nt(q_in[:, sl], k_in[:, sl])
        sc = jnp.where(incl, sc, 0.0).astype(BF16)
        st = st_ref[h]
        o_ref[pl.ds(r0, c), sl] = _dot(sc, iv[:, sl]) + _dot_nt(q_st[:, sl], st.astype(BF16))
        st_ref[h] = st * dec[:, sl] + _dot_tn(iv[:, sl], k_st[:, sl])


def _hgrn_kernel(qf_ref, if_ref, lff_ref, qb_ref, ib_ref, lfb_ref, of_ref, ob_ref, stf_ref, stb_ref, *, tt):
    @pl.when(pl.program_id(1) == 0)
    def _():
        stf_ref[...] = jnp.zeros_like(stf_ref)
        stb_ref[...] = jnp.zeros_like(stb_ref)

    nchunks = tt // HGRN_CHUNK
    for ci in range(nchunks):
        _scan_chunk(qf_ref, if_ref, lff_ref, of_ref, stf_ref, ci * HGRN_CHUNK, False)
        _scan_chunk(qb_ref, ib_ref, lfb_ref, ob_ref, stb_ref, (nchunks - 1 - ci) * HGRN_CHUNK, True)


def _hgrn(q, i, lf_f, lf_b, batch, seq, ctx_len):
    t = q.shape[0]
    tt = SCAN_ROWS
    nt, nc = seq // tt, ctx_len // tt
    ctx0 = batch * nt

    def fwd(b, s):
        return (jnp.where(s < nc, ctx0 + b * nc + s, b * nt + (s - nc)), 0)

    def bwd(b, s):
        return (jnp.where(s < nc, ctx0 + b * nc + (nc - 1 - s), b * nt + (nt - 1 - (s - nc))), 0)

    tf, tb = pl.BlockSpec((tt, HW), fwd), pl.BlockSpec((tt, HW), bwd)
    return pl.pallas_call(
        functools.partial(_hgrn_kernel, tt=tt),
        out_shape=[jax.ShapeDtypeStruct((t, HW), F32)] * 2,
        grid=(batch, nc + nt),
        in_specs=[tf, tf, tf, tb, tb, tb],
        out_specs=[tf, tb],
        scratch_shapes=[pltpu.VMEM((HEADS, HEAD_DIM, HEAD_DIM), F32)] * 2,
        compiler_params=_cparams("arbitrary", "arbitrary"),
        name="hgrn",
    )(q, i, lf_f, q, i, lf_b)


def _route(logits):
    lane = lax.broadcasted_iota(jnp.int32, logits.shape, 1).astype(F32)
    neg = -jnp.inf
    is_group = lane < N_GROUPS
    gl = jnp.where(is_group, logits, neg)
    gmax = jnp.max(gl, axis=-1, keepdims=True)
    g_sel = jnp.min(jnp.where(gl == gmax, lane, float(ROUTE_LANES)), axis=-1, keepdims=True)
    den = jnp.sum(jnp.where(is_group, jnp.exp(logits - gmax), 0.0), axis=-1, keepdims=True)
    p_sel = 1.0 / den
    first = N_GROUPS + EXPERTS_PER_GROUP * g_sel
    el = jnp.where((lane >= first) & (lane < first + EXPERTS_PER_GROUP), logits, neg)
    t1 = jnp.max(el, axis=-1, keepdims=True)
    i1 = jnp.min(jnp.where(el == t1, lane, float(ROUTE_LANES)), axis=-1, keepdims=True)
    el2 = jnp.where(lane == i1, neg, el)
    t2 = jnp.max(el2, axis=-1, keepdims=True)
    i2 = jnp.min(jnp.where(el2 == t2, lane, float(ROUTE_LANES)), axis=-1, keepdims=True)
    e2 = jnp.exp(t2 - t1)
    w1 = p_sel / (1.0 + e2)
    w2 = p_sel * e2 / (1.0 + e2)
    rec = jnp.where(lane == 0.0, i1 - N_GROUPS, 0.0)
    rec = jnp.where(lane == 1.0, i2 - N_GROUPS, rec)
    rec = jnp.where(lane == 2.0, w1, rec)
    return jnp.where(lane == 3.0, w2, rec)


def _mixer_kernel(of_ref, ob_ref, sg_ref, gu_ref, vn_ref, *refs, tm, n_a_tiles, two_inputs):
    xa_ref, xb_ref = (refs[0], refs[1]) if two_inputs else (refs[0], None)
    (hgain_ref, ws_ref, bs_ref, wo_ref, g1_ref, n2_ref, sh2_ref, sc2_ref, wrh_ref, wrl_ref, br_ref,
     xo_ref, h2_ref, rt_ref, cat_ref) = refs[2 if two_inputs else 1:]
    o = of_ref[...] + ob_ref[...]
    for h in range(HEADS):
        sl = slice(h * HEAD_DIM, (h + 1) * HEAD_DIM)
        hg = _rms(o[:, sl]) * hgain_ref[:, sl] * sg_ref[:, sl].astype(F32)
        cat_ref[:, sl] = hg.astype(BF16)
    for cc in range(tm // SGU_CHUNK):
        rows = slice(cc * SGU_CHUNK, (cc + 1) * SGU_CHUNK)
        for h in range(HEADS):
            sl = slice(h * HEAD_DIM, (h + 1) * HEAD_DIM)
            mixed = _dot(ws_ref[h], vn_ref[rows, sl]) + bs_ref[h]
            cat_ref[rows, HW + h * HEAD_DIM:HW + (h + 1) * HEAD_DIM] = (
                gu_ref[rows, sl].astype(F32) * mixed).astype(BF16)
    xn = _stream_tile(xa_ref, xb_ref, n_a_tiles) + g1_ref[...] * _dot(cat_ref[...], wo_ref[...])
    xo_ref[...] = xn
    h2 = _rms(xn) * n2_ref[...]
    h2 = h2 * (1.0 + sc2_ref[...]) + sh2_ref[...]
    _store_token_tiles(h2_ref, h2)
    hi, lo = _split2(h2)
    logits = _dot(hi, wrh_ref[...]) + (_dot(lo, wrh_ref[...]) + _dot(hi, wrl_ref[...])) + br_ref[...]
    rt_ref[...] = _route(logits)


def _mixer(o_f, o_b, sg, gu, vn, xa, xb, t_a, t_b, mod3, tokens_per_mod_row, hgain, w_s, b_s, w_out, n2,
           wr_hi, wr_lo, br):
    d = xa.shape[1]
    tm = MIX_ROWS
    n_a, n_b = t_a // tm, t_b // tm
    t = t_a + t_b
    nd = d // LANES
    row = functools.partial(_mod_row, d, tokens_per_mod_row // tm, mod3.shape[0])
    tok = pl.BlockSpec((tm, HW), lambda i: (i, 0))
    wide = pl.BlockSpec((tm, d), lambda i: (i, 0))
    x_in, x_spec = _stream_inputs(xa, xb, n_a, tm, d)
    return pl.pallas_call(
        functools.partial(_mixer_kernel, tm=tm, n_a_tiles=n_a, two_inputs=xb is not None),
        out_shape=[jax.ShapeDtypeStruct((t, d), F32), jax.ShapeDtypeStruct((t * nd, LANES), F32),
                   jax.ShapeDtypeStruct((t, ROUTE_LANES), F32)],
        grid=(n_a + n_b,),
        in_specs=[tok, tok, tok, tok, tok] + x_spec + [_full(hgain), _full(w_s), _full(b_s), _full(w_out),
                  row(2), _full(n2), row(3), row(4), _full(wr_hi), _full(wr_lo), _full(br)],
        out_specs=[wide, pl.BlockSpec((tm * nd, LANES), lambda i: (i, 0)),
                   pl.BlockSpec((tm, ROUTE_LANES), lambda i: (i, 0))],
        scratch_shapes=[pltpu.VMEM((tm, 2 * HW), BF16)],
        compiler_params=_cparams("arbitrary"),
        name="mixer",
    )(o_f, o_b, sg, gu, vn, *x_in, hgain, w_s, b_s, w_out, mod3, n2, mod3, mod3, wr_hi, wr_lo, br)


MOE_ROWS = 256
MOE_LAG = 3
RING = 3
DMA_GROUPS = 8


def _moe_kernel(last_ref, sexp_ref, tab_hbm, h_hbm, wg_ref, wu_ref, wd_ref, y_hbm,
                idx_ref, xbuf, ybuf, zbuf, xb_ref, hm_ref, wgb, wub, wdb, sem_idx, sem_g, sem_s, sem_z):
    i = pl.program_id(0)
    last = last_ref[0]
    bm = MOE_ROWS
    de = wgb.shape[1]
    d = wgb.shape[0]
    nd = d // LANES

    def idx_copy(step):
        s = step % 2
        return pltpu.make_async_copy(tab_hbm.at[step], idx_ref.at[s], sem_idx.at[s])

    def gathered(slot):
        return pltpu.make_async_copy(h_hbm.at[pl.ds(0, bm * nd)], xbuf.at[slot], sem_g.at[slot])

    def scattered(slot):
        return pltpu.make_async_copy(ybuf.at[slot], y_hbm.at[pl.ds(0, bm * nd)], sem_s.at[slot])

    @pl.when(i <= last)
    def _():
        islot = i % 2
        gslot = i % RING
        cslot = (i + 1) % RING
        sslot = i % RING

        @pl.when(i == 0)
        def _():
            xbuf[...] = jnp.zeros_like(xbuf)
            ybuf[...] = jnp.zeros_like(ybuf)
            zbuf[...] = jnp.zeros_like(zbuf)
            idx_copy(0).start()

        idx_copy(i).wait()

        @pl.when(i < last)
        def _():
            idx_copy(i + 1).start()

        @pl.when(i >= 2)
        def _():
            gathered(cslot).wait()
            scattered(cslot).wait()

        @pl.when((i == 0) | (sexp_ref[i] != sexp_ref[jnp.maximum(i - 1, 0)]))
        def _():
            wgb[...] = wg_ref[...].astype(BF16)
            wub[...] = wu_ref[...].astype(BF16)
            wdb[...] = wd_ref[...].astype(BF16)

        per = bm // (DMA_GROUPS // 2)

        def scatter_group(g):
            for r in range(g * per, (g + 1) * per):
                dst = pl.multiple_of(idx_ref[islot, 1, r], nd)
                pltpu.make_async_copy(ybuf.at[sslot, pl.ds(r * nd, nd)], y_hbm.at[pl.ds(dst, nd)],
                                      sem_s.at[sslot]).start(priority=r % 2)

        def gather_group(g):
            for r in range(g * per, (g + 1) * per):
                src = pl.multiple_of(idx_ref[islot, 0, r], nd)
                pltpu.make_async_copy(h_hbm.at[pl.ds(src, nd)], xbuf.at[gslot, pl.ds(r * nd, nd)],
                                      sem_g.at[gslot]).start(priority=r % 2)

        dma_groups = [functools.partial(scatter_group, g) for g in range(DMA_GROUPS // 2)]
        dma_groups += [functools.partial(gather_group, g) for g in range(DMA_GROUPS // 2)]

        def issue_some():
            if dma_groups:
                dma_groups.pop(0)()

        xsrc = xbuf.at[cslot]
        for j in range(nd):
            xb_ref[:, j * LANES:(j + 1) * LANES] = xsrc[pl.ds(j, bm, stride=nd), :].astype(BF16)
        nh = 2
        for j in range(nh):
            cs = slice(j * de // nh, (j + 1) * de // nh)
            issue_some()
            gate = _dot(xb_ref[...], wgb[:, cs])
            issue_some()
            hm_ref[:, cs] = (_silu(gate) * _dot(xb_ref[...], wub[:, cs])).astype(BF16)
        ydst = ybuf.at[cslot]
        n_down = min(4, nd)
        for j in range(n_down):
            issue_some()
            _store_token_tiles_cols(ydst, _dot(hm_ref[...], wdb[:, j * d // n_down:(j + 1) * d // n_down]),
                                    j * nd // n_down, nd)
        while dma_groups:
            issue_some()

        @pl.when(i == last)
        def _():
            gathered(gslot).wait()
            gathered((i + 2) % RING).wait()
            scattered(sslot).wait()
            scattered((i + 2) % RING).wait()

    @pl.when(i > last)
    def _():
        fill = pltpu.make_async_copy(zbuf, y_hbm.at[pl.ds((i - MOE_LAG) * (bm * nd), bm * nd)], sem_z.at[0])
        fill.start()
        fill.wait()


def _moe(h2, tab, last, sexp, n_out_rows, w_gate, w_up, w_down, layer):
    d, de = w_gate.shape[-2:]
    nd = d // LANES
    n_steps = tab.shape[0]
    grid_spec = pltpu.PrefetchScalarGridSpec(
        num_scalar_prefetch=2,
        grid=(n_steps,),
        in_specs=[
            pl.BlockSpec(memory_space=pl.ANY),
            pl.BlockSpec(memory_space=pl.ANY),
            pl.BlockSpec((None, None, d, de), lambda i, la, se: (layer, se[i], 0, 0)),
            pl.BlockSpec((None, None, d, de), lambda i, la, se: (layer, se[i], 0, 0)),
            pl.BlockSpec((None, None, de, d), lambda i, la, se: (layer, se[i], 0, 0)),
        ],
        out_specs=pl.BlockSpec(memory_space=pl.ANY),
        scratch_shapes=[
            pltpu.SMEM((2, 2, MOE_ROWS), jnp.int32),
            pltpu.VMEM((RING, MOE_ROWS * nd, LANES), F32),
            pltpu.VMEM((RING, MOE_ROWS * nd, LANES), F32),
            pltpu.VMEM((MOE_ROWS * nd, LANES), F32),
            pltpu.VMEM((MOE_ROWS, d), BF16),
            pltpu.VMEM((MOE_ROWS, de), BF16),
            pltpu.VMEM((d, de), BF16),
            pltpu.VMEM((d, de), BF16),
            pltpu.VMEM((de, d), BF16),
            pltpu.SemaphoreType.DMA((2,)),
            pltpu.SemaphoreType.DMA((RING,)),
            pltpu.SemaphoreType.DMA((RING,)),
            pltpu.SemaphoreType.DMA((1,)),
        ],
    )
    return pl.pallas_call(
        _moe_kernel,
        out_shape=jax.ShapeDtypeStruct((n_out_rows * nd, LANES), F32),
        grid_spec=grid_spec,
        compiler_params=_cparams("arbitrary"),
        name="moe",
    )(last, sexp, tab, h2, w_gate, w_up, w_down)


def _dispatch_tables(expert_flat, ttot, nd):
    bm = MOE_ROWS
    n_slots = expert_flat.shape[0]
    n_blocks = -(-n_slots // bm) + N_EXPERTS
    n_steps = n_blocks + MOE_LAG
    _, order = lax.sort_key_val(expert_flat, lax.iota(jnp.int32, n_slots))
    counts = jnp.sum(expert_flat[:, None] == jnp.arange(N_EXPERTS, dtype=jnp.int32)[None, :], axis=0,
                     dtype=jnp.int32)
    padded = (counts + bm - 1) // bm * bm
    pad_end = jnp.cumsum(padded)
    pad_start = pad_end - padded
    start = jnp.cumsum(counts) - counts
    blk_row0 = jnp.arange(n_blocks, dtype=jnp.int32) * bm
    bexp = jnp.minimum(jnp.sum(pad_end[None, :] <= blk_row0[:, None], axis=1), N_EXPERTS - 1).astype(jnp.int32)
    lane = jnp.arange(bm, dtype=jnp.int32)[None, :]
    off = (blk_row0 - pad_start[bexp])[:, None] + lane
    valid = off < counts[bexp][:, None]
    slot = order[jnp.clip(start[bexp][:, None] + off, 0, n_slots - 1)]
    pad_rank = blk_row0[:, None] + lane - (start[bexp] + counts[bexp])[:, None]
    gsrc = jnp.where(valid, slot % ttot, 0)
    sdst = jnp.where(valid, slot, n_slots + pad_rank)
    spare = n_blocks * bm + jnp.arange(MOE_LAG * bm, dtype=jnp.int32).reshape(MOE_LAG, bm)
    gtab = jnp.concatenate([gsrc, jnp.zeros((MOE_LAG, bm), jnp.int32)], axis=0)
    stab = jnp.concatenate([spare, sdst], axis=0)
    tab = (jnp.stack([gtab, stab], axis=1) * nd).astype(jnp.int32)
    n_used = jnp.sum(padded) // bm
    last = (n_used + MOE_LAG - 1).astype(jnp.int32).reshape(1)
    sexp = bexp[jnp.clip(jnp.arange(n_steps) - (MOE_LAG - 1), 0, n_blocks - 1)]
    return tab, last, sexp, n_steps * bm


def _combine_kernel(x_ref, y0_ref, y1_ref, rt_ref, g2_ref, fn_ref, o_ref, *, final):
    w = rt_ref[...]
    tm, d = x_ref.shape
    nd = d // LANES
    f = w[:, 2:3] * _load_token_tiles(y0_ref, tm, nd) + w[:, 3:4] * _load_token_tiles(y1_ref, tm, nd)
    xn = x_ref[...] + g2_ref[...] * f
    if final:
        xn = _rms(xn) * fn_ref[...]
    o_ref[...] = xn


def _combine(x2d, y2, t, ttot, route, mod3, tokens_per_mod_row, final_norm, final):
    d = x2d.shape[1]
    tm = MIX_ROWS
    b1 = ttot // tm
    nd = d // LANES
    return pl.pallas_call(
        functools.partial(_combine_kernel, final=final),
        out_shape=jax.ShapeDtypeStruct((t, d), F32),
        grid=(t // tm,),
        in_specs=[pl.BlockSpec((tm, d), lambda i: (i, 0)),
                  pl.BlockSpec((tm * nd, LANES), lambda i: (i, 0)),
                  pl.BlockSpec((tm * nd, LANES), lambda i: (i + b1, 0)),
                  pl.BlockSpec((tm, ROUTE_LANES), lambda i: (i, 0)),
                  _mod_row(d, tokens_per_mod_row // tm, mod3.shape[0], 5),
                  _full(final_norm)],
        out_specs=pl.BlockSpec((tm, d), lambda i: (i, 0)),
        compiler_params=_cparams("arbitrary"),
        name="combine",
    )(x2d, y2, y2, route, mod3, final_norm)


def kernel(x, c, ctx, c_ctx, norm1, norm2, w_mod, b_mod, w_in, lb_logits, hgrn_norm, sgu_norm, sgu_w, sgu_b,
           w_out, w_group, b_group, w_router, b_router, w_gate, w_up, w_down, final_norm):
    b, l, d = x.shape
    lc = ctx.shape[1]
    depth = w_mod.shape[0]
    t_lat, t_ctx = b * l, b * lc
    nd = d // LANES

    lb_cum = jnp.cumsum(jax.nn.softmax(lb_logits.astype(F32), axis=0), axis=0)
    lower_bound = jnp.maximum(lb_cum - lb_cum[0:1], 0.0)

    cc = jnp.zeros((MOD_ROWS, d), F32).at[:b].set(c).at[b].set(c_ctx)
    mod = _modulation(cc, w_mod, b_mod)

    w_route = jnp.concatenate([w_group, w_router], axis=-1)
    w_route = jnp.pad(w_route, ((0, 0), (0, 0), (0, ROUTE_LANES - w_route.shape[-1])))
    wr_hi = w_route.astype(BF16)
    wr_lo = (w_route - wr_hi.astype(F32)).astype(BF16)
    b_route = jnp.concatenate([b_group, b_router], axis=-1)
    b_route = jnp.pad(b_route, ((0, 0), (0, ROUTE_LANES - b_route.shape[-1])))[:, None, :]
    b_s = jnp.broadcast_to(sgu_b[..., None], sgu_b.shape + (HEAD_DIM,)).astype(F32)

    w_in_b, w_out_b, sgu_w_b = w_in.astype(BF16), w_out.astype(BF16), sgu_w.astype(BF16)

    xa, xb, t_a, t_b = x.reshape(t_lat, d), ctx.reshape(t_ctx, d), t_lat, t_ctx
    fn = final_norm.reshape(1, d)

    for layer in range(depth):
        last = layer == depth - 1
        mod3 = mod[layer, :b + 1].reshape(b + 1, 1, N_MOD * d)
        n1 = norm1[layer].reshape(1, d)
        n2 = norm2[layer].reshape(1, d)
        sgn = sgu_norm[layer].reshape(1, HW)
        hgain = hgrn_norm[layer].reshape(1, HW)

        q, i, lf_f, lf_b, sg, gu, vn = _inproj(xa, xb, t_a, t_b, mod3, l, n1, w_in_b[layer],
                                               lower_bound[layer], sgn, layer == 0)
        o_f, o_b = _hgrn(q, i, lf_f, lf_b, b, l, lc)

        if last:
            xb, t_a, t_b = None, t_lat, 0
        ttot = t_a + t_b
        xs, h2, route = _mixer(o_f, o_b, sg, gu, vn, xa, xb, t_a, t_b, mod3, l, hgain, sgu_w_b[layer],
                               b_s[layer], w_out_b[layer], n2, wr_hi[layer], wr_lo[layer], b_route[layer])

        expert_flat = route[:, :TOP_K].astype(jnp.int32).T.reshape(-1)
        tab, last_step, sexp, n_out_rows = _dispatch_tables(expert_flat, ttot, nd)
        y2 = _moe(h2, tab, last_step, sexp, n_out_rows, w_gate, w_up, w_down, layer)
        xs = _combine(xs, y2, ttot, ttot, route, mod3, l, fn, final=last)

        xa, xb, t_a, t_b = xs, None, ttot, 0

    return xs.reshape(b, l, d)
```

```python
import functools

import jax
import jax.numpy as jnp
from jax import lax
from jax.experimental import pallas as pl
from jax.experimental.pallas import tpu as pltpu
from jax.experimental.pallas import tpu_sc as plsc

F32 = jnp.float32
BF16 = jnp.bfloat16

EPS = 1e-6
HEADS = 4
HEAD_DIM = 128
HW = HEADS * HEAD_DIM
HGRN_CHUNK = 64
SGU_CHUNK = 128
N_GROUPS = 4
EXPERTS_PER_GROUP = 8
N_EXPERTS = N_GROUPS * EXPERTS_PER_GROUP
TOP_K = 2
N_MOD = 6
LANES = 128
SUBLANES = 8
ROUTE_LANES = LANES
MOD_ROWS = 16

PROJ_ROWS = 512
SCAN_ROWS = 256
MIX_ROWS = 256

VMEM_LIMIT = 48 * 1024 * 1024


def _cparams(*sem):
    return pltpu.CompilerParams(dimension_semantics=sem, vmem_limit_bytes=VMEM_LIMIT)


def _split2(a):
    hi = a.astype(BF16)
    lo = (a - hi.astype(F32)).astype(BF16)
    return hi, lo


def _dot(a, b):
    return jnp.dot(a, b, preferred_element_type=F32)


def _dot_nt(a, b):
    return lax.dot_general(a, b, (((1,), (1,)), ((), ())), preferred_element_type=F32)


def _dot_tn(a, b):
    return lax.dot_general(a, b, (((0,), (0,)), ((), ())), preferred_element_type=F32)


def _dot3(a, b):
    ah, al = _split2(a)
    bh, bl = _split2(b)
    return _dot(ah, bh) + (_dot(al, bh) + _dot(ah, bl))


def _silu(x):
    return x / (1.0 + jnp.exp(-x))


def _rms(x):
    return x * lax.rsqrt(jnp.mean(x * x, axis=-1, keepdims=True) + EPS)


def _full(a):
    return pl.BlockSpec(a.shape, lambda *_: (0,) * a.ndim)


def _stream_inputs(xa, xb, n_a, tm, d):
    if xb is None:
        return [xa], [pl.BlockSpec((tm, d), lambda i: (i, 0))]
    return [xa, xb], [pl.BlockSpec((tm, d), lambda i: (jnp.minimum(i, n_a - 1), 0)),
                      pl.BlockSpec((tm, d), lambda i: (jnp.maximum(i - n_a, 0), 0))]


def _mod_row(d, tiles_per_row, n_rows, j):
    return pl.BlockSpec((None, 1, d), lambda i: (jnp.minimum(i // tiles_per_row, n_rows - 1), 0, j))


def _store_token_tiles_cols(ref, x, j0, nd):
    rows, w = x.shape
    for j in range(w // LANES):
        ref[pl.ds(j0 + j, rows, stride=nd), :] = x[:, j * LANES:(j + 1) * LANES]


def _store_token_tiles(ref, x):
    _store_token_tiles_cols(ref, x, 0, x.shape[1] // LANES)


def _load_token_tiles(ref, rows, nd):
    return jnp.concatenate([ref[pl.ds(j, rows, stride=nd), :] for j in range(nd)], axis=-1)


def _mod_kernel(c_ref, w_ref, b_ref, o_ref):
    o_ref[...] = _dot3(_silu(c_ref[...]), w_ref[...]) + b_ref[...]


def _modulation(cc, w_mod, b_mod):
    depth, d, n = w_mod.shape
    tn = 1536
    return pl.pallas_call(
        _mod_kernel,
        out_shape=jax.ShapeDtypeStruct((depth, MOD_ROWS, n), F32),
        grid=(depth, n // tn),
        in_specs=[
            pl.BlockSpec((MOD_ROWS, d), lambda l, j: (0, 0)),
            pl.BlockSpec((None, d, tn), lambda l, j: (l, 0, j)),
            pl.BlockSpec((None, 1, tn), lambda l, j: (l, 0, j)),
        ],
        out_specs=pl.BlockSpec((None, MOD_ROWS, tn), lambda l, j: (l, 0, j)),
        compiler_params=_cparams("arbitrary", "arbitrary"),
        name="modulation",
    )(cc, w_mod, b_mod.reshape(depth, 1, n))


def _log_forget(z, lb, lb_is_zero):
    ls = jnp.minimum(z, 0.0) - jnp.log(1.0 + jnp.exp(-jnp.abs(z)))
    if lb_is_zero:
        return ls
    a = jnp.log(1.0 - lb) + ls
    b = jnp.log(lb)
    return jnp.maximum(a, b) + jnp.log(1.0 + jnp.exp(-jnp.abs(a - b)))


def _stream_tile(xa_ref, xb_ref, n_a_tiles):
    if xb_ref is None:
        return xa_ref[...]
    return jnp.where(pl.program_id(0) < n_a_tiles, xa_ref[...], xb_ref[...])


def _inproj_kernel(*refs, lb_is_zero, n_a_tiles, two_inputs):
    xa_ref, xb_ref = (refs[0], refs[1]) if two_inputs else (refs[0], None)
    (sh_ref, sc_ref, n1_ref, w_ref, lb_ref, sgn_ref,
     q_ref, i_ref, lff_ref, lfb_ref, sg_ref, gu_ref, vn_ref) = refs[2 if two_inputs else 1:]
    h = _rms(_stream_tile(xa_ref, xb_ref, n_a_tiles)) * n1_ref[...]
    hb = (h * (1.0 + sc_ref[...]) + sh_ref[...]).astype(BF16)

    def proj(j):
        return _dot(hb, w_ref[:, j * HW:(j + 1) * HW])

    q_ref[...] = proj(0).astype(BF16)
    lff_ref[...] = _log_forget(proj(1), lb_ref[0:1, :], lb_is_zero)
    lfb_ref[...] = _log_forget(proj(2), lb_ref[1:2, :], lb_is_zero)
    i_ref[...] = proj(3).astype(BF16)
    sg_ref[...] = _silu(proj(4)).astype(BF16)
    gu_ref[...] = jax.nn.gelu(proj(5)).astype(BF16)
    vn_ref[...] = (_rms(jax.nn.gelu(proj(6))) * sgn_ref[...]).astype(BF16)


def _inproj(xa, xb, t_a, t_b, mod3, tokens_per_mod_row, n1, w_in, lb, sgu_gain, lb_is_zero):
    d = xa.shape[1]
    tm = PROJ_ROWS
    n_a, n_b = t_a // tm, t_b // tm
    t = t_a + t_b
    tok = pl.BlockSpec((tm, HW), lambda i: (i, 0))
    row = functools.partial(_mod_row, d, tokens_per_mod_row // tm, mod3.shape[0])
    x_in, x_spec = _stream_inputs(xa, xb, n_a, tm, d)
    return pl.pallas_call(
        functools.partial(_inproj_kernel, lb_is_zero=lb_is_zero, n_a_tiles=n_a, two_inputs=xb is not None),
        out_shape=[jax.ShapeDtypeStruct((t, HW), dt) for dt in (BF16, BF16, F32, F32, BF16, BF16, BF16)],
        grid=(n_a + n_b,),
        in_specs=x_spec + [row(0), row(1), _full(n1), _full(w_in), _full(lb), _full(sgu_gain)],
        out_specs=[tok] * 7,
        compiler_params=_cparams("arbitrary"),
        name="inproj",
    )(*x_in, mod3, mod3, n1, w_in, lb, sgu_gain)


def _chunk_cumsum(x, reverse):
    c, w = x.shape
    g = c // SUBLANES
    x3 = x.reshape(g, SUBLANES, w)
    sub = lax.broadcasted_iota(jnp.int32, x3.shape, 1)
    for s in (1, 2, 4):
        if reverse:
            x3 = x3 + jnp.where(sub < SUBLANES - s, pltpu.roll(x3, SUBLANES - s, axis=1), 0.0)
        else:
            x3 = x3 + jnp.where(sub >= s, pltpu.roll(x3, s, axis=1), 0.0)
    edge = 0 if reverse else SUBLANES - 1
    tot = x3[:, edge:edge + 1, :]
    offs = [None] * g
    acc = jnp.zeros((1, w), F32)
    for gi in (reversed(range(g)) if reverse else range(g)):
        offs[gi] = acc
        acc = acc + tot[gi]
    x3 = x3 + jnp.stack(offs, axis=0)
    return x3.reshape(c, w)


def _scan_chunk(q_ref, i_ref, lf_ref, o_ref, st_ref, r0, reverse):
    c = HGRN_CHUNK
    rows = lax.broadcasted_iota(jnp.int32, (c, c), 0)
    cols = lax.broadcasted_iota(jnp.int32, (c, c), 1)
    incl = (cols >= rows) if reverse else (cols <= rows)
    ref_row = c // 2 if reverse else c // 2 - 1
    tot_row = 0 if reverse else c - 1
    lf = lf_ref[pl.ds(r0, c), :]
    cum = _chunk_cumsum(lf, reverse)
    ref = cum[ref_row:ref_row + 1, :]
    tot = cum[tot_row:tot_row + 1, :]
    k = 1.0 - jnp.exp(lf)
    qf = q_ref[pl.ds(r0, c), :].astype(F32)
    iv = i_ref[pl.ds(r0, c), :]
    q_in = (qf * jnp.exp(cum - ref)).astype(BF16)
    k_in = (k * jnp.exp(ref - cum)).astype(BF16)
    k_st = (k * jnp.exp(tot - cum)).astype(BF16)
    q_st = (qf * jnp.exp(cum)).astype(BF16)
    dec = jnp.exp(tot)
    for h in range(HEADS):
        sl = slice(h * HEAD_DIM, (h + 1) * HEAD_DIM)
        sc = _dot_nt(q_in[:, sl], k_in[:, sl])
        sc = jnp.where(incl, sc, 0.0).astype(BF16)
        st = st_ref[h]
        o_ref[pl.ds(r0, c), sl] = _dot(sc, iv[:, sl]) + _dot_nt(q_st[:, sl], st.astype(BF16))
        st_ref[h] = st * dec[:, sl] + _dot_tn(iv[:, sl], k_st[:, sl])


def _hgrn_kernel(qf_ref, if_ref, lff_ref, qb_ref, ib_ref, lfb_ref, of_ref, ob_ref, stf_ref, stb_ref, *, tt):
    @pl.when(pl.program_id(1) == 0)
    def _():
        stf_ref[...] = jnp.zeros_like(stf_ref)
        stb_ref[...] = jnp.zeros_like(stb_ref)

    nchunks = tt // HGRN_CHUNK
    for ci in range(nchunks):
        _scan_chunk(qf_ref, if_ref, lff_ref, of_ref, stf_ref, ci * HGRN_CHUNK, False)
        _scan_chunk(qb_ref, ib_ref, lfb_ref, ob_ref, stb_ref, (nchunks - 1 - ci) * HGRN_CHUNK, True)


def _hgrn(q, i, lf_f, lf_b, batch, seq, ctx_len):
    t = q.shape[0]
    tt = SCAN_ROWS
    nt, nc = seq // tt, ctx_len // tt
    ctx0 = batch * nt

    def fwd(b, s):
        return (jnp.where(s < nc, ctx0 + b * nc + s, b * nt + (s - nc)), 0)

    def bwd(b, s):
        return (jnp.where(s < nc, ctx0 + b * nc + (nc - 1 - s), b * nt + (nt - 1 - (s - nc))), 0)

    tf, tb = pl.BlockSpec((tt, HW), fwd), pl.BlockSpec((tt, HW), bwd)
    return pl.pallas_call(
        functools.partial(_hgrn_kernel, tt=tt),
        out_shape=[jax.ShapeDtypeStruct((t, HW), F32)] * 2,
        grid=(batch, nc + nt),
        in_specs=[tf, tf, tf, tb, tb, tb],
        out_specs=[tf, tb],
        scratch_shapes=[pltpu.VMEM((HEADS, HEAD_DIM, HEAD_DIM), F32)] * 2,
        compiler_params=_cparams("arbitrary", "arbitrary"),
        name="hgrn",
    )(q, i, lf_f, q, i, lf_b)


def _route(logits):
    lane = lax.broadcasted_iota(jnp.int32, logits.shape, 1).astype(F32)
    neg = -jnp.inf
    is_group = lane < N_GROUPS
    gl = jnp.where(is_group, logits, neg)
    gmax = jnp.max(gl, axis=-1, keepdims=True)
    g_sel = jnp.min(jnp.where(gl == gmax, lane, float(ROUTE_LANES)), axis=-1, keepdims=True)
    den = jnp.sum(jnp.where(is_group, jnp.exp(logits - gmax), 0.0), axis=-1, keepdims=True)
    p_sel = 1.0 / den
    first = N_GROUPS + EXPERTS_PER_GROUP * g_sel
    el = jnp.where((lane >= first) & (lane < first + EXPERTS_PER_GROUP), logits, neg)
    t1 = jnp.max(el, axis=-1, keepdims=True)
    i1 = jnp.min(jnp.where(el == t1, lane, float(ROUTE_LANES)), axis=-1, keepdims=True)
    el2 = jnp.where(lane == i1, neg, el)
    t2 = jnp.max(el2, axis=-1, keepdims=True)
    i2 = jnp.min(jnp.where(el2 == t2, lane, float(ROUTE_LANES)), axis=-1, keepdims=True)
    e2 = jnp.exp(t2 - t1)
    w1 = p_sel / (1.0 + e2)
    w2 = p_sel * e2 / (1.0 + e2)
    rec = jnp.where(lane == 0.0, i1 - N_GROUPS, 0.0)
    rec = jnp.where(lane == 1.0, i2 - N_GROUPS, rec)
    rec = jnp.where(lane == 2.0, w1, rec)
    return jnp.where(lane == 3.0, w2, rec)


def _mixer_kernel(of_ref, ob_ref, sg_ref, gu_ref, vn_ref, *refs, tm, n_a_tiles, two_inputs):
    xa_ref, xb_ref = (refs[0], refs[1]) if two_inputs else (refs[0], None)
    (hgain_ref, ws_ref, bs_ref, wo_ref, g1_ref, n2_ref, sh2_ref, sc2_ref, wrh_ref, wrl_ref, br_ref,
     xo_ref, h2_ref, rt_ref, cat_ref) = refs[2 if two_inputs else 1:]
    o = of_ref[...] + ob_ref[...]
    for h in range(HEADS):
        sl = slice(h * HEAD_DIM, (h + 1) * HEAD_DIM)
        hg = _rms(o[:, sl]) * hgain_ref[:, sl] * sg_ref[:, sl].astype(F32)
        cat_ref[:, sl] = hg.astype(BF16)
    for cc in range(tm // SGU_CHUNK):
        rows = slice(cc * SGU_CHUNK, (cc + 1) * SGU_CHUNK)
        for h in range(HEADS):
            sl = slice(h * HEAD_DIM, (h + 1) * HEAD_DIM)
            mixed = _dot(ws_ref[h], vn_ref[rows, sl]) + bs_ref[h]
            cat_ref[rows, HW + h * HEAD_DIM:HW + (h + 1) * HEAD_DIM] = (
                gu_ref[rows, sl].astype(F32) * mixed).astype(BF16)
    xn = _stream_tile(xa_ref, xb_ref, n_a_tiles) + g1_ref[...] * _dot(cat_ref[...], wo_ref[...])
    xo_ref[...] = xn
    h2 = _rms(xn) * n2_ref[...]
    h2 = h2 * (1.0 + sc2_ref[...]) + sh2_ref[...]
    _store_token_tiles(h2_ref, h2)
    hi, lo = _split2(h2)
    logits = _dot(hi, wrh_ref[...]) + (_dot(lo, wrh_ref[...]) + _dot(hi, wrl_ref[...])) + br_ref[...]
    rt_ref[...] = _route(logits)


def _mixer(o_f, o_b, sg, gu, vn, xa, xb, t_a, t_b, mod3, tokens_per_mod_row, hgain, w_s, b_s, w_out, n2,
           wr_hi, wr_lo, br):
    d = xa.shape[1]
    tm = MIX_ROWS
    n_a, n_b = t_a // tm, t_b // tm
    t = t_a + t_b
    nd = d // LANES
    row = functools.partial(_mod_row, d, tokens_per_mod_row // tm, mod3.shape[0])
    tok = pl.BlockSpec((tm, HW), lambda i: (i, 0))
    wide = pl.BlockSpec((tm, d), lambda i: (i, 0))
    x_in, x_spec = _stream_inputs(xa, xb, n_a, tm, d)
    return pl.pallas_call(
        functools.partial(_mixer_kernel, tm=tm, n_a_tiles=n_a, two_inputs=xb is not None),
        out_shape=[jax.ShapeDtypeStruct((t, d), F32), jax.ShapeDtypeStruct((t * nd, LANES), F32),
                   jax.ShapeDtypeStruct((t, ROUTE_LANES), F32)],
        grid=(n_a + n_b,),
        in_specs=[tok, tok, tok, tok, tok] + x_spec + [_full(hgain), _full(w_s), _full(b_s), _full(w_out),
                  row(2), _full(n2), row(3), row(4), _full(wr_hi), _full(wr_lo), _full(br)],
        out_specs=[wide, pl.BlockSpec((tm * nd, LANES), lambda i: (i, 0)),
                   pl.BlockSpec((tm, ROUTE_LANES), lambda i: (i, 0))],
        scratch_shapes=[pltpu.VMEM((tm, 2 * HW), BF16)],
        compiler_params=_cparams("arbitrary"),
        name="mixer",
    )(o_f, o_b, sg, gu, vn, *x_in, hgain, w_s, b_s, w_out, mod3, n2, mod3, mod3, wr_hi, wr_lo, br)


MOE_ROWS = 256
RING = 3
DMA_GROUPS = 8

SC_CORES = 2
SC_SUBCORES = 16
SC_CHUNK = 32


def _sc_gather_rows(table, idx):
    nw, n_chunks, ch = idx.shape
    _, nd, lanes = table.shape
    n_rows = nw * n_chunks * ch
    assert nw == SC_CORES * SC_SUBCORES and n_chunks % 2 == 0
    mesh = plsc.VectorSubcoreMesh(core_axis_name="c", subcore_axis_name="s",
                                  num_cores=SC_CORES, num_subcores=SC_SUBCORES)

    @functools.partial(
        pl.kernel, mesh=mesh,
        out_type=jax.ShapeDtypeStruct((n_rows, nd, lanes), F32),
        scratch_types=[pltpu.VMEM((n_chunks, ch), jnp.int32),
                       pltpu.VMEM((2, ch, nd, lanes), F32),
                       pltpu.SemaphoreType.DMA((2,)),
                       pltpu.SemaphoreType.DMA((2,))],
        compiler_params=pltpu.CompilerParams(use_tc_tiling_on_sc=True),
        name="sc_gather_rows",
    )
    def gather_kernel(table_hbm, idx_hbm, out_hbm, idx_v, rows_v, sem_g, sem_w):
        wid = lax.axis_index("s") * SC_CORES + lax.axis_index("c")
        base = wid * (n_chunks * ch)
        pltpu.sync_copy(idx_hbm.at[wid], idx_v)

        def gather(j, b):
            return pltpu.make_async_copy(table_hbm.at[idx_v.at[j]], rows_v.at[b], sem_g.at[b])

        def write(j, b):
            return pltpu.make_async_copy(rows_v.at[b], out_hbm.at[pl.ds(base + j * ch, ch)], sem_w.at[b])

        gather(0, 0).start()

        @pl.loop(0, n_chunks, step=2)
        def _(j0):
            for b in range(2):
                j = j0 + b

                @pl.when(j + 1 < n_chunks)
                def _():
                    @pl.when(j >= 1)
                    def _():
                        write(j - 1, 1 - b).wait()

                    gather(j + 1, 1 - b).start()

                gather(j, b).wait()
                write(j, b).start()

        write(n_chunks - 2, 0).wait()
        write(n_chunks - 1, 1).wait()

    return gather_kernel(table, idx)


def _moe_kernel(last_ref, sexp_ref, stab_hbm, xs_ref, wg_ref, wu_ref, wd_ref, y_hbm,
                idx_ref, ybuf, zbuf, xb_ref, hm_ref, wgb, wub, wdb, sem_idx, sem_s, sem_z):
    i = pl.program_id(0)
    last = last_ref[0]
    bm = MOE_ROWS
    de = wgb.shape[1]
    d = wgb.shape[0]
    nd = d // LANES

    def idx_copy(step):
        s = step % 2
        return pltpu.make_async_copy(stab_hbm.at[step], idx_ref.at[s], sem_idx.at[s])

    def scattered(slot):
        return pltpu.make_async_copy(ybuf.at[slot], y_hbm.at[pl.ds(0, bm * nd)], sem_s.at[slot])

    @pl.when(i <= last)
    def _():
        islot = i % 2
        cslot = i % RING
        sslot = (i + 2) % RING

        @pl.when(i == 0)
        def _():
            ybuf[...] = jnp.zeros_like(ybuf)
            zbuf[...] = jnp.zeros_like(zbuf)
            idx_copy(0).start()

        idx_copy(i).wait()

        @pl.when(i < last)
        def _():
            idx_copy(i + 1).start()

        @pl.when(i >= 2)
        def _():
            scattered(cslot).wait()

        @pl.when((i == 0) | (sexp_ref[i] != sexp_ref[jnp.maximum(i - 1, 0)]))
        def _():
            wgb[...] = wg_ref[...].astype(BF16)
            wub[...] = wu_ref[...].astype(BF16)
            wdb[...] = wd_ref[...].astype(BF16)

        per = bm // DMA_GROUPS

        def scatter_group(g):
            for r in range(g * per, (g + 1) * per):
                dst = pl.multiple_of(idx_ref[islot, r], nd)
                pltpu.make_async_copy(ybuf.at[sslot, pl.ds(r * nd, nd)], y_hbm.at[pl.ds(dst, nd)],
                                      sem_s.at[sslot]).start(priority=r % 2)

        dma_groups = [functools.partial(scatter_group, g) for g in range(DMA_GROUPS)]

        def issue_some():
            if dma_groups:
                dma_groups.pop(0)()

        for j in range(nd):
            xb_ref[:, j * LANES:(j + 1) * LANES] = xs_ref[pl.ds(j, bm, stride=nd), :].astype(BF16)
        nh = 2
        for j in range(nh):
            cs = slice(j * de // nh, (j + 1) * de // nh)
            issue_some()
            gate = _dot(xb_ref[...], wgb[:, cs])
            issue_some()
            hm_ref[:, cs] = (_silu(gate) * _dot(xb_ref[...], wub[:, cs])).astype(BF16)
        ydst = ybuf.at[cslot]
        n_down = min(4, nd)
        for j in range(n_down):
            issue_some()
            _store_token_tiles_cols(ydst, _dot(hm_ref[...], wdb[:, j * d // n_down:(j + 1) * d // n_down]),
                                    j * nd // n_down, nd)
        while dma_groups:
            issue_some()

        @pl.when(i == last)
        def _():
            scattered(sslot).wait()
            scattered((i + 1) % RING).wait()

    @pl.when(i > last)
    def _():
        fill = pltpu.make_async_copy(zbuf, y_hbm.at[pl.ds((i - 1) * (bm * nd), bm * nd)], sem_z.at[0])
        fill.start()
        fill.wait()


def _moe(xs, stab, last, sexp, w_gate, w_up, w_down, layer):
    d, de = w_gate.shape[-2:]
    nd = d // LANES
    n_steps = stab.shape[0]
    n_blocks = n_steps - 1
    grid_spec = pltpu.PrefetchScalarGridSpec(
        num_scalar_prefetch=2,
        grid=(n_steps,),
        in_specs=[
            pl.BlockSpec(memory_space=pl.ANY),
            pl.BlockSpec((MOE_ROWS * nd, LANES), lambda i, la, se: (jnp.minimum(i, n_blocks - 1), 0)),
            pl.BlockSpec((None, None, d, de), lambda i, la, se: (layer, se[i], 0, 0)),
            pl.BlockSpec((None, None, d, de), lambda i, la, se: (layer, se[i], 0, 0)),
            pl.BlockSpec((None, None, de, d), lambda i, la, se: (layer, se[i], 0, 0)),
        ],
        out_specs=pl.BlockSpec(memory_space=pl.ANY),
        scratch_shapes=[
            pltpu.SMEM((2, MOE_ROWS), jnp.int32),
            pltpu.VMEM((RING, MOE_ROWS * nd, LANES), F32),
            pltpu.VMEM((MOE_ROWS * nd, LANES), F32),
            pltpu.VMEM((MOE_ROWS, d), BF16),
            pltpu.VMEM((MOE_ROWS, de), BF16),
            pltpu.VMEM((d, de), BF16),
            pltpu.VMEM((d, de), BF16),
            pltpu.VMEM((de, d), BF16),
            pltpu.SemaphoreType.DMA((2,)),
            pltpu.SemaphoreType.DMA((RING,)),
            pltpu.SemaphoreType.DMA((1,)),
        ],
    )
    return pl.pallas_call(
        _moe_kernel,
        out_shape=jax.ShapeDtypeStruct((n_steps * MOE_ROWS * nd, LANES), F32),
        grid_spec=grid_spec,
        compiler_params=_cparams("arbitrary"),
        name="moe",
    )(last, sexp, stab, xs, w_gate, w_up, w_down)


def _dispatch_tables(expert_flat, ttot, nd):
    bm = MOE_ROWS
    n_slots = expert_flat.shape[0]
    n_blocks = -(-n_slots // bm) + N_EXPERTS
    _, order = lax.sort_key_val(expert_flat, lax.iota(jnp.int32, n_slots))
    counts = jnp.sum(expert_flat[:, None] == jnp.arange(N_EXPERTS, dtype=jnp.int32)[None, :], axis=0,
                     dtype=jnp.int32)
    padded = (counts + bm - 1) // bm * bm
    pad_end = jnp.cumsum(padded)
    pad_start = pad_end - padded
    start = jnp.cumsum(counts) - counts
    blk_row0 = jnp.arange(n_blocks, dtype=jnp.int32) * bm
    bexp = jnp.minimum(jnp.sum(pad_end[None, :] <= blk_row0[:, None], axis=1), N_EXPERTS - 1).astype(jnp.int32)
    lane = jnp.arange(bm, dtype=jnp.int32)[None, :]
    off = (blk_row0 - pad_start[bexp])[:, None] + lane
    valid = off < counts[bexp][:, None]
    slot = order[jnp.clip(start[bexp][:, None] + off, 0, n_slots - 1)]
    pad_rank = blk_row0[:, None] + lane - (start[bexp] + counts[bexp])[:, None]
    gsrc = jnp.where(valid, slot % ttot, 0).astype(jnp.int32)
    sdst = jnp.where(valid, slot, n_slots + pad_rank)
    spare = n_blocks * bm + jnp.arange(bm, dtype=jnp.int32)[None, :]
    stab = (jnp.concatenate([spare, sdst], axis=0) * nd).astype(jnp.int32)
    last = (jnp.sum(padded) // bm).astype(jnp.int32).reshape(1)
    sexp = jnp.concatenate([bexp, bexp[-1:]])
    return gsrc, stab, last, sexp


def _combine_kernel(x_ref, y0_ref, y1_ref, rt_ref, g2_ref, fn_ref, o_ref, *, final):
    w = rt_ref[...]
    tm, d = x_ref.shape
    nd = d // LANES
    f = w[:, 2:3] * _load_token_tiles(y0_ref, tm, nd) + w[:, 3:4] * _load_token_tiles(y1_ref, tm, nd)
    xn = x_ref[...] + g2_ref[...] * f
    if final:
        xn = _rms(xn) * fn_ref[...]
    o_ref[...] = xn


def _combine(x2d, y2, t, ttot, route, mod3, tokens_per_mod_row, final_norm, final):
    d = x2d.shape[1]
    tm = MIX_ROWS
    b1 = ttot // tm
    nd = d // LANES
    return pl.pallas_call(
        functools.partial(_combine_kernel, final=final),
        out_shape=jax.ShapeDtypeStruct((t, d), F32),
        grid=(t // tm,),
        in_specs=[pl.BlockSpec((tm, d), lambda i: (i, 0)),
                  pl.BlockSpec((tm * nd, LANES), lambda i: (i, 0)),
                  pl.BlockSpec((tm * nd, LANES), lambda i: (i + b1, 0)),
                  pl.BlockSpec((tm, ROUTE_LANES), lambda i: (i, 0)),
                  _mod_row(d, tokens_per_mod_row // tm, mod3.shape[0], 5),
                  _full(final_norm)],
        out_specs=pl.BlockSpec((tm, d), lambda i: (i, 0)),
        compiler_params=_cparams("arbitrary"),
        name="combine",
    )(x2d, y2, y2, route, mod3, final_norm)


def kernel(x, c, ctx, c_ctx, norm1, norm2, w_mod, b_mod, w_in, lb_logits, hgrn_norm, sgu_norm, sgu_w, sgu_b,
           w_out, w_group, b_group, w_router, b_router, w_gate, w_up, w_down, final_norm):
    b, l, d = x.shape
    lc = ctx.shape[1]
    depth = w_mod.shape[0]
    t_lat, t_ctx = b * l, b * lc
    nd = d // LANES

    lb_cum = jnp.cumsum(jax.nn.softmax(lb_logits.astype(F32), axis=0), axis=0)
    lower_bound = jnp.maximum(lb_cum - lb_cum[0:1], 0.0)

    cc = jnp.zeros((MOD_ROWS, d), F32).at[:b].set(c).at[b].set(c_ctx)
    mod = _modulation(cc, w_mod, b_mod)

    w_route = jnp.concatenate([w_group, w_router], axis=-1)
    w_route = jnp.pad(w_route, ((0, 0), (0, 0), (0, ROUTE_LANES - w_route.shape[-1])))
    wr_hi = w_route.astype(BF16)
    wr_lo = (w_route - wr_hi.astype(F32)).astype(BF16)
    b_route = jnp.concatenate([b_group, b_router], axis=-1)
    b_route = jnp.pad(b_route, ((0, 0), (0, ROUTE_LANES - b_route.shape[-1])))[:, None, :]
    b_s = jnp.broadcast_to(sgu_b[..., None], sgu_b.shape + (HEAD_DIM,)).astype(F32)

    w_in_b, w_out_b, sgu_w_b = w_in.astype(BF16), w_out.astype(BF16), sgu_w.astype(BF16)

    xa, xb, t_a, t_b = x.reshape(t_lat, d), ctx.reshape(t_ctx, d), t_lat, t_ctx
    fn = final_norm.reshape(1, d)

    for layer in range(depth):
        last = layer == depth - 1
        mod3 = mod[layer, :b + 1].reshape(b + 1, 1, N_MOD * d)
        n1 = norm1[layer].reshape(1, d)
        n2 = norm2[layer].reshape(1, d)
        sgn = sgu_norm[layer].reshape(1, HW)
        hgain = hgrn_norm[layer].reshape(1, HW)

        q, i, lf_f, lf_b, sg, gu, vn = _inproj(xa, xb, t_a, t_b, mod3, l, n1, w_in_b[layer],
                                               lower_bound[layer], sgn, layer == 0)
        o_f, o_b = _hgrn(q, i, lf_f, lf_b, b, l, lc)

        if last:
            xb, t_a, t_b = None, t_lat, 0
        ttot = t_a + t_b
        xs, h2, route = _mixer(o_f, o_b, sg, gu, vn, xa, xb, t_a, t_b, mod3, l, hgain, sgu_w_b[layer],
                               b_s[layer], w_out_b[layer], n2, wr_hi[layer], wr_lo[layer], b_route[layer])

        expert_flat = route[:, :TOP_K].astype(jnp.int32).T.reshape(-1)
        gsrc, stab, last_step, sexp = _dispatch_tables(expert_flat, ttot, nd)
        xs_rows = _sc_gather_rows(h2.reshape(ttot, nd, LANES), gsrc.reshape(SC_CORES * SC_SUBCORES, -1, SC_CHUNK))
        y2 = _moe(xs_rows.reshape(-1, LANES), stab, last_step, sexp, w_gate, w_up, w_down, layer)
        xs = _combine(xs, y2, ttot, ttot, route, mod3, l, fn, final=last)

        xa, xb, t_a, t_b = xs, None, ttot, 0

    return xs.reshape(b, l, d)
```

```python
import functools

import jax
import jax.numpy as jnp
from jax import lax
from jax.experimental import pallas as pl
from jax.experimental.pallas import tpu as pltpu

F32 = jnp.float32
BF16 = jnp.bfloat16

EPS = 1e-6
HEADS = 4
HEAD_DIM = 128
HW = HEADS * HEAD_DIM
HGRN_CHUNK = 64
SGU_CHUNK = 128
N_GROUPS = 4
EXPERTS_PER_GROUP = 8
N_EXPERTS = N_GROUPS * EXPERTS_PER_GROUP
TOP_K = 2
N_MOD = 6
LANES = 128
SUBLANES = 8
ROUTE_LANES = LANES
MOD_ROWS = 16

PROJ_ROWS = 512
SCAN_ROWS = 256
MIX_ROWS = 256

VMEM_LIMIT = 48 * 1024 * 1024


def _cparams(*sem):
    return pltpu.CompilerParams(dimension_semantics=sem, vmem_limit_bytes=VMEM_LIMIT)


def _split2(a):
    hi = a.astype(BF16)
    lo = (a - hi.astype(F32)).astype(BF16)
    return hi, lo


def _dot(a, b):
    return jnp.dot(a, b, preferred_element_type=F32)


def _dot_nt(a, b):
    return lax.dot_general(a, b, (((1,), (1,)), ((), ())), preferred_element_type=F32)


def _dot_tn(a, b):
    return lax.dot_general(a, b, (((0,), (0,)), ((), ())), preferred_element_type=F32)


def _dot3(a, b):
    ah, al = _split2(a)
    bh, bl = _split2(b)
    return _dot(ah, bh) + (_dot(al, bh) + _dot(ah, bl))


def _silu(x):
    return x / (1.0 + jnp.exp(-x))


def _rms(x):
    return x * lax.rsqrt(jnp.mean(x * x, axis=-1, keepdims=True) + EPS)


def _full(a):
    return pl.BlockSpec(a.shape, lambda *_: (0,) * a.ndim)


def _stream_inputs(xa, xb, n_a, tm, d):
    if xb is None:
        return [xa], [pl.BlockSpec((tm, d), lambda i: (i, 0))]
    return [xa, xb], [pl.BlockSpec((tm, d), lambda i: (jnp.minimum(i, n_a - 1), 0)),
                      pl.BlockSpec((tm, d), lambda i: (jnp.maximum(i - n_a, 0), 0))]


def _mod_row(d, tiles_per_row, n_rows, j):
    return pl.BlockSpec((None, 1, d), lambda i: (jnp.minimum(i // tiles_per_row, n_rows - 1), 0, j))


def _store_token_tiles_cols(ref, x, j0, nd):
    rows, w = x.shape
    for j in range(w // LANES):
        ref[pl.ds(j0 + j, rows, stride=nd), :] = x[:, j * LANES:(j + 1) * LANES]


def _store_token_tiles(ref, x):
    _store_token_tiles_cols(ref, x, 0, x.shape[1] // LANES)


def _load_token_tiles(ref, rows, nd):
    return jnp.concatenate([ref[pl.ds(j, rows, stride=nd), :] for j in range(nd)], axis=-1)


def _mod_kernel(c_ref, w_ref, b_ref, o_ref):
    o_ref[...] = _dot3(_silu(c_ref[...]), w_ref[...]) + b_ref[...]


def _modulation(cc, w_mod, b_mod):
    depth, d, n = w_mod.shape
    tn = 1536
    return pl.pallas_call(
        _mod_kernel,
        out_shape=jax.ShapeDtypeStruct((depth, MOD_ROWS, n), F32),
        grid=(depth, n // tn),
        in_specs=[
            pl.BlockSpec((MOD_ROWS, d), lambda l, j: (0, 0)),
            pl.BlockSpec((None, d, tn), lambda l, j: (l, 0, j)),
            pl.BlockSpec((None, 1, tn), lambda l, j: (l, 0, j)),
        ],
        out_specs=pl.BlockSpec((None, MOD_ROWS, tn), lambda l, j: (l, 0, j)),
        compiler_params=_cparams("arbitrary", "arbitrary"),
        name="modulation",
    )(cc, w_mod, b_mod.reshape(depth, 1, n))


def _log_forget(z, lb, lb_is_zero):
    ls = jnp.minimum(z, 0.0) - jnp.log(1.0 + jnp.exp(-jnp.abs(z)))
    if lb_is_zero:
        return ls
    a = jnp.log(1.0 - lb) + ls
    b = jnp.log(lb)
    return jnp.maximum(a, b) + jnp.log(1.0 + jnp.exp(-jnp.abs(a - b)))


def _stream_tile(xa_ref, xb_ref, n_a_tiles):
    if xb_ref is None:
        return xa_ref[...]
    return jnp.where(pl.program_id(0) < n_a_tiles, xa_ref[...], xb_ref[...])


def _inproj_kernel(*refs, lb_is_zero, n_a_tiles, two_inputs):
    xa_ref, xb_ref = (refs[0], refs[1]) if two_inputs else (refs[0], None)
    (sh_ref, sc_ref, n1_ref, w_ref, lb_ref, sgn_ref,
     q_ref, i_ref, lff_ref, lfb_ref, sg_ref, gu_ref, vn_ref) = refs[2 if two_inputs else 1:]
    h = _rms(_stream_tile(xa_ref, xb_ref, n_a_tiles)) * n1_ref[...]
    hb = (h * (1.0 + sc_ref[...]) + sh_ref[...]).astype(BF16)

    def proj(j):
        return _dot(hb, w_ref[:, j * HW:(j + 1) * HW])

    q_ref[...] = proj(0).astype(BF16)
    lff_ref[...] = _log_forget(proj(1), lb_ref[0:1, :], lb_is_zero)
    lfb_ref[...] = _log_forget(proj(2), lb_ref[1:2, :], lb_is_zero)
    i_ref[...] = proj(3).astype(BF16)
    sg_ref[...] = _silu(proj(4)).astype(BF16)
    gu_ref[...] = jax.nn.gelu(proj(5)).astype(BF16)
    vn_ref[...] = (_rms(jax.nn.gelu(proj(6))) * sgn_ref[...]).astype(BF16)


def _inproj(xa, xb, t_a, t_b, mod3, tokens_per_mod_row, n1, w_in, lb, sgu_gain, lb_is_zero):
    d = xa.shape[1]
    tm = PROJ_ROWS
    n_a, n_b = t_a // tm, t_b // tm
    t = t_a + t_b
    tok = pl.BlockSpec((tm, HW), lambda i: (i, 0))
    row = functools.partial(_mod_row, d, tokens_per_mod_row // tm, mod3.shape[0])
    x_in, x_spec = _stream_inputs(xa, xb, n_a, tm, d)
    return pl.pallas_call(
        functools.partial(_inproj_kernel, lb_is_zero=lb_is_zero, n_a_tiles=n_a, two_inputs=xb is not None),
        out_shape=[jax.ShapeDtypeStruct((t, HW), dt) for dt in (BF16, BF16, F32, F32, BF16, BF16, BF16)],
        grid=(n_a + n_b,),
        in_specs=x_spec + [row(0), row(1), _full(n1), _full(w_in), _full(lb), _full(sgu_gain)],
        out_specs=[tok] * 7,
        compiler_params=_cparams("arbitrary"),
        name="inproj",
    )(*x_in, mod3, mod3, n1, w_in, lb, sgu_gain)


def _chunk_cumsum(x, reverse):
    c, w = x.shape
    g = c // SUBLANES
    x3 = x.reshape(g, SUBLANES, w)
    sub = lax.broadcasted_iota(jnp.int32, x3.shape, 1)
    for s in (1, 2, 4):
        if reverse:
            x3 = x3 + jnp.where(sub < SUBLANES - s, pltpu.roll(x3, SUBLANES - s, axis=1), 0.0)
        else:
            x3 = x3 + jnp.where(sub >= s, pltpu.roll(x3, s, axis=1), 0.0)
    edge = 0 if reverse else SUBLANES - 1
    tot = x3[:, edge:edge + 1, :]
    offs = [None] * g
    acc = jnp.zeros((1, w), F32)
    for gi in (reversed(range(g)) if reverse else range(g)):
        offs[gi] = acc
        acc = acc + tot[gi]
    x3 = x3 + jnp.stack(offs, axis=0)
    return x3.reshape(c, w)


def _scan_chunk(q_ref, i_ref, lf_ref, o_ref, st_ref, r0, reverse):
    c = HGRN_CHUNK
    rows = lax.broadcasted_iota(jnp.int32, (c, c), 0)
    cols = lax.broadcasted_iota(jnp.int32, (c, c), 1)
    incl = (cols >= rows) if reverse else (cols <= rows)
    ref_row = c // 2 if reverse else c // 2 - 1
    tot_row = 0 if reverse else c - 1
    lf = lf_ref[pl.ds(r0, c), :]
    cum = _chunk_cumsum(lf, reverse)
    ref = cum[ref_row:ref_row + 1, :]
    tot = cum[tot_row:tot_row + 1, :]
    k = 1.0 - jnp.exp(lf)
    qf = q_ref[pl.ds(r0, c), :].astype(F32)
    iv = i_ref[pl.ds(r0, c), :]
    q_in = (qf * jnp.exp(cum - ref)).astype(BF16)
    k_in = (k * jnp.exp(ref - cum)).astype(BF16)
    k_st = (k * jnp.exp(tot - cum)).astype(BF16)
    q_st = (qf * jnp.exp(cum)).astype(BF16)
    dec = jnp.exp(tot)
    for h in range(HEADS):
        sl = slice(h * HEAD_DIM, (h + 1) * HEAD_DIM)
        sc = _dot_nt(q_in[:, sl], k_in[:, sl])
        sc = jnp.where(incl, sc, 0.0).astype(BF16)
        st = st_ref[h]
        o_ref[pl.ds(r0, c), sl] = _dot(sc, iv[:, sl]) + _dot_nt(q_st[:, sl], st.astype(BF16))
        st_ref[h] = st * dec[:, sl] + _dot_tn(iv[:, sl], k_st[:, sl])


def _hgrn_kernel(qf_ref, if_ref, lff_ref, qb_ref, ib_ref, lfb_ref, of_ref, ob_ref, stf_ref, stb_ref, *, tt):
    @pl.when(pl.program_id(1) == 0)
    def _():
        stf_ref[...] = jnp.zeros_like(stf_ref)
        stb_ref[...] = jnp.zeros_like(stb_ref)

    nchunks = tt // HGRN_CHUNK
    for ci in range(nchunks):
        _scan_chunk(qf_ref, if_ref, lff_ref, of_ref, stf_ref, ci * HGRN_CHUNK, False)
        _scan_chunk(qb_ref, ib_ref, lfb_ref, ob_ref, stb_ref, (nchunks - 1 - ci) * HGRN_CHUNK, True)


def _hgrn(q, i, lf_f, lf_b, batch, seq, ctx_len):
    t = q.shape[0]
    tt = SCAN_ROWS
    nt, nc = seq // tt, ctx_len // tt
    ctx0 = batch * nt

    def fwd(b, s):
        return (jnp.where(s < nc, ctx0 + b * nc + s, b * nt + (s - nc)), 0)

    def bwd(b, s):
        return (jnp.where(s < nc, ctx0 + b * nc + (nc - 1 - s), b * nt + (nt - 1 - (s - nc))), 0)

    tf, tb = pl.BlockSpec((tt, HW), fwd), pl.BlockSpec((tt, HW), bwd)
    return pl.pallas_call(
        functools.partial(_hgrn_kernel, tt=tt),
        out_shape=[jax.ShapeDtypeStruct((t, HW), F32)] * 2,
        grid=(batch, nc + nt),
        in_specs=[tf, tf, tf, tb, tb, tb],
        out_specs=[tf, tb],
        scratch_shapes=[pltpu.VMEM((HEADS, HEAD_DIM, HEAD_DIM), F32)] * 2,
        compiler_params=_cparams("arbitrary", "arbitrary"),
        name="hgrn",
    )(q, i, lf_f, q, i, lf_b)


def _route_t(lt):
    neg = -jnp.inf
    blocks = [lt[SUBLANES * e:SUBLANES * (e + 1), :] for e in range(EXPERTS_PER_GROUP)]
    groups = lt[SUBLANES * EXPERTS_PER_GROUP:SUBLANES * (EXPERTS_PER_GROUP + 1), :]
    sub = lax.broadcasted_iota(jnp.int32, groups.shape, 0).astype(F32)

    def top(vals):
        best = functools.reduce(jnp.maximum, vals)
        idx = jnp.full_like(best, float(EXPERTS_PER_GROUP))
        for e in reversed(range(EXPERTS_PER_GROUP)):
            idx = jnp.where(vals[e] == best, float(e), idx)
        return best, idx

    t1, i1 = top(blocks)
    t2, i2 = top([jnp.where(i1 == float(e), neg, blocks[e]) for e in range(EXPERTS_PER_GROUP)])
    gmax = jnp.max(groups, axis=0, keepdims=True)
    g_sel = jnp.min(jnp.where(groups == gmax, sub, float(SUBLANES)), axis=0, keepdims=True)
    p_sel = 1.0 / jnp.sum(jnp.exp(groups - gmax), axis=0, keepdims=True)
    chosen = sub == g_sel

    def pick(x):
        return jnp.sum(jnp.where(chosen, x, 0.0), axis=0, keepdims=True)

    t1, i1, t2, i2 = pick(t1), pick(i1), pick(t2), pick(i2)
    e2 = jnp.exp(t2 - t1)
    w1 = p_sel / (1.0 + e2)
    w2 = p_sel * e2 / (1.0 + e2)
    rec = jnp.where(sub == 0.0, g_sel * EXPERTS_PER_GROUP + i1, 0.0)
    rec = jnp.where(sub == 1.0, g_sel * EXPERTS_PER_GROUP + i2, rec)
    rec = jnp.where(sub == 2.0, w1, rec)
    return jnp.where(sub == 3.0, w2, rec)


def _mixer_kernel(of_ref, ob_ref, sg_ref, gu_ref, vn_ref, *refs, tm, n_a_tiles, two_inputs):
    xa_ref, xb_ref = (refs[0], refs[1]) if two_inputs else (refs[0], None)
    (hgain_ref, ws_ref, bs_ref, wo_ref, g1_ref, n2_ref, sh2_ref, sc2_ref, wrh_ref, wrl_ref, br_ref,
     xo_ref, h2_ref, rt_ref, rtt_ref, cat_ref) = refs[2 if two_inputs else 1:]
    o = of_ref[...] + ob_ref[...]
    for h in range(HEADS):
        sl = slice(h * HEAD_DIM, (h + 1) * HEAD_DIM)
        hg = _rms(o[:, sl]) * hgain_ref[:, sl] * sg_ref[:, sl].astype(F32)
        cat_ref[:, sl] = hg.astype(BF16)
    for cc in range(tm // SGU_CHUNK):
        rows = slice(cc * SGU_CHUNK, (cc + 1) * SGU_CHUNK)
        for h in range(HEADS):
            sl = slice(h * HEAD_DIM, (h + 1) * HEAD_DIM)
            mixed = _dot(ws_ref[h], vn_ref[rows, sl]) + bs_ref[h]
            cat_ref[rows, HW + h * HEAD_DIM:HW + (h + 1) * HEAD_DIM] = (
                gu_ref[rows, sl].astype(F32) * mixed).astype(BF16)
    xn = _stream_tile(xa_ref, xb_ref, n_a_tiles) + g1_ref[...] * _dot(cat_ref[...], wo_ref[...])
    xo_ref[...] = xn
    h2 = _rms(xn) * n2_ref[...]
    h2 = h2 * (1.0 + sc2_ref[...]) + sh2_ref[...]
    _store_token_tiles(h2_ref, h2)
    hi, lo = _split2(h2)
    lt = _dot_nt(wrh_ref[...], hi) + (_dot_nt(wrh_ref[...], lo) + _dot_nt(wrl_ref[...], hi)) + br_ref[...]
    rec = _route_t(lt)
    rtt_ref[...] = rec
    pad = jnp.zeros((ROUTE_LANES - SUBLANES, tm), F32)
    rt_ref[...] = jnp.concatenate([rec, pad], axis=0).T


def _mixer(o_f, o_b, sg, gu, vn, xa, xb, t_a, t_b, mod3, tokens_per_mod_row, hgain, w_s, b_s, w_out, n2,
           wr_hi, wr_lo, br):
    d = xa.shape[1]
    tm = MIX_ROWS
    n_a, n_b = t_a // tm, t_b // tm
    t = t_a + t_b
    nd = d // LANES
    row = functools.partial(_mod_row, d, tokens_per_mod_row // tm, mod3.shape[0])
    tok = pl.BlockSpec((tm, HW), lambda i: (i, 0))
    wide = pl.BlockSpec((tm, d), lambda i: (i, 0))
    x_in, x_spec = _stream_inputs(xa, xb, n_a, tm, d)
    return pl.pallas_call(
        functools.partial(_mixer_kernel, tm=tm, n_a_tiles=n_a, two_inputs=xb is not None),
        out_shape=[jax.ShapeDtypeStruct((t, d), F32), jax.ShapeDtypeStruct((t * nd, LANES), F32),
                   jax.ShapeDtypeStruct((t, ROUTE_LANES), F32), jax.ShapeDtypeStruct((SUBLANES, t), F32)],
        grid=(n_a + n_b,),
        in_specs=[tok, tok, tok, tok, tok] + x_spec + [_full(hgain), _full(w_s), _full(b_s), _full(w_out),
                  row(2), _full(n2), row(3), row(4), _full(wr_hi), _full(wr_lo), _full(br)],
        out_specs=[wide, pl.BlockSpec((tm * nd, LANES), lambda i: (i, 0)),
                   pl.BlockSpec((tm, ROUTE_LANES), lambda i: (i, 0)), pl.BlockSpec((SUBLANES, tm), lambda i: (0, i))],
        scratch_shapes=[pltpu.VMEM((tm, 2 * HW), BF16)],
        compiler_params=_cparams("arbitrary"),
        name="mixer",
    )(o_f, o_b, sg, gu, vn, *x_in, hgain, w_s, b_s, w_out, mod3, n2, mod3, mod3, wr_hi, wr_lo, br)


MOE_ROWS = 256
MOE_LAG = 3
RING = 3
DMA_GROUPS = 8


def _moe_kernel(last_ref, sexp_ref, tab_hbm, h_hbm, wg_ref, wu_ref, wd_ref, y_hbm,
                idx_ref, xbuf, ybuf, zbuf, xb_ref, hm_ref, wgb, wub, wdb, sem_idx, sem_g, sem_s, sem_z):
    i = pl.program_id(0)
    last = last_ref[0]
    bm = MOE_ROWS
    de = wgb.shape[1]
    d = wgb.shape[0]
    nd = d // LANES

    def idx_copy(step):
        s = step % 2
        return pltpu.make_async_copy(tab_hbm.at[step], idx_ref.at[s], sem_idx.at[s])

    def gathered(slot):
        return pltpu.make_async_copy(h_hbm.at[pl.ds(0, bm * nd)], xbuf.at[slot], sem_g.at[slot])

    def scattered(slot):
        return pltpu.make_async_copy(ybuf.at[slot], y_hbm.at[pl.ds(0, bm * nd)], sem_s.at[slot])

    @pl.when(i <= last)
    def _():
        islot = i % 2
        gslot = i % RING
        cslot = (i + 1) % RING
        sslot = i % RING

        @pl.when(i == 0)
        def _():
            xbuf[...] = jnp.zeros_like(xbuf)
            ybuf[...] = jnp.zeros_like(ybuf)
            zbuf[...] = jnp.zeros_like(zbuf)
            idx_copy(0).start()

        idx_copy(i).wait()

        @pl.when(i < last)
        def _():
            idx_copy(i + 1).start()

        @pl.when(i >= 2)
        def _():
            gathered(cslot).wait()
            scattered(cslot).wait()

        @pl.when((i == 0) | (sexp_ref[i] != sexp_ref[jnp.maximum(i - 1, 0)]))
        def _():
            wgb[...] = wg_ref[...].astype(BF16)
            wub[...] = wu_ref[...].astype(BF16)
            wdb[...] = wd_ref[...].astype(BF16)

        per = bm // (DMA_GROUPS // 2)

        def scatter_group(g):
            for r in range(g * per, (g + 1) * per):
                dst = pl.multiple_of(idx_ref[islot, 1, r], nd)
                pltpu.make_async_copy(ybuf.at[sslot, pl.ds(r * nd, nd)], y_hbm.at[pl.ds(dst, nd)],
                                      sem_s.at[sslot]).start(priority=r % 2)

        def gather_group(g):
            for r in range(g * per, (g + 1) * per):
                src = pl.multiple_of(idx_ref[islot, 0, r], nd)
                pltpu.make_async_copy(h_hbm.at[pl.ds(src, nd)], xbuf.at[gslot, pl.ds(r * nd, nd)],
                                      sem_g.at[gslot]).start(priority=r % 2)

        dma_groups = [functools.partial(scatter_group, g) for g in range(DMA_GROUPS // 2)]
        dma_groups += [functools.partial(gather_group, g) for g in range(DMA_GROUPS // 2)]

        def issue_some():
            if dma_groups:
                dma_groups.pop(0)()

        xsrc = xbuf.at[cslot]
        for j in range(nd):
            xb_ref[:, j * LANES:(j + 1) * LANES] = xsrc[pl.ds(j, bm, stride=nd), :].astype(BF16)
        nh = 2
        for j in range(nh):
            cs = slice(j * de // nh, (j + 1) * de // nh)
            issue_some()
            gate = _dot(xb_ref[...], wgb[:, cs])
            issue_some()
            hm_ref[:, cs] = (_silu(gate) * _dot(xb_ref[...], wub[:, cs])).astype(BF16)
        ydst = ybuf.at[cslot]
        n_down = min(4, nd)
        for j in range(n_down):
            issue_some()
            _store_token_tiles_cols(ydst, _dot(hm_ref[...], wdb[:, j * d // n_down:(j + 1) * d // n_down]),
                                    j * nd // n_down, nd)
        while dma_groups:
            issue_some()

        @pl.when(i == last)
        def _():
            gathered(gslot).wait()
            gathered((i + 2) % RING).wait()
            scattered(sslot).wait()
            scattered((i + 2) % RING).wait()

    @pl.when(i > last)
    def _():
        fill = pltpu.make_async_copy(zbuf, y_hbm.at[pl.ds((i - MOE_LAG) * (bm * nd), bm * nd)], sem_z.at[0])
        fill.start()
        fill.wait()


def _moe(h2, tab, last, sexp, n_out_rows, w_gate, w_up, w_down, layer):
    d, de = w_gate.shape[-2:]
    nd = d // LANES
    n_steps = tab.shape[0]
    grid_spec = pltpu.PrefetchScalarGridSpec(
        num_scalar_prefetch=2,
        grid=(n_steps,),
        in_specs=[
            pl.BlockSpec(memory_space=pl.ANY),
            pl.BlockSpec(memory_space=pl.ANY),
            pl.BlockSpec((None, None, d, de), lambda i, la, se: (layer, se[i], 0, 0)),
            pl.BlockSpec((None, None, d, de), lambda i, la, se: (layer, se[i], 0, 0)),
            pl.BlockSpec((None, None, de, d), lambda i, la, se: (layer, se[i], 0, 0)),
        ],
        out_specs=pl.BlockSpec(memory_space=pl.ANY),
        scratch_shapes=[
            pltpu.SMEM((2, 2, MOE_ROWS), jnp.int32),
            pltpu.VMEM((RING, MOE_ROWS * nd, LANES), F32),
            pltpu.VMEM((RING, MOE_ROWS * nd, LANES), F32),
            pltpu.VMEM((MOE_ROWS * nd, LANES), F32),
            pltpu.VMEM((MOE_ROWS, d), BF16),
            pltpu.VMEM((MOE_ROWS, de), BF16),
            pltpu.VMEM((d, de), BF16),
            pltpu.VMEM((d, de), BF16),
            pltpu.VMEM((de, d), BF16),
            pltpu.SemaphoreType.DMA((2,)),
            pltpu.SemaphoreType.DMA((RING,)),
            pltpu.SemaphoreType.DMA((RING,)),
            pltpu.SemaphoreType.DMA((1,)),
        ],
    )
    return pl.pallas_call(
        _moe_kernel,
        out_shape=jax.ShapeDtypeStruct((n_out_rows * nd, LANES), F32),
        grid_spec=grid_spec,
        compiler_params=_cparams("arbitrary"),
        name="moe",
    )(last, sexp, tab, h2, w_gate, w_up, w_down)


def _dispatch_tables(expert_flat, ttot, nd):
    bm = MOE_ROWS
    n_slots = expert_flat.shape[0]
    n_blocks = -(-n_slots // bm) + N_EXPERTS
    n_steps = n_blocks + MOE_LAG
    _, order = lax.sort_key_val(expert_flat, lax.iota(jnp.int32, n_slots))
    counts = jnp.sum(expert_flat[:, None] == jnp.arange(N_EXPERTS, dtype=jnp.int32)[None, :], axis=0,
                     dtype=jnp.int32)
    padded = (counts + bm - 1) // bm * bm
    pad_end = jnp.cumsum(padded)
    pad_start = pad_end - padded
    start = jnp.cumsum(counts) - counts
    blk_row0 = jnp.arange(n_blocks, dtype=jnp.int32) * bm
    bexp = jnp.minimum(jnp.sum(pad_end[None, :] <= blk_row0[:, None], axis=1), N_EXPERTS - 1).astype(jnp.int32)
    lane = jnp.arange(bm, dtype=jnp.int32)[None, :]
    off = (blk_row0 - pad_start[bexp])[:, None] + lane
    valid = off < counts[bexp][:, None]
    slot = order[jnp.clip(start[bexp][:, None] + off, 0, n_slots - 1)]
    pad_rank = blk_row0[:, None] + lane - (start[bexp] + counts[bexp])[:, None]
    gsrc = jnp.where(valid, slot % ttot, 0)
    sdst = jnp.where(valid, slot, n_slots + pad_rank)
    spare = n_blocks * bm + jnp.arange(MOE_LAG * bm, dtype=jnp.int32).reshape(MOE_LAG, bm)
    gtab = jnp.concatenate([gsrc, jnp.zeros((MOE_LAG, bm), jnp.int32)], axis=0)
    stab = jnp.concatenate([spare, sdst], axis=0)
    tab = (jnp.stack([gtab, stab], axis=1) * nd).astype(jnp.int32)
    n_used = jnp.sum(padded) // bm
    last = (n_used + MOE_LAG - 1).astype(jnp.int32).reshape(1)
    sexp = bexp[jnp.clip(jnp.arange(n_steps) - (MOE_LAG - 1), 0, n_blocks - 1)]
    return tab, last, sexp, n_steps * bm


def _combine_kernel(x_ref, y0_ref, y1_ref, rt_ref, g2_ref, fn_ref, o_ref, *, final):
    w = rt_ref[...]
    tm, d = x_ref.shape
    nd = d // LANES
    f = w[:, 2:3] * _load_token_tiles(y0_ref, tm, nd) + w[:, 3:4] * _load_token_tiles(y1_ref, tm, nd)
    xn = x_ref[...] + g2_ref[...] * f
    if final:
        xn = _rms(xn) * fn_ref[...]
    o_ref[...] = xn


def _combine(x2d, y2, t, ttot, route, mod3, tokens_per_mod_row, final_norm, final):
    d = x2d.shape[1]
    tm = MIX_ROWS
    b1 = ttot // tm
    nd = d // LANES
    return pl.pallas_call(
        functools.partial(_combine_kernel, final=final),
        out_shape=jax.ShapeDtypeStruct((t, d), F32),
        grid=(t // tm,),
        in_specs=[pl.BlockSpec((tm, d), lambda i: (i, 0)),
                  pl.BlockSpec((tm * nd, LANES), lambda i: (i, 0)),
                  pl.BlockSpec((tm * nd, LANES), lambda i: (i + b1, 0)),
                  pl.BlockSpec((tm, ROUTE_LANES), lambda i: (i, 0)),
                  _mod_row(d, tokens_per_mod_row // tm, mod3.shape[0], 5),
                  _full(final_norm)],
        out_specs=pl.BlockSpec((tm, d), lambda i: (i, 0)),
        compiler_params=_cparams("arbitrary"),
        name="combine",
    )(x2d, y2, y2, route, mod3, final_norm)


def kernel(x, c, ctx, c_ctx, norm1, norm2, w_mod, b_mod, w_in, lb_logits, hgrn_norm, sgu_norm, sgu_w, sgu_b,
           w_out, w_group, b_group, w_router, b_router, w_gate, w_up, w_down, final_norm):
    b, l, d = x.shape
    lc = ctx.shape[1]
    depth = w_mod.shape[0]
    t_lat, t_ctx = b * l, b * lc
    nd = d // LANES

    lb_cum = jnp.cumsum(jax.nn.softmax(lb_logits.astype(F32), axis=0), axis=0)
    lower_bound = jnp.maximum(lb_cum - lb_cum[0:1], 0.0)

    cc = jnp.zeros((MOD_ROWS, d), F32).at[:b].set(c).at[b].set(c_ctx)
    mod = _modulation(cc, w_mod, b_mod)

    n_used_rows = SUBLANES * EXPERTS_PER_GROUP + N_GROUPS
    w_e = w_router.reshape(depth, d, N_GROUPS, EXPERTS_PER_GROUP).transpose(0, 1, 3, 2)
    w_e = jnp.pad(w_e, ((0, 0), (0, 0), (0, 0), (0, SUBLANES - N_GROUPS))).reshape(depth, d, -1)
    w_g = jnp.pad(w_group, ((0, 0), (0, 0), (0, ROUTE_LANES - n_used_rows)))
    w_route_t = jnp.concatenate([w_e, w_g], axis=-1).transpose(0, 2, 1)
    wr_hi = w_route_t.astype(BF16)
    wr_lo = (w_route_t - wr_hi.astype(F32)).astype(BF16)
    b_e = b_router.reshape(depth, N_GROUPS, EXPERTS_PER_GROUP).transpose(0, 2, 1)
    b_e = jnp.pad(b_e, ((0, 0), (0, 0), (0, SUBLANES - N_GROUPS)), constant_values=-jnp.inf).reshape(depth, -1)
    b_g = jnp.pad(b_group, ((0, 0), (0, ROUTE_LANES - n_used_rows)), constant_values=-jnp.inf)
    b_route = jnp.broadcast_to(jnp.concatenate([b_e, b_g], axis=-1)[:, :, None], (depth, ROUTE_LANES, MIX_ROWS))
    b_s = jnp.broadcast_to(sgu_b[..., None], sgu_b.shape + (HEAD_DIM,)).astype(F32)

    w_in_b, w_out_b, sgu_w_b = w_in.astype(BF16), w_out.astype(BF16), sgu_w.astype(BF16)

    xa, xb, t_a, t_b = x.reshape(t_lat, d), ctx.reshape(t_ctx, d), t_lat, t_ctx
    fn = final_norm.reshape(1, d)

    for layer in range(depth):
        last = layer == depth - 1
        mod3 = mod[layer, :b + 1].reshape(b + 1, 1, N_MOD * d)
        n1 = norm1[layer].reshape(1, d)
        n2 = norm2[layer].reshape(1, d)
        sgn = sgu_norm[layer].reshape(1, HW)
        hgain = hgrn_norm[layer].reshape(1, HW)

        q, i, lf_f, lf_b, sg, gu, vn = _inproj(xa, xb, t_a, t_b, mod3, l, n1, w_in_b[layer],
                                               lower_bound[layer], sgn, layer == 0)
        o_f, o_b = _hgrn(q, i, lf_f, lf_b, b, l, lc)

        if last:
            xb, t_a, t_b = None, t_lat, 0
        ttot = t_a + t_b
        xs, h2, route, route_t = _mixer(o_f, o_b, sg, gu, vn, xa, xb, t_a, t_b, mod3, l, hgain, sgu_w_b[layer],
                                        b_s[layer], w_out_b[layer], n2, wr_hi[layer], wr_lo[layer],
                                        b_route[layer])

        expert_flat = route_t[:TOP_K].astype(jnp.int32).reshape(-1)
        tab, last_step, sexp, n_out_rows = _dispatch_tables(expert_flat, ttot, nd)
        y2 = _moe(h2, tab, last_step, sexp, n_out_rows, w_gate, w_up, w_down, layer)
        xs = _combine(xs, y2, ttot, ttot, route, mod3, l, fn, final=last)

        xa, xb, t_a, t_b = xs, None, ttot, 0

    return xs.reshape(b, l, d)
```

```python
import functools

import jax
import jax.numpy as jnp
from jax import lax
from jax.experimental import pallas as pl
from jax.experimental.pallas import tpu as pltpu

F32 = jnp.float32
BF16 = jnp.bfloat16

EPS = 1e-6
HEADS = 4
HEAD_DIM = 128
HW = HEADS * HEAD_DIM
HGRN_CHUNK = 64
SGU_CHUNK = 128
N_GROUPS = 4
EXPERTS_PER_GROUP = 8
N_EXPERTS = N_GROUPS * EXPERTS_PER_GROUP
TOP_K = 2
N_MOD = 6
LANES = 128
SUBLANES = 8
ROUTE_LANES = LANES
MOD_ROWS = 16

PROJ_ROWS = 512
SCAN_ROWS = 256
MIX_ROWS = 256

VMEM_LIMIT = 48 * 1024 * 1024


def _cparams(*sem):
    return pltpu.CompilerParams(dimension_semantics=sem, vmem_limit_bytes=VMEM_LIMIT)


def _split2(a):
    hi = a.astype(BF16)
    lo = (a - hi.astype(F32)).astype(BF16)
    return hi, lo


def _dot(a, b):
    return jnp.dot(a, b, preferred_element_type=F32)


def _dot_nt(a, b):
    return lax.dot_general(a, b, (((1,), (1,)), ((), ())), preferred_element_type=F32)


def _dot_tn(a, b):
    return lax.dot_general(a, b, (((0,), (0,)), ((), ())), preferred_element_type=F32)


def _dot3(a, b):
    ah, al = _split2(a)
    bh, bl = _split2(b)
    return _dot(ah, bh) + (_dot(al, bh) + _dot(ah, bl))


def _silu(x):
    return x / (1.0 + jnp.exp(-x))


def _rms(x):
    return x * lax.rsqrt(jnp.mean(x * x, axis=-1, keepdims=True) + EPS)


def _full(a):
    return pl.BlockSpec(a.shape, lambda *_: (0,) * a.ndim)


def _stream_inputs(xa, xb, n_a, tm, d):
    if xb is None:
        return [xa], [pl.BlockSpec((tm, d), lambda i: (i, 0))]
    return [xa, xb], [pl.BlockSpec((tm, d), lambda i: (jnp.minimum(i, n_a - 1), 0)),
                      pl.BlockSpec((tm, d), lambda i: (jnp.maximum(i - n_a, 0), 0))]


def _mod_row(d, tiles_per_row, n_rows, j):
    return pl.BlockSpec((None, 1, d), lambda i: (jnp.minimum(i // tiles_per_row, n_rows - 1), 0, j))


def _store_token_tiles_cols(ref, x, j0, nd):
    rows, w = x.shape
    for j in range(w // LANES):
        ref[pl.ds(j0 + j, rows, stride=nd), :] = x[:, j * LANES:(j + 1) * LANES]


def _store_token_tiles(ref, x):
    _store_token_tiles_cols(ref, x, 0, x.shape[1] // LANES)


def _load_token_tiles(ref, rows, nd):
    return jnp.concatenate([ref[pl.ds(j, rows, stride=nd), :] for j in range(nd)], axis=-1)


def _mod_kernel(c_ref, w_ref, b_ref, o_ref):
    o_ref[...] = _dot3(_silu(c_ref[...]), w_ref[...]) + b_ref[...]


def _modulation(cc, w_mod, b_mod):
    depth, d, n = w_mod.shape
    tn = 1536
    return pl.pallas_call(
        _mod_kernel,
        out_shape=jax.ShapeDtypeStruct((depth, MOD_ROWS, n), F32),
        grid=(depth, n // tn),
        in_specs=[
            pl.BlockSpec((MOD_ROWS, d), lambda l, j: (0, 0)),
            pl.BlockSpec((None, d, tn), lambda l, j: (l, 0, j)),
            pl.BlockSpec((None, 1, tn), lambda l, j: (l, 0, j)),
        ],
        out_specs=pl.BlockSpec((None, MOD_ROWS, tn), lambda l, j: (l, 0, j)),
        compiler_params=_cparams("arbitrary", "arbitrary"),
        name="modulation",
    )(cc, w_mod, b_mod.reshape(depth, 1, n))


def _log_forget(z, lb, lb_is_zero):
    ls = jnp.minimum(z, 0.0) - jnp.log(1.0 + jnp.exp(-jnp.abs(z)))
    if lb_is_zero:
        return ls
    a = jnp.log(1.0 - lb) + ls
    b = jnp.log(lb)
    return jnp.maximum(a, b) + jnp.log(1.0 + jnp.exp(-jnp.abs(a - b)))


def _stream_tile(xa_ref, xb_ref, n_a_tiles):
    if xb_ref is None:
        return xa_ref[...]
    return jnp.where(pl.program_id(0) < n_a_tiles, xa_ref[...], xb_ref[...])


def _inproj_kernel(*refs, lb_is_zero, n_a_tiles, two_inputs):
    xa_ref, xb_ref = (refs[0], refs[1]) if two_inputs else (refs[0], None)
    (sh_ref, sc_ref, n1_ref, w_ref, lb_ref, sgn_ref,
     q_ref, i_ref, lff_ref, lfb_ref, sg_ref, gu_ref, vn_ref) = refs[2 if two_inputs else 1:]
    h = _rms(_stream_tile(xa_ref, xb_ref, n_a_tiles)) * n1_ref[...]
    hb = (h * (1.0 + sc_ref[...]) + sh_ref[...]).astype(BF16)

    def proj(j):
        return _dot(hb, w_ref[:, j * HW:(j + 1) * HW])

    q_ref[...] = proj(0).astype(BF16)
    lff_ref[...] = _log_forget(proj(1), lb_ref[0:1, :], lb_is_zero)
    lfb_ref[...] = _log_forget(proj(2), lb_ref[1:2, :], lb_is_zero)
    i_ref[...] = proj(3).astype(BF16)
    sg_ref[...] = _silu(proj(4)).astype(BF16)
    gu_ref[...] = jax.nn.gelu(proj(5)).astype(BF16)
    vn_ref[...] = (_rms(jax.nn.gelu(proj(6))) * sgn_ref[...]).astype(BF16)


def _inproj(xa, xb, t_a, t_b, mod3, tokens_per_mod_row, n1, w_in, lb, sgu_gain, lb_is_zero):
    d = xa.shape[1]
    tm = PROJ_ROWS
    n_a, n_b = t_a // tm, t_b // tm
    t = t_a + t_b
    tok = pl.BlockSpec((tm, HW), lambda i: (i, 0))
    row = functools.partial(_mod_row, d, tokens_per_mod_row // tm, mod3.shape[0])
    x_in, x_spec = _stream_inputs(xa, xb, n_a, tm, d)
    return pl.pallas_call(
        functools.partial(_inproj_kernel, lb_is_zero=lb_is_zero, n_a_tiles=n_a, two_inputs=xb is not None),
        out_shape=[jax.ShapeDtypeStruct((t, HW), dt) for dt in (BF16, BF16, F32, F32, BF16, BF16, BF16)],
        grid=(n_a + n_b,),
        in_specs=x_spec + [row(0), row(1), _full(n1), _full(w_in), _full(lb), _full(sgu_gain)],
        out_specs=[tok] * 7,
        compiler_params=_cparams("arbitrary"),
        name="inproj",
    )(*x_in, mod3, mod3, n1, w_in, lb, sgu_gain)


def _chunk_cumsum(x, reverse):
    c, w = x.shape
    g = c // SUBLANES
    x3 = x.reshape(g, SUBLANES, w)
    sub = lax.broadcasted_iota(jnp.int32, x3.shape, 1)
    for s in (1, 2, 4):
        if reverse:
            x3 = x3 + jnp.where(sub < SUBLANES - s, pltpu.roll(x3, SUBLANES - s, axis=1), 0.0)
        else:
            x3 = x3 + jnp.where(sub >= s, pltpu.roll(x3, s, axis=1), 0.0)
    edge = 0 if reverse else SUBLANES - 1
    tot = x3[:, edge:edge + 1, :]
    offs = [None] * g
    acc = jnp.zeros((1, w), F32)
    for gi in (reversed(range(g)) if reverse else range(g)):
        offs[gi] = acc
        acc = acc + tot[gi]
    x3 = x3 + jnp.stack(offs, axis=0)
    return x3.reshape(c, w)


def _scan_chunk(q_ref, i_ref, lf_ref, o_ref, st_ref, r0, reverse):
    c = HGRN_CHUNK
    rows = lax.broadcasted_iota(jnp.int32, (c, c), 0)
    cols = lax.broadcasted_iota(jnp.int32, (c, c), 1)
    incl = (cols >= rows) if reverse else (cols <= rows)
    ref_row = c // 2 if reverse else c // 2 - 1
    tot_row = 0 if reverse else c - 1
    lf = lf_ref[pl.ds(r0, c), :]
    cum = _chunk_cumsum(lf, reverse)
    ref = cum[ref_row:ref_row + 1, :]
    tot = cum[tot_row:tot_row + 1, :]
    k = 1.0 - jnp.exp(lf)
    qf = q_ref[pl.ds(r0, c), :].astype(F32)
    iv = i_ref[pl.ds(r0, c), :]
    q_in = (qf * jnp.exp(cum - ref)).astype(BF16)
    k_in = (k * jnp.exp(ref - cum)).astype(BF16)
    k_st = (k * jnp.exp(tot - cum)).astype(BF16)
    q_st = (qf * jnp.exp(cum)).astype(BF16)
    dec = jnp.exp(tot)
    for h in range(HEADS):
        sl = slice(h * HEAD_DIM, (h + 1) * HEAD_DIM)
        sc = _dot_nt(q_in[:, sl], k_in[:, sl])
        sc = jnp.where(incl, sc, 0.0).astype(BF16)
        st = st_ref[h]
        o_ref[pl.ds(r0, c), sl] = _dot(sc, iv[:, sl]) + _dot_nt(q_st[:, sl], st.astype(BF16))
        st_ref[h] = st * dec[:, sl] + _dot_tn(iv[:, sl], k_st[:, sl])


def _hgrn_kernel(qf_ref, if_ref, lff_ref, qb_ref, ib_ref, lfb_ref, of_ref, ob_ref, stf_ref, stb_ref, *, tt):
    @pl.when(pl.program_id(1) == 0)
    def _():
        stf_ref[...] = jnp.zeros_like(stf_ref)
        stb_ref[...] = jnp.zeros_like(stb_ref)

    nchunks = tt // HGRN_CHUNK
    for ci in range(nchunks):
        _scan_chunk(qf_ref, if_ref, lff_ref, of_ref, stf_ref, ci * HGRN_CHUNK, False)
        _scan_chunk(qb_ref, ib_ref, lfb_ref, ob_ref, stb_ref, (nchunks - 1 - ci) * HGRN_CHUNK, True)


def _hgrn(q, i, lf_f, lf_b, batch, seq, ctx_len):
    t = q.shape[0]
    tt = SCAN_ROWS
    nt, nc = seq // tt, ctx_len // tt
    ctx0 = batch * nt

    def fwd(b, s):
        return (jnp.where(s < nc, ctx0 + b * nc + s, b * nt + (s - nc)), 0)

    def bwd(b, s):
        return (jnp.where(s < nc, ctx0 + b * nc + (nc - 1 - s), b * nt + (nt - 1 - (s - nc))), 0)

    tf, tb = pl.BlockSpec((tt, HW), fwd), pl.BlockSpec((tt, HW), bwd)
    return pl.pallas_call(
        functools.partial(_hgrn_kernel, tt=tt),
        out_shape=[jax.ShapeDtypeStruct((t, HW), F32)] * 2,
        grid=(batch, nc + nt),
        in_specs=[tf, tf, tf, tb, tb, tb],
        out_specs=[tf, tb],
        scratch_shapes=[pltpu.VMEM((HEADS, HEAD_DIM, HEAD_DIM), F32)] * 2,
        compiler_params=_cparams("arbitrary", "arbitrary"),
        name="hgrn",
    )(q, i, lf_f, q, i, lf_b)


def _route(logits):
    lane = lax.broadcasted_iota(jnp.int32, logits.shape, 1).astype(F32)
    neg = -jnp.inf
    is_group = lane < N_GROUPS
    gl = jnp.where(is_group, logits, neg)
    gmax = jnp.max(gl, axis=-1, keepdims=True)
    g_sel = jnp.min(jnp.where(gl == gmax, lane, float(ROUTE_LANES)), axis=-1, keepdims=True)
    den = jnp.sum(jnp.where(is_group, jnp.exp(logits - gmax), 0.0), axis=-1, keepdims=True)
    p_sel = 1.0 / den
    first = N_GROUPS + EXPERTS_PER_GROUP * g_sel
    el = jnp.where((lane >= first) & (lane < first + EXPERTS_PER_GROUP), logits, neg)
    t1 = jnp.max(el, axis=-1, keepdims=True)
    i1 = jnp.min(jnp.where(el == t1, lane, float(ROUTE_LANES)), axis=-1, keepdims=True)
    el2 = jnp.where(lane == i1, neg, el)
    t2 = jnp.max(el2, axis=-1, keepdims=True)
    i2 = jnp.min(jnp.where(el2 == t2, lane, float(ROUTE_LANES)), axis=-1, keepdims=True)
    e2 = jnp.exp(t2 - t1)
    w1 = p_sel / (1.0 + e2)
    w2 = p_sel * e2 / (1.0 + e2)
    rec = jnp.where(lane == 0.0, i1 - N_GROUPS, 0.0)
    rec = jnp.where(lane == 1.0, i2 - N_GROUPS, rec)
    rec = jnp.where(lane == 2.0, w1, rec)
    return jnp.where(lane == 3.0, w2, rec)


def _mixer_kernel(of_ref, ob_ref, sg_ref, gu_ref, vn_ref, *refs, tm, n_a_tiles, two_inputs):
    xa_ref, xb_ref = (refs[0], refs[1]) if two_inputs else (refs[0], None)
    (hgain_ref, ws_ref, bs_ref, wo_ref, g1_ref, n2_ref, sh2_ref, sc2_ref, wrh_ref, wrl_ref, br_ref,
     xo_ref, h2_ref, rt_ref, cat_ref) = refs[2 if two_inputs else 1:]
    o = of_ref[...] + ob_ref[...]
    for h in range(HEADS):
        sl = slice(h * HEAD_DIM, (h + 1) * HEAD_DIM)
        hg = _rms(o[:, sl]) * hgain_ref[:, sl] * sg_ref[:, sl].astype(F32)
        cat_ref[:, sl] = hg.astype(BF16)
    for cc in range(tm // SGU_CHUNK):
        rows = slice(cc * SGU_CHUNK, (cc + 1) * SGU_CHUNK)
        for h in range(HEADS):
            sl = slice(h * HEAD_DIM, (h + 1) * HEAD_DIM)
            mixed = _dot(ws_ref[h], vn_ref[rows, sl]) + bs_ref[h]
            cat_ref[rows, HW + h * HEAD_DIM:HW + (h + 1) * HEAD_DIM] = (
                gu_ref[rows, sl].astype(F32) * mixed).astype(BF16)
    xn = _stream_tile(xa_ref, xb_ref, n_a_tiles) + g1_ref[...] * _dot(cat_ref[...], wo_ref[...])
    xo_ref[...] = xn
    h2 = _rms(xn) * n2_ref[...]
    h2 = h2 * (1.0 + sc2_ref[...]) + sh2_ref[...]
    _store_token_tiles(h2_ref, h2)
    hi, lo = _split2(h2)
    logits = _dot(hi, wrh_ref[...]) + (_dot(lo, wrh_ref[...]) + _dot(hi, wrl_ref[...])) + br_ref[...]
    rt_ref[...] = _route(logits)


def _mixer(o_f, o_b, sg, gu, vn, xa, xb, t_a, t_b, mod3, tokens_per_mod_row, hgain, w_s, b_s, w_out, n2,
           wr_hi, wr_lo, br):
    d = xa.shape[1]
    tm = MIX_ROWS
    n_a, n_b = t_a // tm, t_b // tm
    t = t_a + t_b
    nd = d // LANES
    row = functools.partial(_mod_row, d, tokens_per_mod_row // tm, mod3.shape[0])
    tok = pl.BlockSpec((tm, HW), lambda i: (i, 0))
    wide = pl.BlockSpec((tm, d), lambda i: (i, 0))
    x_in, x_spec = _stream_inputs(xa, xb, n_a, tm, d)
    return pl.pallas_call(
        functools.partial(_mixer_kernel, tm=tm, n_a_tiles=n_a, two_inputs=xb is not None),
        out_shape=[jax.ShapeDtypeStruct((t, d), F32), jax.ShapeDtypeStruct((t * nd, LANES), F32),
                   jax.ShapeDtypeStruct((t, ROUTE_LANES), F32)],
        grid=(n_a + n_b,),
        in_specs=[tok, tok, tok, tok, tok] + x_spec + [_full(hgain), _full(w_s), _full(b_s), _full(w_out),
                  row(2), _full(n2), row(3), row(4), _full(wr_hi), _full(wr_lo), _full(br)],
        out_specs=[wide, pl.BlockSpec((tm * nd, LANES), lambda i: (i, 0)),
                   pl.BlockSpec((tm, ROUTE_LANES), lambda i: (i, 0))],
        scratch_shapes=[pltpu.VMEM((tm, 2 * HW), BF16)],
        compiler_params=_cparams("arbitrary"),
        name="mixer",
    )(o_f, o_b, sg, gu, vn, *x_in, hgain, w_s, b_s, w_out, mod3, n2, mod3, mod3, wr_hi, wr_lo, br)


MOE_ROWS = 256
MOE_LAG = 3
RING = 3
DMA_GROUPS = 8


def _moe_kernel(last_ref, sexp_ref, tab_hbm, h_hbm, wg_ref, wu_ref, wd_ref, y_hbm,
                idx_ref, xbuf, ybuf, zbuf, xb_ref, hm_ref, wgb, wub, wdb, sem_idx, sem_g, sem_s, sem_z):
    i = pl.program_id(0)
    last = last_ref[0]
    bm = MOE_ROWS
    de = wgb.shape[1]
    d = wgb.shape[0]
    nd = d // LANES

    def idx_copy(step, slot):
        return pltpu.make_async_copy(tab_hbm.at[step], idx_ref.at[slot], sem_idx.at[slot])

    def gathered(slot):
        return pltpu.make_async_copy(h_hbm.at[pl.ds(0, bm * nd)], xbuf.at[slot], sem_g.at[slot])

    def scattered(slot):
        return pltpu.make_async_copy(ybuf.at[slot], y_hbm.at[pl.ds(0, bm * nd)], sem_s.at[slot])

    def step(k):
        gslot = k
        cslot = (k + 1) % RING
        sslot = k

        if k == 0:
            @pl.when(i == 0)
            def _():
                xbuf[...] = jnp.zeros_like(xbuf)
                ybuf[...] = jnp.zeros_like(ybuf)
                zbuf[...] = jnp.zeros_like(zbuf)
                idx_copy(0, 0).start()

        idx_copy(i, k).wait()

        @pl.when(i < last)
        def _():
            idx_copy(i + 1, (k + 1) % RING).start()

        @pl.when(i >= 2)
        def _():
            gathered(cslot).wait()
            scattered(cslot).wait()

        @pl.when((i == 0) | (sexp_ref[i] != sexp_ref[jnp.maximum(i - 1, 0)]))
        def _():
            wgb[...] = wg_ref[...].astype(BF16)
            wub[...] = wu_ref[...].astype(BF16)
            wdb[...] = wd_ref[...].astype(BF16)

        per = bm // (DMA_GROUPS // 2)

        def scatter_group(g):
            for r in range(g * per, (g + 1) * per):
                dst = pl.multiple_of(idx_ref[k, 1, r], nd)
                pltpu.make_async_copy(ybuf.at[sslot, pl.ds(r * nd, nd)], y_hbm.at[pl.ds(dst, nd)],
                                      sem_s.at[sslot]).start(priority=r % 2)

        def gather_group(g):
            for r in range(g * per, (g + 1) * per):
                src = pl.multiple_of(idx_ref[k, 0, r], nd)
                pltpu.make_async_copy(h_hbm.at[pl.ds(src, nd)], xbuf.at[gslot, pl.ds(r * nd, nd)],
                                      sem_g.at[gslot]).start(priority=r % 2)

        dma_groups = [functools.partial(scatter_group, g) for g in range(DMA_GROUPS // 2)]
        dma_groups += [functools.partial(gather_group, g) for g in range(DMA_GROUPS // 2)]

        def issue_some():
            if dma_groups:
                dma_groups.pop(0)()

        xsrc = xbuf.at[cslot]
        for j in range(nd):
            xb_ref[:, j * LANES:(j + 1) * LANES] = xsrc[pl.ds(j, bm, stride=nd), :].astype(BF16)
        nh = 2
        for j in range(nh):
            cs = slice(j * de // nh, (j + 1) * de // nh)
            issue_some()
            gate = _dot(xb_ref[...], wgb[:, cs])
            issue_some()
            hm_ref[:, cs] = (_silu(gate) * _dot(xb_ref[...], wub[:, cs])).astype(BF16)
        ydst = ybuf.at[cslot]
        n_down = min(4, nd)
        for j in range(n_down):
            issue_some()
            _store_token_tiles_cols(ydst, _dot(hm_ref[...], wdb[:, j * d // n_down:(j + 1) * d // n_down]),
                                    j * nd // n_down, nd)
        while dma_groups:
            issue_some()

        @pl.when(i == last)
        def _():
            gathered(gslot).wait()
            gathered((k + 2) % RING).wait()
            scattered(sslot).wait()
            scattered((k + 2) % RING).wait()

    for k in range(RING):
        pl.when((i <= last) & (i % RING == k))(functools.partial(step, k))

    @pl.when(i > last)
    def _():
        fill = pltpu.make_async_copy(zbuf, y_hbm.at[pl.ds((i - MOE_LAG) * (bm * nd), bm * nd)], sem_z.at[0])
        fill.start()
        fill.wait()


def _moe(h2, tab, last, sexp, n_out_rows, w_gate, w_up, w_down, layer):
    d, de = w_gate.shape[-2:]
    nd = d // LANES
    n_steps = tab.shape[0]
    grid_spec = pltpu.PrefetchScalarGridSpec(
        num_scalar_prefetch=2,
        grid=(n_steps,),
        in_specs=[
            pl.BlockSpec(memory_space=pl.ANY),
            pl.BlockSpec(memory_space=pl.ANY),
            pl.BlockSpec((None, None, d, de), lambda i, la, se: (layer, se[i], 0, 0)),
            pl.BlockSpec((None, None, d, de), lambda i, la, se: (layer, se[i], 0, 0)),
            pl.BlockSpec((None, None, de, d), lambda i, la, se: (layer, se[i], 0, 0)),
        ],
        out_specs=pl.BlockSpec(memory_space=pl.ANY),
        scratch_shapes=[
            pltpu.SMEM((RING, 2, MOE_ROWS), jnp.int32),
            pltpu.VMEM((RING, MOE_ROWS * nd, LANES), F32),
            pltpu.VMEM((RING, MOE_ROWS * nd, LANES), F32),
            pltpu.VMEM((MOE_ROWS * nd, LANES), F32),
            pltpu.VMEM((MOE_ROWS, d), BF16),
            pltpu.VMEM((MOE_ROWS, de), BF16),
            pltpu.VMEM((d, de), BF16),
            pltpu.VMEM((d, de), BF16),
            pltpu.VMEM((de, d), BF16),
            pltpu.SemaphoreType.DMA((RING,)),
            pltpu.SemaphoreType.DMA((RING,)),
            pltpu.SemaphoreType.DMA((RING,)),
            pltpu.SemaphoreType.DMA((1,)),
        ],
    )
    return pl.pallas_call(
        _moe_kernel,
        out_shape=jax.ShapeDtypeStruct((n_out_rows * nd, LANES), F32),
        grid_spec=grid_spec,
        compiler_params=_cparams("arbitrary"),
        name="moe",
    )(last, sexp, tab, h2, w_gate, w_up, w_down)


def _dispatch_tables(expert_flat, ttot, nd):
    bm = MOE_ROWS
    n_slots = expert_flat.shape[0]
    n_blocks = -(-n_slots // bm) + N_EXPERTS
    n_steps = n_blocks + MOE_LAG
    _, order = lax.sort_key_val(expert_flat, lax.iota(jnp.int32, n_slots))
    counts = jnp.sum(expert_flat[:, None] == jnp.arange(N_EXPERTS, dtype=jnp.int32)[None, :], axis=0,
                     dtype=jnp.int32)
    padded = (counts + bm - 1) // bm * bm
    pad_end = jnp.cumsum(padded)
    pad_start = pad_end - padded
    start = jnp.cumsum(counts) - counts
    blk_row0 = jnp.arange(n_blocks, dtype=jnp.int32) * bm
    bexp = jnp.minimum(jnp.sum(pad_end[None, :] <= blk_row0[:, None], axis=1), N_EXPERTS - 1).astype(jnp.int32)
    lane = jnp.arange(bm, dtype=jnp.int32)[None, :]
    off = (blk_row0 - pad_start[bexp])[:, None] + lane
    valid = off < counts[bexp][:, None]
    slot = order[jnp.clip(start[bexp][:, None] + off, 0, n_slots - 1)]
    pad_rank = blk_row0[:, None] + lane - (start[bexp] + counts[bexp])[:, None]
    gsrc = jnp.where(valid, slot % ttot, 0)
    sdst = jnp.where(valid, slot, n_slots + pad_rank)
    spare = n_blocks * bm + jnp.arange(MOE_LAG * bm, dtype=jnp.int32).reshape(MOE_LAG, bm)
    gtab = jnp.concatenate([gsrc, jnp.zeros((MOE_LAG, bm), jnp.int32)], axis=0)
    stab = jnp.concatenate([spare, sdst], axis=0)
    tab = (jnp.stack([gtab, stab], axis=1) * nd).astype(jnp.int32)
    n_used = jnp.sum(padded) // bm
    last = (n_used + MOE_LAG - 1).astype(jnp.int32).reshape(1)
    sexp = bexp[jnp.clip(jnp.arange(n_steps) - (MOE_LAG - 1), 0, n_blocks - 1)]
    return tab, last, sexp, n_steps * bm


def _combine_kernel(x_ref, y0_ref, y1_ref, rt_ref, g2_ref, fn_ref, o_ref, *, final):
    w = rt_ref[...]
    tm, d = x_ref.shape
    nd = d // LANES
    f = w[:, 2:3] * _load_token_tiles(y0_ref, tm, nd) + w[:, 3:4] * _load_token_tiles(y1_ref, tm, nd)
    xn = x_ref[...] + g2_ref[...] * f
    if final:
        xn = _rms(xn) * fn_ref[...]
    o_ref[...] = xn


def _combine(x2d, y2, t, ttot, route, mod3, tokens_per_mod_row, final_norm, final):
    d = x2d.shape[1]
    tm = MIX_ROWS
    b1 = ttot // tm
    nd = d // LANES
    return pl.pallas_call(
        functools.partial(_combine_kernel, final=final),
        out_shape=jax.ShapeDtypeStruct((t, d), F32),
        grid=(t // tm,),
        in_specs=[pl.BlockSpec((tm, d), lambda i: (i, 0)),
                  pl.BlockSpec((tm * nd, LANES), lambda i: (i, 0)),
                  pl.BlockSpec((tm * nd, LANES), lambda i: (i + b1, 0)),
                  pl.BlockSpec((tm, ROUTE_LANES), lambda i: (i, 0)),
                  _mod_row(d, tokens_per_mod_row // tm, mod3.shape[0], 5),
                  _full(final_norm)],
        out_specs=pl.BlockSpec((tm, d), lambda i: (i, 0)),
        compiler_params=_cparams("arbitrary"),
        name="combine",
    )(x2d, y2, y2, route, mod3, final_norm)


def kernel(x, c, ctx, c_ctx, norm1, norm2, w_mod, b_mod, w_in, lb_logits, hgrn_norm, sgu_norm, sgu_w, sgu_b,
           w_out, w_group, b_group, w_router, b_router, w_gate, w_up, w_down, final_norm):
    b, l, d = x.shape
    lc = ctx.shape[1]
    depth = w_mod.shape[0]
    t_lat, t_ctx = b * l, b * lc
    nd = d // LANES

    lb_cum = jnp.cumsum(jax.nn.softmax(lb_logits.astype(F32), axis=0), axis=0)
    lower_bound = jnp.maximum(lb_cum - lb_cum[0:1], 0.0)

    cc = jnp.zeros((MOD_ROWS, d), F32).at[:b].set(c).at[b].set(c_ctx)
    mod = _modulation(cc, w_mod, b_mod)

    w_route = jnp.concatenate([w_group, w_router], axis=-1)
    w_route = jnp.pad(w_route, ((0, 0), (0, 0), (0, ROUTE_LANES - w_route.shape[-1])))
    wr_hi = w_route.astype(BF16)
    wr_lo = (w_route - wr_hi.astype(F32)).astype(BF16)
    b_route = jnp.concatenate([b_group, b_router], axis=-1)
    b_route = jnp.pad(b_route, ((0, 0), (0, ROUTE_LANES - b_route.shape[-1])))[:, None, :]
    b_s = jnp.broadcast_to(sgu_b[..., None], sgu_b.shape + (HEAD_DIM,)).astype(F32)

    w_in_b, w_out_b, sgu_w_b = w_in.astype(BF16), w_out.astype(BF16), sgu_w.astype(BF16)

    xa, xb, t_a, t_b = x.reshape(t_lat, d), ctx.reshape(t_ctx, d), t_lat, t_ctx
    fn = final_norm.reshape(1, d)

    for layer in range(depth):
        last = layer == depth - 1
        mod3 = mod[layer, :b + 1].reshape(b + 1, 1, N_MOD * d)
        n1 = norm1[layer].reshape(1, d)
        n2 = norm2[layer].reshape(1, d)
        sgn = sgu_norm[layer].reshape(1, HW)
        hgain = hgrn_norm[layer].reshape(1, HW)

        q, i, lf_f, lf_b, sg, gu, vn = _inproj(xa, xb, t_a, t_b, mod3, l, n1, w_in_b[layer],
                                               lower_bound[layer], sgn, layer == 0)
        o_f, o_b = _hgrn(q, i, lf_f, lf_b, b, l, lc)

        if last:
            xb, t_a, t_b = None, t_lat, 0
        ttot = t_a + t_b
        xs, h2, route = _mixer(o_f, o_b, sg, gu, vn, xa, xb, t_a, t_b, mod3, l, hgain, sgu_w_b[layer],
                               b_s[layer], w_out_b[layer], n2, wr_hi[layer], wr_lo[layer], b_route[layer])

        expert_flat = route[:, :TOP_K].astype(jnp.int32).T.reshape(-1)
        tab, last_step, sexp, n_out_rows = _dispatch_tables(expert_flat, ttot, nd)
        y2 = _moe(h2, tab, last_step, sexp, n_out_rows, w_gate, w_up, w_down, layer)
        xs = _combine(xs, y2, ttot, ttot, route, mod3, l, fn, final=last)

        xa, xb, t_a, t_b = xs, None, ttot, 0

    return xs.reshape(b, l, d)
```

```python
import functools

import jax
import jax.numpy as jnp
from jax import lax
from jax.experimental import pallas as pl
from jax.experimental.pallas import tpu as pltpu

F32 = jnp.float32
BF16 = jnp.bfloat16

EPS = 1e-6
LOG2_E = 1.4426950408889634
HEADS = 4
HEAD_DIM = 128
HW = HEADS * HEAD_DIM
HGRN_CHUNK = 64
SGU_CHUNK = 128
N_GROUPS = 4
EXPERTS_PER_GROUP = 8
N_EXPERTS = N_GROUPS * EXPERTS_PER_GROUP
TOP_K = 2
N_MOD = 6
LANES = 128
SUBLANES = 8
ROUTE_LANES = LANES
MOD_ROWS = 16

PROJ_ROWS = 512
SCAN_ROWS = 256
MIX_ROWS = 256

VMEM_LIMIT = 48 * 1024 * 1024


def _cparams(*sem):
    return pltpu.CompilerParams(dimension_semantics=sem, vmem_limit_bytes=VMEM_LIMIT)


def _split2(a):
    hi = a.astype(BF16)
    lo = (a - hi.astype(F32)).astype(BF16)
    return hi, lo


def _dot(a, b):
    return jnp.dot(a, b, preferred_element_type=F32)


def _dot_nt(a, b):
    return lax.dot_general(a, b, (((1,), (1,)), ((), ())), preferred_element_type=F32)


def _dot_tn(a, b):
    return lax.dot_general(a, b, (((0,), (0,)), ((), ())), preferred_element_type=F32)


def _dot3(a, b):
    ah, al = _split2(a)
    bh, bl = _split2(b)
    return _dot(ah, bh) + (_dot(al, bh) + _dot(ah, bl))


def _silu(x):
    return x / (1.0 + jnp.exp(-x))


def _rms(x):
    return x * lax.rsqrt(jnp.mean(x * x, axis=-1, keepdims=True) + EPS)


def _full(a):
    return pl.BlockSpec(a.shape, lambda *_: (0,) * a.ndim)


def _stream_inputs(xa, xb, n_a, tm, d):
    if xb is None:
        return [xa], [pl.BlockSpec((tm, d), lambda i: (i, 0))]
    return [xa, xb], [pl.BlockSpec((tm, d), lambda i: (jnp.minimum(i, n_a - 1), 0)),
                      pl.BlockSpec((tm, d), lambda i: (jnp.maximum(i - n_a, 0), 0))]


def _mod_row(d, tiles_per_row, n_rows, j):
    return pl.BlockSpec((None, 1, d), lambda i: (jnp.minimum(i // tiles_per_row, n_rows - 1), 0, j))


def _store_token_tiles_cols(ref, x, j0, nd):
    rows, w = x.shape
    for j in range(w // LANES):
        ref[pl.ds(j0 + j, rows, stride=nd), :] = x[:, j * LANES:(j + 1) * LANES]


def _store_token_tiles(ref, x):
    _store_token_tiles_cols(ref, x, 0, x.shape[1] // LANES)


def _load_token_tiles(ref, rows, nd):
    return jnp.concatenate([ref[pl.ds(j, rows, stride=nd), :] for j in range(nd)], axis=-1)


def _mod_kernel(c_ref, w_ref, b_ref, o_ref):
    o_ref[...] = _dot3(_silu(c_ref[...]), w_ref[...]) + b_ref[...]


def _modulation(cc, w_mod, b_mod):
    depth, d, n = w_mod.shape
    tn = 1536
    return pl.pallas_call(
        _mod_kernel,
        out_shape=jax.ShapeDtypeStruct((depth, MOD_ROWS, n), F32),
        grid=(depth, n // tn),
        in_specs=[
            pl.BlockSpec((MOD_ROWS, d), lambda l, j: (0, 0)),
            pl.BlockSpec((None, d, tn), lambda l, j: (l, 0, j)),
            pl.BlockSpec((None, 1, tn), lambda l, j: (l, 0, j)),
        ],
        out_specs=pl.BlockSpec((None, MOD_ROWS, tn), lambda l, j: (l, 0, j)),
        compiler_params=_cparams("arbitrary", "arbitrary"),
        name="modulation",
    )(cc, w_mod, b_mod.reshape(depth, 1, n))


def _log_forget(z, lb, lb_is_zero):
    ls = jnp.minimum(z, 0.0) - jnp.log(1.0 + jnp.exp(-jnp.abs(z)))
    if lb_is_zero:
        return ls
    return jnp.maximum(jnp.log(lb + (1.0 - lb) * jnp.exp(ls)), ls)


def _stream_tile(xa_ref, xb_ref, n_a_tiles):
    if xb_ref is None:
        return xa_ref[...]
    return jnp.where(pl.program_id(0) < n_a_tiles, xa_ref[...], xb_ref[...])


def _inproj_kernel(*refs, lb_is_zero, n_a_tiles, two_inputs):
    xa_ref, xb_ref = (refs[0], refs[1]) if two_inputs else (refs[0], None)
    (sh_ref, sc_ref, n1_ref, w_ref, lb_ref, sgn_ref,
     q_ref, i_ref, lff_ref, lfb_ref, sg_ref, gu_ref, vn_ref) = refs[2 if two_inputs else 1:]
    h = _rms(_stream_tile(xa_ref, xb_ref, n_a_tiles)) * n1_ref[...]
    hb = (h * (1.0 + sc_ref[...]) + sh_ref[...]).astype(BF16)

    def proj(j):
        return _dot(hb, w_ref[:, j * HW:(j + 1) * HW])

    q_ref[...] = proj(0).astype(BF16)
    lff_ref[...] = _log_forget(proj(1), lb_ref[0:1, :], lb_is_zero)
    lfb_ref[...] = _log_forget(proj(2), lb_ref[1:2, :], lb_is_zero)
    i_ref[...] = proj(3).astype(BF16)
    sg_ref[...] = _silu(proj(4)).astype(BF16)
    gu_ref[...] = jax.nn.gelu(proj(5)).astype(BF16)
    vn_ref[...] = (_rms(jax.nn.gelu(proj(6))) * sgn_ref[...]).astype(BF16)


def _inproj(xa, xb, t_a, t_b, mod3, tokens_per_mod_row, n1, w_in, lb, sgu_gain, lb_is_zero):
    d = xa.shape[1]
    tm = PROJ_ROWS
    n_a, n_b = t_a // tm, t_b // tm
    t = t_a + t_b
    tok = pl.BlockSpec((tm, HW), lambda i: (i, 0))
    row = functools.partial(_mod_row, d, tokens_per_mod_row // tm, mod3.shape[0])
    x_in, x_spec = _stream_inputs(xa, xb, n_a, tm, d)
    return pl.pallas_call(
        functools.partial(_inproj_kernel, lb_is_zero=lb_is_zero, n_a_tiles=n_a, two_inputs=xb is not None),
        out_shape=[jax.ShapeDtypeStruct((t, HW), dt) for dt in (BF16, BF16, F32, F32, BF16, BF16, BF16)],
        grid=(n_a + n_b,),
        in_specs=x_spec + [row(0), row(1), _full(n1), _full(w_in), _full(lb), _full(sgu_gain)],
        out_specs=[tok] * 7,
        compiler_params=_cparams("arbitrary"),
        name="inproj",
    )(*x_in, mod3, mod3, n1, w_in, lb, sgu_gain)


def _chunk_cumsum(x, reverse):
    c, w = x.shape
    g = c // SUBLANES
    x3 = x.reshape(g, SUBLANES, w)
    sub = lax.broadcasted_iota(jnp.int32, x3.shape, 1)
    for s in (1, 2, 4):
        if reverse:
            x3 = x3 + jnp.where(sub < SUBLANES - s, pltpu.roll(x3, SUBLANES - s, axis=1), 0.0)
        else:
            x3 = x3 + jnp.where(sub >= s, pltpu.roll(x3, s, axis=1), 0.0)
    edge = 0 if reverse else SUBLANES - 1
    tot = x3[:, edge:edge + 1, :]
    offs = [None] * g
    acc = jnp.zeros((1, w), F32)
    for gi in (reversed(range(g)) if reverse else range(g)):
        offs[gi] = acc
        acc = acc + tot[gi]
    x3 = x3 + jnp.stack(offs, axis=0)
    return x3.reshape(c, w)


def _scan_chunk(q_ref, i_ref, lf_ref, o_ref, st_ref, r0, reverse):
    c = HGRN_CHUNK
    rows = lax.broadcasted_iota(jnp.int32, (c, c), 0)
    cols = lax.broadcasted_iota(jnp.int32, (c, c), 1)
    incl = (cols >= rows) if reverse else (cols <= rows)
    ref_row = c // 2 if reverse else c // 2 - 1
    tot_row = 0 if reverse else c - 1
    lf = lf_ref[pl.ds(r0, c), :] * LOG2_E
    cum = _chunk_cumsum(lf, reverse)
    ref = cum[ref_row:ref_row + 1, :]
    tot = cum[tot_row:tot_row + 1, :]
    k = 1.0 - jnp.exp2(lf)
    qf = q_ref[pl.ds(r0, c), :].astype(F32)
    iv = i_ref[pl.ds(r0, c), :]
    q_in = (qf * jnp.exp2(cum - ref)).astype(BF16)
    k_in = (k * jnp.exp2(ref - cum)).astype(BF16)
    k_st = (k * jnp.exp2(tot - cum)).astype(BF16)
    q_st = (qf * jnp.exp2(cum)).astype(BF16)
    dec = jnp.exp2(tot)
    for h in range(HEADS):
        sl = slice(h * HEAD_DIM, (h + 1) * HEAD_DIM)
        sc = _dot_nt(q_in[:, sl], k_in[:, sl])
        sc = jnp.where(incl, sc, 0.0).astype(BF16)
        st = st_ref[h]
        o_ref[pl.ds(r0, c), sl] = _dot(sc, iv[:, sl]) + _dot_nt(q_st[:, sl], st.astype(BF16))
        st_ref[h] = st * dec[:, sl] + _dot_tn(iv[:, sl], k_st[:, sl])


def _hgrn_kernel(qf_ref, if_ref, lff_ref, qb_ref, ib_ref, lfb_ref, of_ref, ob_ref, stf_ref, stb_ref, *, tt):
    @pl.when(pl.program_id(1) == 0)
    def _():
        stf_ref[...] = jnp.zeros_like(stf_ref)
        stb_ref[...] = jnp.zeros_like(stb_ref)

    nchunks = tt // HGRN_CHUNK
    for ci in range(nchunks):
        _scan_chunk(qf_ref, if_ref, lff_ref, of_ref, stf_ref, ci * HGRN_CHUNK, False)
        _scan_chunk(qb_ref, ib_ref, lfb_ref, ob_ref, stb_ref, (nchunks - 1 - ci) * HGRN_CHUNK, True)


def _hgrn(q, i, lf_f, lf_b, batch, seq, ctx_len):
    t = q.shape[0]
    tt = SCAN_ROWS
    nt, nc = seq // tt, ctx_len // tt
    ctx0 = batch * nt

    def fwd(b, s):
        return (jnp.where(s < nc, ctx0 + b * nc + s, b * nt + (s - nc)), 0)

    def bwd(b, s):
        return (jnp.where(s < nc, ctx0 + b * nc + (nc - 1 - s), b * nt + (nt - 1 - (s - nc))), 0)

    tf, tb = pl.BlockSpec((tt, HW), fwd), pl.BlockSpec((tt, HW), bwd)
    return pl.pallas_call(
        functools.partial(_hgrn_kernel, tt=tt),
        out_shape=[jax.ShapeDtypeStruct((t, HW), F32)] * 2,
        grid=(batch, nc + nt),
        in_specs=[tf, tf, tf, tb, tb, tb],
        out_specs=[tf, tb],
        scratch_shapes=[pltpu.VMEM((HEADS, HEAD_DIM, HEAD_DIM), F32)] * 2,
        compiler_params=_cparams("arbitrary", "arbitrary"),
        name="hgrn",
    )(q, i, lf_f, q, i, lf_b)


def _route(logits):
    lane = lax.broadcasted_iota(jnp.int32, logits.shape, 1).astype(F32)
    neg = -jnp.inf
    is_group = lane < N_GROUPS
    gl = jnp.where(is_group, logits, neg)
    gmax = jnp.max(gl, axis=-1, keepdims=True)
    g_sel = jnp.min(jnp.where(gl == gmax, lane, float(ROUTE_LANES)), axis=-1, keepdims=True)
    den = jnp.sum(jnp.where(is_group, jnp.exp(logits - gmax), 0.0), axis=-1, keepdims=True)
    p_sel = 1.0 / den
    first = N_GROUPS + EXPERTS_PER_GROUP * g_sel
    el = jnp.where((lane >= first) & (lane < first + EXPERTS_PER_GROUP), logits, neg)
    t1 = jnp.max(el, axis=-1, keepdims=True)
    i1 = jnp.min(jnp.where(el == t1, lane, float(ROUTE_LANES)), axis=-1, keepdims=True)
    el2 = jnp.where(lane == i1, neg, el)
    t2 = jnp.max(el2, axis=-1, keepdims=True)
    i2 = jnp.min(jnp.where(el2 == t2, lane, float(ROUTE_LANES)), axis=-1, keepdims=True)
    e2 = jnp.exp(t2 - t1)
    w1 = p_sel / (1.0 + e2)
    w2 = p_sel * e2 / (1.0 + e2)
    rec = jnp.where(lane == 0.0, i1 - N_GROUPS, 0.0)
    rec = jnp.where(lane == 1.0, i2 - N_GROUPS, rec)
    rec = jnp.where(lane == 2.0, w1, rec)
    return jnp.where(lane == 3.0, w2, rec)


def _mixer_kernel(of_ref, ob_ref, sg_ref, gu_ref, vn_ref, *refs, tm, n_a_tiles, two_inputs):
    xa_ref, xb_ref = (refs[0], refs[1]) if two_inputs else (refs[0], None)
    (hgain_ref, ws_ref, bs_ref, wo_ref, g1_ref, n2_ref, sh2_ref, sc2_ref, wrh_ref, wrl_ref, br_ref,
     xo_ref, h2_ref, rt_ref, rtt_ref, cat_ref) = refs[2 if two_inputs else 1:]
    o = of_ref[...] + ob_ref[...]
    for h in range(HEADS):
        sl = slice(h * HEAD_DIM, (h + 1) * HEAD_DIM)
        hg = _rms(o[:, sl]) * hgain_ref[:, sl] * sg_ref[:, sl].astype(F32)
        cat_ref[:, sl] = hg.astype(BF16)
    for cc in range(tm // SGU_CHUNK):
        rows = slice(cc * SGU_CHUNK, (cc + 1) * SGU_CHUNK)
        for h in range(HEADS):
            sl = slice(h * HEAD_DIM, (h + 1) * HEAD_DIM)
            mixed = _dot(ws_ref[h], vn_ref[rows, sl]) + bs_ref[h]
            cat_ref[rows, HW + h * HEAD_DIM:HW + (h + 1) * HEAD_DIM] = (
                gu_ref[rows, sl].astype(F32) * mixed).astype(BF16)
    xn = _stream_tile(xa_ref, xb_ref, n_a_tiles) + g1_ref[...] * _dot(cat_ref[...], wo_ref[...])
    xo_ref[...] = xn
    h2 = _rms(xn) * n2_ref[...]
    h2 = h2 * (1.0 + sc2_ref[...]) + sh2_ref[...]
    _store_token_tiles(h2_ref, h2)
    hi, lo = _split2(h2)
    logits = _dot(hi, wrh_ref[...]) + (_dot(lo, wrh_ref[...]) + _dot(hi, wrl_ref[...])) + br_ref[...]
    rec = _route(logits)
    rt_ref[...] = rec
    rtt_ref[...] = rec.T[:SUBLANES, :]


def _mixer(o_f, o_b, sg, gu, vn, xa, xb, t_a, t_b, mod3, tokens_per_mod_row, hgain, w_s, b_s, w_out, n2,
           wr_hi, wr_lo, br):
    d = xa.shape[1]
    tm = MIX_ROWS
    n_a, n_b = t_a // tm, t_b // tm
    t = t_a + t_b
    nd = d // LANES
    row = functools.partial(_mod_row, d, tokens_per_mod_row // tm, mod3.shape[0])
    tok = pl.BlockSpec((tm, HW), lambda i: (i, 0))
    wide = pl.BlockSpec((tm, d), lambda i: (i, 0))
    x_in, x_spec = _stream_inputs(xa, xb, n_a, tm, d)
    return pl.pallas_call(
        functools.partial(_mixer_kernel, tm=tm, n_a_tiles=n_a, two_inputs=xb is not None),
        out_shape=[jax.ShapeDtypeStruct((t, d), F32), jax.ShapeDtypeStruct((t * nd, LANES), F32),
                   jax.ShapeDtypeStruct((t, ROUTE_LANES), F32), jax.ShapeDtypeStruct((SUBLANES, t), F32)],
        grid=(n_a + n_b,),
        in_specs=[tok, tok, tok, tok, tok] + x_spec + [_full(hgain), _full(w_s), _full(b_s), _full(w_out),
                  row(2), _full(n2), row(3), row(4), _full(wr_hi), _full(wr_lo), _full(br)],
        out_specs=[wide, pl.BlockSpec((tm * nd, LANES), lambda i: (i, 0)),
                   pl.BlockSpec((tm, ROUTE_LANES), lambda i: (i, 0)), pl.BlockSpec((SUBLANES, tm), lambda i: (0, i))],
        scratch_shapes=[pltpu.VMEM((tm, 2 * HW), BF16)],
        compiler_params=_cparams("arbitrary"),
        name="mixer",
    )(o_f, o_b, sg, gu, vn, *x_in, hgain, w_s, b_s, w_out, mod3, n2, mod3, mod3, wr_hi, wr_lo, br)


MOE_ROWS = 256
MOE_LAG = 3
RING = 3
DMA_GROUPS = 8


def _moe_kernel(last_ref, sexp_ref, tab_hbm, h_hbm, wg_ref, wu_ref, wd_ref, y_hbm,
                idx_ref, xbuf, ybuf, zbuf, xb_ref, hm_ref, wgb, wub, wdb, sem_idx, sem_g, sem_s, sem_z):
    i = pl.program_id(0)
    last = last_ref[0]
    bm = MOE_ROWS
    de = wgb.shape[1]
    d = wgb.shape[0]
    nd = d // LANES

    def idx_copy(step, slot):
        return pltpu.make_async_copy(tab_hbm.at[step], idx_ref.at[slot], sem_idx.at[slot])

    def gathered(slot):
        return pltpu.make_async_copy(h_hbm.at[pl.ds(0, bm * nd)], xbuf.at[slot], sem_g.at[slot])

    def scattered(slot):
        return pltpu.make_async_copy(ybuf.at[slot], y_hbm.at[pl.ds(0, bm * nd)], sem_s.at[slot])

    def step(k):
        gslot = k
        cslot = (k + 1) % RING
        sslot = k

        if k == 0:
            @pl.when(i == 0)
            def _():
                xbuf[...] = jnp.zeros_like(xbuf)
                ybuf[...] = jnp.zeros_like(ybuf)
                zbuf[...] = jnp.zeros_like(zbuf)
                idx_copy(0, 0).start()

        idx_copy(i, k).wait()

        @pl.when(i < last)
        def _():
            idx_copy(i + 1, (k + 1) % RING).start()

        @pl.when(i >= 2)
        def _():
            gathered(cslot).wait()
            scattered(cslot).wait()

        @pl.when((i == 0) | (sexp_ref[i] != sexp_ref[jnp.maximum(i - 1, 0)]))
        def _():
            wgb[...] = wg_ref[...].astype(BF16)
            wub[...] = wu_ref[...].astype(BF16)
            wdb[...] = wd_ref[...].astype(BF16)

        per = bm // (DMA_GROUPS // 2)

        def scatter_group(g):
            for r in range(g * per, (g + 1) * per):
                dst = pl.multiple_of(idx_ref[k, 1, r], nd)
                pltpu.make_async_copy(ybuf.at[sslot, pl.ds(r * nd, nd)], y_hbm.at[pl.ds(dst, nd)],
                                      sem_s.at[sslot]).start(priority=r % 2)

        def gather_group(g):
            for r in range(g * per, (g + 1) * per):
                src = pl.multiple_of(idx_ref[k, 0, r], nd)
                pltpu.make_async_copy(h_hbm.at[pl.ds(src, nd)], xbuf.at[gslot, pl.ds(r * nd, nd)],
                                      sem_g.at[gslot]).start(priority=r % 2)

        dma_groups = [functools.partial(scatter_group, g) for g in range(DMA_GROUPS // 2)]
        dma_groups += [functools.partial(gather_group, g) for g in range(DMA_GROUPS // 2)]

        def issue_some():
            if dma_groups:
                dma_groups.pop(0)()

        xsrc = xbuf.at[cslot]
        for j in range(nd):
            xb_ref[:, j * LANES:(j + 1) * LANES] = xsrc[pl.ds(j, bm, stride=nd), :].astype(BF16)
        nh = 2
        for j in range(nh):
            cs = slice(j * de // nh, (j + 1) * de // nh)
            issue_some()
            gate = _dot(xb_ref[...], wgb[:, cs])
            issue_some()
            hm_ref[:, cs] = (_silu(gate) * _dot(xb_ref[...], wub[:, cs])).astype(BF16)
        ydst = ybuf.at[cslot]
        n_down = min(4, nd)
        for j in range(n_down):
            issue_some()
            _store_token_tiles_cols(ydst, _dot(hm_ref[...], wdb[:, j * d // n_down:(j + 1) * d // n_down]),
                                    j * nd // n_down, nd)
        while dma_groups:
            issue_some()

        @pl.when(i == last)
        def _():
            gathered(gslot).wait()
            gathered((k + 2) % RING).wait()
            scattered(sslot).wait()
            scattered((k + 2) % RING).wait()

    for k in range(RING):
        pl.when((i <= last) & (i % RING == k))(functools.partial(step, k))

    @pl.when(i > last)
    def _():
        fill = pltpu.make_async_copy(zbuf, y_hbm.at[pl.ds((i - MOE_LAG) * (bm * nd), bm * nd)], sem_z.at[0])
        fill.start()
        fill.wait()


def _moe(h2, tab, last, sexp, n_out_rows, w_gate, w_up, w_down, layer):
    d, de = w_gate.shape[-2:]
    nd = d // LANES
    n_steps = tab.shape[0]
    grid_spec = pltpu.PrefetchScalarGridSpec(
        num_scalar_prefetch=2,
        grid=(n_steps,),
        in_specs=[
            pl.BlockSpec(memory_space=pl.ANY),
            pl.BlockSpec(memory_space=pl.ANY),
            pl.BlockSpec((None, None, d, de), lambda i, la, se: (layer, se[i], 0, 0)),
            pl.BlockSpec((None, None, d, de), lambda i, la, se: (layer, se[i], 0, 0)),
            pl.BlockSpec((None, None, de, d), lambda i, la, se: (layer, se[i], 0, 0)),
        ],
        out_specs=pl.BlockSpec(memory_space=pl.ANY),
        scratch_shapes=[
            pltpu.SMEM((RING, 2, MOE_ROWS), jnp.int32),
            pltpu.VMEM((RING, MOE_ROWS * nd, LANES), F32),
            pltpu.VMEM((RING, MOE_ROWS * nd, LANES), F32),
            pltpu.VMEM((MOE_ROWS * nd, LANES), F32),
            pltpu.VMEM((MOE_ROWS, d), BF16),
            pltpu.VMEM((MOE_ROWS, de), BF16),
            pltpu.VMEM((d, de), BF16),
            pltpu.VMEM((d, de), BF16),
            pltpu.VMEM((de, d), BF16),
            pltpu.SemaphoreType.DMA((RING,)),
            pltpu.SemaphoreType.DMA((RING,)),
            pltpu.SemaphoreType.DMA((RING,)),
            pltpu.SemaphoreType.DMA((1,)),
        ],
    )
    return pl.pallas_call(
        _moe_kernel,
        out_shape=jax.ShapeDtypeStruct((n_out_rows * nd, LANES), F32),
        grid_spec=grid_spec,
        compiler_params=_cparams("arbitrary"),
        name="moe",
    )(last, sexp, tab, h2, w_gate, w_up, w_down)


def _dispatch_tables(expert_flat, ttot, nd):
    bm = MOE_ROWS
    n_slots = expert_flat.shape[0]
    n_blocks = -(-n_slots // bm) + N_EXPERTS
    n_steps = n_blocks + MOE_LAG
    n_main = 1 << (n_slots.bit_length() - 1)
    pieces = []
    for lo, n in ((0, n_main), (n_main, n_slots - n_main)):
        if n:
            e = expert_flat[lo:lo + n]
            _, order = lax.sort_key_val(e, lax.iota(jnp.int32, n))
            cnt = jnp.sum(e[:, None] == jnp.arange(N_EXPERTS, dtype=jnp.int32)[None, :], axis=0, dtype=jnp.int32)
            pieces.append((order + lo, cnt))
    counts = sum(cnt for _, cnt in pieces)
    padded = (counts + bm - 1) // bm * bm
    pad_end = jnp.cumsum(padded)
    pad_start = pad_end - padded
    start = jnp.cumsum(counts) - counts
    blk_row0 = jnp.arange(n_blocks, dtype=jnp.int32) * bm
    bexp = jnp.minimum(jnp.sum(pad_end[None, :] <= blk_row0[:, None], axis=1), N_EXPERTS - 1).astype(jnp.int32)
    lane = jnp.arange(bm, dtype=jnp.int32)[None, :]
    off = (blk_row0 - pad_start[bexp])[:, None] + lane
    valid = off < counts[bexp][:, None]
    slot, rem = jnp.zeros_like(off), off
    for order, cnt in pieces:
        first = (jnp.cumsum(cnt) - cnt)[bexp][:, None]
        here = (rem >= 0) & (rem < cnt[bexp][:, None])
        slot = jnp.where(here, order[jnp.clip(first + rem, 0, order.shape[0] - 1)], slot)
        rem = rem - cnt[bexp][:, None]
    pad_rank = blk_row0[:, None] + lane - (start[bexp] + counts[bexp])[:, None]
    gsrc = jnp.where(valid, slot % ttot, 0)
    sdst = jnp.where(valid, slot, n_slots + pad_rank)
    spare = n_blocks * bm + jnp.arange(MOE_LAG * bm, dtype=jnp.int32).reshape(MOE_LAG, bm)
    gtab = jnp.concatenate([gsrc, jnp.zeros((MOE_LAG, bm), jnp.int32)], axis=0)
    stab = jnp.concatenate([spare, sdst], axis=0)
    tab = (jnp.stack([gtab, stab], axis=1) * nd).astype(jnp.int32)
    n_used = jnp.sum(padded) // bm
    last = (n_used + MOE_LAG - 1).astype(jnp.int32).reshape(1)
    sexp = bexp[jnp.clip(jnp.arange(n_steps) - (MOE_LAG - 1), 0, n_blocks - 1)]
    return tab, last, sexp, n_steps * bm


def _combine_kernel(x_ref, y0_ref, y1_ref, rt_ref, g2_ref, fn_ref, o_ref, *, final):
    w = rt_ref[...]
    tm, d = x_ref.shape
    nd = d // LANES
    f = w[:, 2:3] * _load_token_tiles(y0_ref, tm, nd) + w[:, 3:4] * _load_token_tiles(y1_ref, tm, nd)
    xn = x_ref[...] + g2_ref[...] * f
    if final:
        xn = _rms(xn) * fn_ref[...]
    o_ref[...] = xn


def _combine(x2d, y2, t, ttot, route, mod3, tokens_per_mod_row, final_norm, final):
    d = x2d.shape[1]
    tm = MIX_ROWS
    b1 = ttot // tm
    nd = d // LANES
    return pl.pallas_call(
        functools.partial(_combine_kernel, final=final),
        out_shape=jax.ShapeDtypeStruct((t, d), F32),
        grid=(t // tm,),
        in_specs=[pl.BlockSpec((tm, d), lambda i: (i, 0)),
                  pl.BlockSpec((tm * nd, LANES), lambda i: (i, 0)),
                  pl.BlockSpec((tm * nd, LANES), lambda i: (i + b1, 0)),
                  pl.BlockSpec((tm, ROUTE_LANES), lambda i: (i, 0)),
                  _mod_row(d, tokens_per_mod_row // tm, mod3.shape[0], 5),
                  _full(final_norm)],
        out_specs=pl.BlockSpec((tm, d), lambda i: (i, 0)),
        compiler_params=_cparams("arbitrary"),
        name="combine",
    )(x2d, y2, y2, route, mod3, final_norm)


def kernel(x, c, ctx, c_ctx, norm1, norm2, w_mod, b_mod, w_in, lb_logits, hgrn_norm, sgu_norm, sgu_w, sgu_b,
           w_out, w_group, b_group, w_router, b_router, w_gate, w_up, w_down, final_norm):
    b, l, d = x.shape
    lc = ctx.shape[1]
    depth = w_mod.shape[0]
    t_lat, t_ctx = b * l, b * lc
    nd = d // LANES

    lb_cum = jnp.cumsum(jax.nn.softmax(lb_logits.astype(F32), axis=0), axis=0)
    lower_bound = jnp.maximum(lb_cum - lb_cum[0:1], 0.0)

    cc = jnp.zeros((MOD_ROWS, d), F32).at[:b].set(c).at[b].set(c_ctx)
    mod = _modulation(cc, w_mod, b_mod)

    w_route = jnp.concatenate([w_group, w_router], axis=-1)
    w_route = jnp.pad(w_route, ((0, 0), (0, 0), (0, ROUTE_LANES - w_route.shape[-1])))
    wr_hi = w_route.astype(BF16)
    wr_lo = (w_route - wr_hi.astype(F32)).astype(BF16)
    b_route = jnp.concatenate([b_group, b_router], axis=-1)
    b_route = jnp.pad(b_route, ((0, 0), (0, ROUTE_LANES - b_route.shape[-1])))[:, None, :]
    b_s = jnp.broadcast_to(sgu_b[..., None], sgu_b.shape + (HEAD_DIM,)).astype(F32)

    w_in_b, w_out_b, sgu_w_b = w_in.astype(BF16), w_out.astype(BF16), sgu_w.astype(BF16)

    xa, xb, t_a, t_b = x.reshape(t_lat, d), ctx.reshape(t_ctx, d), t_lat, t_ctx
    fn = final_norm.reshape(1, d)

    for layer in range(depth):
        last = layer == depth - 1
        mod3 = mod[layer, :b + 1].reshape(b + 1, 1, N_MOD * d)
        n1 = norm1[layer].reshape(1, d)
        n2 = norm2[layer].reshape(1, d)
        sgn = sgu_norm[layer].reshape(1, HW)
        hgain = hgrn_norm[layer].reshape(1, HW)

        q, i, lf_f, lf_b, sg, gu, vn = _inproj(xa, xb, t_a, t_b, mod3, l, n1, w_in_b[layer],
                                               lower_bound[layer], sgn, layer == 0)
        o_f, o_b = _hgrn(q, i, lf_f, lf_b, b, l, lc)

        if last:
            xb, t_a, t_b = None, t_lat, 0
        ttot = t_a + t_b
        xs, h2, route, route_t = _mixer(o_f, o_b, sg, gu, vn, xa, xb, t_a, t_b, mod3, l, hgain, sgu_w_b[layer],
                                        b_s[layer], w_out_b[layer], n2, wr_hi[layer], wr_lo[layer],
                                        b_route[layer])

        expert_flat = route_t[:TOP_K].astype(jnp.int32).reshape(-1)
        tab, last_step, sexp, n_out_rows = _dispatch_tables(expert_flat, ttot, nd)
        y2 = _moe(h2, tab, last_step, sexp, n_out_rows, w_gate, w_up, w_down, layer)
        xs = _combine(xs, y2, ttot, ttot, route, mod3, l, fn, final=last)

        xa, xb, t_a, t_b = xs, None, ttot, 0

    return xs.reshape(b, l, d)
```

```python
import functools

import jax
import jax.numpy as jnp
from jax import lax
from jax.experimental import pallas as pl
from jax.experimental.pallas import tpu as pltpu

F32 = jnp.float32
BF16 = jnp.bfloat16

EPS = 1e-6
LOG2_E = 1.4426950408889634
HEADS = 4
HEAD_DIM = 128
HW = HEADS * HEAD_DIM
HGRN_CHUNK = 64
SGU_CHUNK = 128
N_GROUPS = 4
EXPERTS_PER_GROUP = 8
N_EXPERTS = N_GROUPS * EXPERTS_PER_GROUP
TOP_K = 2
N_MOD = 6
LANES = 128
SUBLANES = 8
ROUTE_LANES = LANES
MOD_ROWS = 16

PROJ_ROWS = 512
SCAN_ROWS = 256
MIX_ROWS = 256

VMEM_LIMIT = 48 * 1024 * 1024


def _cparams(*sem):
    return pltpu.CompilerParams(dimension_semantics=sem, vmem_limit_bytes=VMEM_LIMIT)


def _split2(a):
    hi = a.astype(BF16)
    lo = (a - hi.astype(F32)).astype(BF16)
    return hi, lo


def _dot(a, b):
    return jnp.dot(a, b, preferred_element_type=F32)


def _dot_nt(a, b):
    return lax.dot_general(a, b, (((1,), (1,)), ((), ())), preferred_element_type=F32)


def _dot_tn(a, b):
    return lax.dot_general(a, b, (((0,), (0,)), ((), ())), preferred_element_type=F32)


def _dot3(a, b):
    ah, al = _split2(a)
    bh, bl = _split2(b)
    return _dot(ah, bh) + (_dot(al, bh) + _dot(ah, bl))


def _silu(x):
    return x / (1.0 + jnp.exp(-x))


def _rms(x):
    return x * lax.rsqrt(jnp.mean(x * x, axis=-1, keepdims=True) + EPS)


def _full(a):
    return pl.BlockSpec(a.shape, lambda *_: (0,) * a.ndim)


def _stream_inputs(xa, xb, n_a, tm, d):
    if xb is None:
        return [xa], [pl.BlockSpec((tm, d), lambda i: (i, 0))]
    return [xa, xb], [pl.BlockSpec((tm, d), lambda i: (jnp.minimum(i, n_a - 1), 0)),
                      pl.BlockSpec((tm, d), lambda i: (jnp.maximum(i - n_a, 0), 0))]


def _mod_row(d, tiles_per_row, n_rows, j):
    return pl.BlockSpec((None, 1, d), lambda i: (jnp.minimum(i // tiles_per_row, n_rows - 1), 0, j))


def _store_token_tiles_cols(ref, x, j0, nd):
    rows, w = x.shape
    for j in range(w // LANES):
        ref[pl.ds(j0 + j, rows, stride=nd), :] = x[:, j * LANES:(j + 1) * LANES]


def _store_token_tiles(ref, x):
    _store_token_tiles_cols(ref, x, 0, x.shape[1] // LANES)


def _load_token_tiles(ref, rows, nd):
    return jnp.concatenate([ref[pl.ds(j, rows, stride=nd), :] for j in range(nd)], axis=-1)


def _mod_kernel(c_ref, w_ref, b_ref, o_ref):
    o_ref[...] = _dot3(_silu(c_ref[...]), w_ref[...]) + b_ref[...]


def _modulation(cc, w_mod, b_mod):
    depth, d, n = w_mod.shape
    tn = 1536
    return pl.pallas_call(
        _mod_kernel,
        out_shape=jax.ShapeDtypeStruct((depth, MOD_ROWS, n), F32),
        grid=(depth, n // tn),
        in_specs=[
            pl.BlockSpec((MOD_ROWS, d), lambda l, j: (0, 0)),
            pl.BlockSpec((None, d, tn), lambda l, j: (l, 0, j)),
            pl.BlockSpec((None, 1, tn), lambda l, j: (l, 0, j)),
        ],
        out_specs=pl.BlockSpec((None, MOD_ROWS, tn), lambda l, j: (l, 0, j)),
        compiler_params=_cparams("arbitrary", "arbitrary"),
        name="modulation",
    )(cc, w_mod, b_mod.reshape(depth, 1, n))


def _log_forget(z, lb, lb_is_zero):
    ls = jnp.minimum(z, 0.0) - jnp.log(1.0 + jnp.exp(-jnp.abs(z)))
    if lb_is_zero:
        return ls
    return jnp.maximum(jnp.log(lb + (1.0 - lb) * jnp.exp(ls)), ls)


def _stream_tile(xa_ref, xb_ref, n_a_tiles):
    if xb_ref is None:
        return xa_ref[...]
    return jnp.where(pl.program_id(0) < n_a_tiles, xa_ref[...], xb_ref[...])


def _inproj_kernel(*refs, lb_is_zero, n_a_tiles, two_inputs):
    xa_ref, xb_ref = (refs[0], refs[1]) if two_inputs else (refs[0], None)
    (sh_ref, sc_ref, n1_ref, w_ref, lb_ref, sgn_ref,
     q_ref, i_ref, lff_ref, lfb_ref, sg_ref, gu_ref, vn_ref) = refs[2 if two_inputs else 1:]
    h = _rms(_stream_tile(xa_ref, xb_ref, n_a_tiles)) * n1_ref[...]
    hb = (h * (1.0 + sc_ref[...]) + sh_ref[...]).astype(BF16)

    def proj(j):
        return _dot(hb, w_ref[:, j * HW:(j + 1) * HW])

    q_ref[...] = proj(0).astype(BF16)
    lff_ref[...] = _log_forget(proj(1), lb_ref[0:1, :], lb_is_zero)
    lfb_ref[...] = _log_forget(proj(2), lb_ref[1:2, :], lb_is_zero)
    i_ref[...] = proj(3).astype(BF16)
    sg_ref[...] = _silu(proj(4)).astype(BF16)
    gu_ref[...] = jax.nn.gelu(proj(5)).astype(BF16)
    vn_ref[...] = (_rms(jax.nn.gelu(proj(6))) * sgn_ref[...]).astype(BF16)


def _inproj(xa, xb, t_a, t_b, mod3, tokens_per_mod_row, n1, w_in, lb, sgu_gain, lb_is_zero):
    d = xa.shape[1]
    tm = PROJ_ROWS
    n_a, n_b = t_a // tm, t_b // tm
    t = t_a + t_b
    tok = pl.BlockSpec((tm, HW), lambda i: (i, 0))
    row = functools.partial(_mod_row, d, tokens_per_mod_row // tm, mod3.shape[0])
    x_in, x_spec = _stream_inputs(xa, xb, n_a, tm, d)
    return pl.pallas_call(
        functools.partial(_inproj_kernel, lb_is_zero=lb_is_zero, n_a_tiles=n_a, two_inputs=xb is not None),
        out_shape=[jax.ShapeDtypeStruct((t, HW), dt) for dt in (BF16, BF16, F32, F32, BF16, BF16, BF16)],
        grid=(n_a + n_b,),
        in_specs=x_spec + [row(0), row(1), _full(n1), _full(w_in), _full(lb), _full(sgu_gain)],
        out_specs=[tok] * 7,
        compiler_params=_cparams("arbitrary"),
        name="inproj",
    )(*x_in, mod3, mod3, n1, w_in, lb, sgu_gain)


def _chunk_cumsum(x, reverse):
    c, w = x.shape
    g = c // SUBLANES
    x3 = x.reshape(g, SUBLANES, w)
    sub = lax.broadcasted_iota(jnp.int32, x3.shape, 1)
    for s in (1, 2, 4):
        if reverse:
            x3 = x3 + jnp.where(sub < SUBLANES - s, pltpu.roll(x3, SUBLANES - s, axis=1), 0.0)
        else:
            x3 = x3 + jnp.where(sub >= s, pltpu.roll(x3, s, axis=1), 0.0)
    edge = 0 if reverse else SUBLANES - 1
    tot = x3[:, edge:edge + 1, :]
    offs = [None] * g
    acc = jnp.zeros((1, w), F32)
    for gi in (reversed(range(g)) if reverse else range(g)):
        offs[gi] = acc
        acc = acc + tot[gi]
    x3 = x3 + jnp.stack(offs, axis=0)
    return x3.reshape(c, w)


def _scan_chunk(q_ref, i_ref, lf_ref, o_ref, st_ref, r0, reverse):
    c = HGRN_CHUNK
    rows = lax.broadcasted_iota(jnp.int32, (c, c), 0)
    cols = lax.broadcasted_iota(jnp.int32, (c, c), 1)
    incl = (cols >= rows) if reverse else (cols <= rows)
    ref_row = c // 2 if reverse else c // 2 - 1
    tot_row = 0 if reverse else c - 1
    lf = lf_ref[pl.ds(r0, c), :] * LOG2_E
    cum = _chunk_cumsum(lf, reverse)
    ref = cum[ref_row:ref_row + 1, :]
    tot = cum[tot_row:tot_row + 1, :]
    k = 1.0 - jnp.exp2(lf)
    qf = q_ref[pl.ds(r0, c), :].astype(F32)
    iv = i_ref[pl.ds(r0, c), :]
    q_in = (qf * jnp.exp2(cum - ref)).astype(BF16)
    k_in = (k * jnp.exp2(ref - cum)).astype(BF16)
    k_st = (k * jnp.exp2(tot - cum)).astype(BF16)
    q_st = (qf * jnp.exp2(cum)).astype(BF16)
    dec = jnp.exp2(tot)
    for h in range(HEADS):
        sl = slice(h * HEAD_DIM, (h + 1) * HEAD_DIM)
        sc = _dot_nt(q_in[:, sl], k_in[:, sl])
        sc = jnp.where(incl, sc, 0.0).astype(BF16)
        st = st_ref[h]
        o_ref[pl.ds(r0, c), sl] = _dot(sc, iv[:, sl]) + _dot_nt(q_st[:, sl], st.astype(BF16))
        st_ref[h] = st * dec[:, sl] + _dot_tn(iv[:, sl], k_st[:, sl])


def _hgrn_kernel(qf_ref, if_ref, lff_ref, qb_ref, ib_ref, lfb_ref, of_ref, ob_ref, stf_ref, stb_ref, *, tt):
    @pl.when(pl.program_id(1) == 0)
    def _():
        stf_ref[...] = jnp.zeros_like(stf_ref)
        stb_ref[...] = jnp.zeros_like(stb_ref)

    nchunks = tt // HGRN_CHUNK
    for ci in range(nchunks):
        _scan_chunk(qf_ref, if_ref, lff_ref, of_ref, stf_ref, ci * HGRN_CHUNK, False)
        _scan_chunk(qb_ref, ib_ref, lfb_ref, ob_ref, stb_ref, (nchunks - 1 - ci) * HGRN_CHUNK, True)


def _hgrn(q, i, lf_f, lf_b, batch, seq, ctx_len):
    t = q.shape[0]
    tt = SCAN_ROWS
    nt, nc = seq // tt, ctx_len // tt
    ctx0 = batch * nt

    def fwd(b, s):
        return (jnp.where(s < nc, ctx0 + b * nc + s, b * nt + (s - nc)), 0)

    def bwd(b, s):
        return (jnp.where(s < nc, ctx0 + b * nc + (nc - 1 - s), b * nt + (nt - 1 - (s - nc))), 0)

    tf, tb = pl.BlockSpec((tt, HW), fwd), pl.BlockSpec((tt, HW), bwd)
    return pl.pallas_call(
        functools.partial(_hgrn_kernel, tt=tt),
        out_shape=[jax.ShapeDtypeStruct((t, HW), F32)] * 2,
        grid=(batch, nc + nt),
        in_specs=[tf, tf, tf, tb, tb, tb],
        out_specs=[tf, tb],
        scratch_shapes=[pltpu.VMEM((HEADS, HEAD_DIM, HEAD_DIM), F32)] * 2,
        compiler_params=_cparams("arbitrary", "arbitrary"),
        name="hgrn",
    )(q, i, lf_f, q, i, lf_b)


def _route(logits):
    lane = lax.broadcasted_iota(jnp.int32, logits.shape, 1).astype(F32)
    neg = -jnp.inf
    is_group = lane < N_GROUPS
    gl = jnp.where(is_group, logits, neg)
    gmax = jnp.max(gl, axis=-1, keepdims=True)
    g_sel = jnp.min(jnp.where(gl == gmax, lane, float(ROUTE_LANES)), axis=-1, keepdims=True)
    den = jnp.sum(jnp.where(is_group, jnp.exp(logits - gmax), 0.0), axis=-1, keepdims=True)
    p_sel = 1.0 / den
    first = N_GROUPS + EXPERTS_PER_GROUP * g_sel
    el = jnp.where((lane >= first) & (lane < first + EXPERTS_PER_GROUP), logits, neg)
    t1 = jnp.max(el, axis=-1, keepdims=True)
    i1 = jnp.min(jnp.where(el == t1, lane, float(ROUTE_LANES)), axis=-1, keepdims=True)
    el2 = jnp.where(lane == i1, neg, el)
    t2 = jnp.max(el2, axis=-1, keepdims=True)
    i2 = jnp.min(jnp.where(el2 == t2, lane, float(ROUTE_LANES)), axis=-1, keepdims=True)
    e2 = jnp.exp(t2 - t1)
    w1 = p_sel / (1.0 + e2)
    w2 = p_sel * e2 / (1.0 + e2)
    rec = jnp.where(lane == 0.0, i1 - N_GROUPS, 0.0)
    rec = jnp.where(lane == 1.0, i2 - N_GROUPS, rec)
    rec = jnp.where(lane == 2.0, w1, rec)
    return jnp.where(lane == 3.0, w2, rec)


def _mixer_kernel(of_ref, ob_ref, sg_ref, gu_ref, vn_ref, *refs, tm, n_a_tiles, two_inputs):
    xa_ref, xb_ref = (refs[0], refs[1]) if two_inputs else (refs[0], None)
    (hgain_ref, ws_ref, bs_ref, wo_ref, g1_ref, n2_ref, sh2_ref, sc2_ref, wrh_ref, wrl_ref, br_ref,
     xo_ref, h2_ref, rt_ref, rtt_ref, cat_ref) = refs[2 if two_inputs else 1:]
    o = of_ref[...] + ob_ref[...]
    for h in range(HEADS):
        sl = slice(h * HEAD_DIM, (h + 1) * HEAD_DIM)
        hg = _rms(o[:, sl]) * hgain_ref[:, sl] * sg_ref[:, sl].astype(F32)
        cat_ref[:, sl] = hg.astype(BF16)
    for cc in range(tm // SGU_CHUNK):
        rows = slice(cc * SGU_CHUNK, (cc + 1) * SGU_CHUNK)
        for h in range(HEADS):
            sl = slice(h * HEAD_DIM, (h + 1) * HEAD_DIM)
            mixed = _dot(ws_ref[h], vn_ref[rows, sl]) + bs_ref[h]
            cat_ref[rows, HW + h * HEAD_DIM:HW + (h + 1) * HEAD_DIM] = (
                gu_ref[rows, sl].astype(F32) * mixed).astype(BF16)
    xn = _stream_tile(xa_ref, xb_ref, n_a_tiles) + g1_ref[...] * _dot(cat_ref[...], wo_ref[...])
    xo_ref[...] = xn
    h2 = _rms(xn) * n2_ref[...]
    h2 = h2 * (1.0 + sc2_ref[...]) + sh2_ref[...]
    _store_token_tiles(h2_ref, h2)
    hi, lo = _split2(h2)
    logits = _dot(hi, wrh_ref[...]) + (_dot(lo, wrh_ref[...]) + _dot(hi, wrl_ref[...])) + br_ref[...]
    rec = _route(logits)
    rt_ref[...] = rec
    rtt_ref[...] = rec.T[:SUBLANES, :]


def _mixer(o_f, o_b, sg, gu, vn, xa, xb, t_a, t_b, mod3, tokens_per_mod_row, hgain, w_s, b_s, w_out, n2,
           wr_hi, wr_lo, br):
    d = xa.shape[1]
    tm = MIX_ROWS
    n_a, n_b = t_a // tm, t_b // tm
    t = t_a + t_b
    nd = d // LANES
    row = functools.partial(_mod_row, d, tokens_per_mod_row // tm, mod3.shape[0])
    tok = pl.BlockSpec((tm, HW), lambda i: (i, 0))
    wide = pl.BlockSpec((tm, d), lambda i: (i, 0))
    x_in, x_spec = _stream_inputs(xa, xb, n_a, tm, d)
    return pl.pallas_call(
        functools.partial(_mixer_kernel, tm=tm, n_a_tiles=n_a, two_inputs=xb is not None),
        out_shape=[jax.ShapeDtypeStruct((t, d), F32), jax.ShapeDtypeStruct((t * nd, LANES), F32),
                   jax.ShapeDtypeStruct((t, ROUTE_LANES), F32), jax.ShapeDtypeStruct((SUBLANES, t), F32)],
        grid=(n_a + n_b,),
        in_specs=[tok, tok, tok, tok, tok] + x_spec + [_full(hgain), _full(w_s), _full(b_s), _full(w_out),
                  row(2), _full(n2), row(3), row(4), _full(wr_hi), _full(wr_lo), _full(br)],
        out_specs=[wide, pl.BlockSpec((tm * nd, LANES), lambda i: (i, 0)),
                   pl.BlockSpec((tm, ROUTE_LANES), lambda i: (i, 0)), pl.BlockSpec((SUBLANES, tm), lambda i: (0, i))],
        scratch_shapes=[pltpu.VMEM((tm, 2 * HW), BF16)],
        compiler_params=_cparams("arbitrary"),
        name="mixer",
    )(o_f, o_b, sg, gu, vn, *x_in, hgain, w_s, b_s, w_out, mod3, n2, mod3, mod3, wr_hi, wr_lo, br)


MOE_ROWS = 256
MOE_LAG = 3
RING = 3
DMA_GROUPS = 8


def _moe_kernel(last_ref, sexp_ref, tab_hbm, h_hbm, wg_ref, wu_ref, wd_ref, y_hbm,
                idx_ref, xbuf, ybuf, zbuf, xb_ref, hm_ref, wgb, wub, wdb, sem_idx, sem_g, sem_s, sem_z):
    i = pl.program_id(0)
    last = last_ref[0]
    bm = MOE_ROWS
    de = wgb.shape[1]
    d = wgb.shape[0]
    nd = d // LANES

    def idx_copy(step, slot):
        return pltpu.make_async_copy(tab_hbm.at[step], idx_ref.at[slot], sem_idx.at[slot])

    def gathered(slot):
        return pltpu.make_async_copy(h_hbm.at[pl.ds(0, bm * nd)], xbuf.at[slot], sem_g.at[slot])

    def scattered(slot):
        return pltpu.make_async_copy(ybuf.at[slot], y_hbm.at[pl.ds(0, bm * nd)], sem_s.at[slot])

    def step(k):
        gslot = k
        cslot = (k + 1) % RING
        sslot = k

        if k == 0:
            @pl.when(i == 0)
            def _():
                xbuf[...] = jnp.zeros_like(xbuf)
                ybuf[...] = jnp.zeros_like(ybuf)
                zbuf[...] = jnp.zeros_like(zbuf)
                idx_copy(0, 0).start()

        idx_copy(i, k).wait()

        @pl.when(i < last)
        def _():
            idx_copy(i + 1, (k + 1) % RING).start()

        @pl.when(i >= 2)
        def _():
            gathered(cslot).wait()
            scattered(cslot).wait()

        @pl.when((i == 0) | (sexp_ref[i] != sexp_ref[jnp.maximum(i - 1, 0)]))
        def _():
            wgb[...] = wg_ref[...].astype(BF16)
            wub[...] = wu_ref[...].astype(BF16)
            wdb[...] = wd_ref[...].astype(BF16)

        per = bm // (DMA_GROUPS // 2)

        def scatter_group(g):
            for r in range(g * per, (g + 1) * per):
                dst = pl.multiple_of(idx_ref[k, 1, r], nd)
                pltpu.make_async_copy(ybuf.at[sslot, pl.ds(r * nd, nd)], y_hbm.at[pl.ds(dst, nd)],
                                      sem_s.at[sslot]).start(priority=r % 2)

        def gather_group(g):
            for r in range(g * per, (g + 1) * per):
                src = pl.multiple_of(idx_ref[k, 0, r], nd)
                pltpu.make_async_copy(h_hbm.at[pl.ds(src, nd)], xbuf.at[gslot, pl.ds(r * nd, nd)],
                                      sem_g.at[gslot]).start(priority=r % 2)

        dma_groups = [functools.partial(scatter_group, g) for g in range(DMA_GROUPS // 2)]
        dma_groups += [functools.partial(gather_group, g) for g in range(DMA_GROUPS // 2)]

        def issue_some():
            if dma_groups:
                dma_groups.pop(0)()

        xsrc = xbuf.at[cslot]
        for j in range(nd):
            xb_ref[:, j * LANES:(j + 1) * LANES] = xsrc[pl.ds(j, bm, stride=nd), :].astype(BF16)
        nh = 2
        for j in range(nh):
            cs = slice(j * de // nh, (j + 1) * de // nh)
            issue_some()
            gate = _dot(xb_ref[...], wgb[:, cs])
            issue_some()
            hm_ref[:, cs] = (_silu(gate) * _dot(xb_ref[...], wub[:, cs])).astype(BF16)
        ydst = ybuf.at[cslot]
        n_down = min(4, nd)
        for j in range(n_down):
            issue_some()
            _store_token_tiles_cols(ydst, _dot(hm_ref[...], wdb[:, j * d // n_down:(j + 1) * d // n_down]),
                                    j * nd // n_down, nd)
        while dma_groups:
            issue_some()

        @pl.when(i == last)
        def _():
            gathered(gslot).wait()
            gathered((k + 2) % RING).wait()
            scattered(sslot).wait()
            scattered((k + 2) % RING).wait()

    for k in range(RING):
        pl.when((i <= last) & (i % RING == k))(functools.partial(step, k))

    @pl.when(i > last)
    def _():
        fill = pltpu.make_async_copy(zbuf, y_hbm.at[pl.ds((i - MOE_LAG) * (bm * nd), bm * nd)], sem_z.at[0])
        fill.start()
        fill.wait()


def _moe(h2, tab, last, sexp, n_out_rows, w_gate, w_up, w_down, layer):
    d, de = w_gate.shape[-2:]
    nd = d // LANES
    n_steps = tab.shape[0]
    grid_spec = pltpu.PrefetchScalarGridSpec(
        num_scalar_prefetch=2,
        grid=(n_steps,),
        in_specs=[
            pl.BlockSpec(memory_space=pl.ANY),
            pl.BlockSpec(memory_space=pl.ANY),
            pl.BlockSpec((None, None, d, de), lambda i, la, se: (layer, se[i], 0, 0)),
            pl.BlockSpec((None, None, d, de), lambda i, la, se: (layer, se[i], 0, 0)),
            pl.BlockSpec((None, None, de, d), lambda i, la, se: (layer, se[i], 0, 0)),
        ],
        out_specs=pl.BlockSpec(memory_space=pl.ANY),
        scratch_shapes=[
            pltpu.SMEM((RING, 2, MOE_ROWS), jnp.int32),
            pltpu.VMEM((RING, MOE_ROWS * nd, LANES), F32),
            pltpu.VMEM((RING, MOE_ROWS * nd, LANES), F32),
            pltpu.VMEM((MOE_ROWS * nd, LANES), F32),
            pltpu.VMEM((MOE_ROWS, d), BF16),
            pltpu.VMEM((MOE_ROWS, de), BF16),
            pltpu.VMEM((d, de), BF16),
            pltpu.VMEM((d, de), BF16),
            pltpu.VMEM((de, d), BF16),
            pltpu.SemaphoreType.DMA((RING,)),
            pltpu.SemaphoreType.DMA((RING,)),
            pltpu.SemaphoreType.DMA((RING,)),
            pltpu.SemaphoreType.DMA((1,)),
        ],
    )
    return pl.pallas_call(
        _moe_kernel,
        out_shape=jax.ShapeDtypeStruct((n_out_rows * nd, LANES), F32),
        grid_spec=grid_spec,
        compiler_params=_cparams("arbitrary"),
        name="moe",
    )(last, sexp, tab, h2, w_gate, w_up, w_down)


def _dispatch_tables(expert_flat, ttot, nd):
    bm = MOE_ROWS
    n_slots = expert_flat.shape[0]
    n_blocks = -(-n_slots // bm) + N_EXPERTS
    n_steps = n_blocks + MOE_LAG
    n_main = 1 << (n_slots.bit_length() - 1)
    pieces = []
    for lo, n in ((0, n_main), (n_main, n_slots - n_main)):
        if n:
            e = expert_flat[lo:lo + n]
            _, order = lax.sort_key_val(e, lax.iota(jnp.int32, n))
            cnt = jnp.sum(e[:, None] == jnp.arange(N_EXPERTS, dtype=jnp.int32)[None, :], axis=0, dtype=jnp.int32)
            pieces.append((order + lo, cnt))
    counts = sum(cnt for _, cnt in pieces)
    padded = (counts + bm - 1) // bm * bm
    pad_end = jnp.cumsum(padded)
    pad_start = pad_end - padded
    start = jnp.cumsum(counts) - counts
    blk_row0 = jnp.arange(n_blocks, dtype=jnp.int32) * bm
    bexp = jnp.minimum(jnp.sum(pad_end[None, :] <= blk_row0[:, None], axis=1), N_EXPERTS - 1).astype(jnp.int32)
    lane = jnp.arange(bm, dtype=jnp.int32)[None, :]
    off = (blk_row0 - pad_start[bexp])[:, None] + lane
    valid = off < counts[bexp][:, None]
    src, rem, base = jnp.zeros_like(off), off, 0
    for order, cnt in pieces:
        first = (jnp.cumsum(cnt) - cnt)[bexp][:, None]
        here = (rem >= 0) & (rem < cnt[bexp][:, None])
        src = jnp.where(here, base + first + rem, src)
        rem = rem - cnt[bexp][:, None]
        base += order.shape[0]
    slot = jnp.concatenate([order for order, _ in pieces])[jnp.clip(src, 0, n_slots - 1)]
    pad_rank = blk_row0[:, None] + lane - (start[bexp] + counts[bexp])[:, None]
    gsrc = jnp.where(valid, slot % ttot, 0)
    sdst = jnp.where(valid, slot, n_slots + pad_rank)
    spare = n_blocks * bm + jnp.arange(MOE_LAG * bm, dtype=jnp.int32).reshape(MOE_LAG, bm)
    gtab = jnp.concatenate([gsrc, jnp.zeros((MOE_LAG, bm), jnp.int32)], axis=0)
    stab = jnp.concatenate([spare, sdst], axis=0)
    tab = (jnp.stack([gtab, stab], axis=1) * nd).astype(jnp.int32)
    n_used = jnp.sum(padded) // bm
    last = (n_used + MOE_LAG - 1).astype(jnp.int32).reshape(1)
    sexp = bexp[jnp.clip(jnp.arange(n_steps) - (MOE_LAG - 1), 0, n_blocks - 1)]
    return tab, last, sexp, n_steps * bm


def _combine_kernel(x_ref, y0_ref, y1_ref, rt_ref, g2_ref, fn_ref, o_ref, *, final):
    w = rt_ref[...]
    tm, d = x_ref.shape
    nd = d // LANES
    f = w[:, 2:3] * _load_token_tiles(y0_ref, tm, nd) + w[:, 3:4] * _load_token_tiles(y1_ref, tm, nd)
    xn = x_ref[...] + g2_ref[...] * f
    if final:
        xn = _rms(xn) * fn_ref[...]
    o_ref[...] = xn


def _combine(x2d, y2, t, ttot, route, mod3, tokens_per_mod_row, final_norm, final):
    d = x2d.shape[1]
    tm = MIX_ROWS
    b1 = ttot // tm
    nd = d // LANES
    return pl.pallas_call(
        functools.partial(_combine_kernel, final=final),
        out_shape=jax.ShapeDtypeStruct((t, d), F32),
        grid=(t // tm,),
        in_specs=[pl.BlockSpec((tm, d), lambda i: (i, 0)),
                  pl.BlockSpec((tm * nd, LANES), lambda i: (i, 0)),
                  pl.BlockSpec((tm * nd, LANES), lambda i: (i + b1, 0)),
                  pl.BlockSpec((tm, ROUTE_LANES), lambda i: (i, 0)),
                  _mod_row(d, tokens_per_mod_row // tm, mod3.shape[0], 5),
                  _full(final_norm)],
        out_specs=pl.BlockSpec((tm, d), lambda i: (i, 0)),
        compiler_params=_cparams("arbitrary"),
        name="combine",
    )(x2d, y2, y2, route, mod3, final_norm)


def kernel(x, c, ctx, c_ctx, norm1, norm2, w_mod, b_mod, w_in, lb_logits, hgrn_norm, sgu_norm, sgu_w, sgu_b,
           w_out, w_group, b_group, w_router, b_router, w_gate, w_up, w_down, final_norm):
    b, l, d = x.shape
    lc = ctx.shape[1]
    depth = w_mod.shape[0]
    t_lat, t_ctx = b * l, b * lc
    nd = d // LANES

    lb_cum = jnp.cumsum(jax.nn.softmax(lb_logits.astype(F32), axis=0), axis=0)
    lower_bound = jnp.maximum(lb_cum - lb_cum[0:1], 0.0)

    cc = jnp.zeros((MOD_ROWS, d), F32).at[:b].set(c).at[b].set(c_ctx)
    mod = _modulation(cc, w_mod, b_mod)

    w_route = jnp.concatenate([w_group, w_router], axis=-1)
    w_route = jnp.pad(w_route, ((0, 0), (0, 0), (0, ROUTE_LANES - w_route.shape[-1])))
    wr_hi = w_route.astype(BF16)
    wr_lo = (w_route - wr_hi.astype(F32)).astype(BF16)
    b_route = jnp.concatenate([b_group, b_router], axis=-1)
    b_route = jnp.pad(b_route, ((0, 0), (0, ROUTE_LANES - b_route.shape[-1])))[:, None, :]
    b_s = jnp.broadcast_to(sgu_b[..., None], sgu_b.shape + (HEAD_DIM,)).astype(F32)

    w_in_b, w_out_b, sgu_w_b = w_in.astype(BF16), w_out.astype(BF16), sgu_w.astype(BF16)

    xa, xb, t_a, t_b = x.reshape(t_lat, d), ctx.reshape(t_ctx, d), t_lat, t_ctx
    fn = final_norm.reshape(1, d)

    for layer in range(depth):
        last = layer == depth - 1
        mod3 = mod[layer, :b + 1].reshape(b + 1, 1, N_MOD * d)
        n1 = norm1[layer].reshape(1, d)
        n2 = norm2[layer].reshape(1, d)
        sgn = sgu_norm[layer].reshape(1, HW)
        hgain = hgrn_norm[layer].reshape(1, HW)

        q, i, lf_f, lf_b, sg, gu, vn = _inproj(xa, xb, t_a, t_b, mod3, l, n1, w_in_b[layer],
                                               lower_bound[layer], sgn, layer == 0)
        o_f, o_b = _hgrn(q, i, lf_f, lf_b, b, l, lc)

        if last:
            xb, t_a, t_b = None, t_lat, 0
        ttot = t_a + t_b
        xs, h2, route, route_t = _mixer(o_f, o_b, sg, gu, vn, xa, xb, t_a, t_b, mod3, l, hgain, sgu_w_b[layer],
                                        b_s[layer], w_out_b[layer], n2, wr_hi[layer], wr_lo[layer],
                                        b_route[layer])

        expert_flat = route_t[:TOP_K].astype(jnp.int32).reshape(-1)
        tab, last_step, sexp, n_out_rows = _dispatch_tables(expert_flat, ttot, nd)
        y2 = _moe(h2, tab, last_step, sexp, n_out_rows, w_gate, w_up, w_down, layer)
        xs = _combine(xs, y2, ttot, ttot, route, mod3, l, fn, final=last)

        xa, xb, t_a, t_b = xs, None, ttot, 0

    return xs.reshape(b, l, d)
```

```python
import functools

import jax
import jax.numpy as jnp
from jax import lax
from jax.experimental import pallas as pl
from jax.experimental.pallas import tpu as pltpu

F32 = jnp.float32
BF16 = jnp.bfloat16

EPS = 1e-6
LOG2_E = 1.4426950408889634
HEADS = 4
HEAD_DIM = 128
HW = HEADS * HEAD_DIM
HGRN_CHUNK = 64
SGU_CHUNK = 128
N_GROUPS = 4
EXPERTS_PER_GROUP = 8
N_EXPERTS = N_GROUPS * EXPERTS_PER_GROUP
TOP_K = 2
N_MOD = 6
LANES = 128
SUBLANES = 8
ROUTE_LANES = LANES
MOD_ROWS = 16

PROJ_ROWS = 512
SCAN_ROWS = 256
MIX_ROWS = 512

VMEM_LIMIT = 48 * 1024 * 1024


def _cparams(*sem):
    return pltpu.CompilerParams(dimension_semantics=sem, vmem_limit_bytes=VMEM_LIMIT)


def _split2(a):
    hi = a.astype(BF16)
    lo = (a - hi.astype(F32)).astype(BF16)
    return hi, lo


def _dot(a, b):
    return jnp.dot(a, b, preferred_element_type=F32)


def _dot_nt(a, b):
    return lax.dot_general(a, b, (((1,), (1,)), ((), ())), preferred_element_type=F32)


def _dot_tn(a, b):
    return lax.dot_general(a, b, (((0,), (0,)), ((), ())), preferred_element_type=F32)


def _dot3(a, b):
    ah, al = _split2(a)
    bh, bl = _split2(b)
    return _dot(ah, bh) + (_dot(al, bh) + _dot(ah, bl))


def _silu(x):
    return x / (1.0 + jnp.exp(-x))


def _rms(x):
    return x * lax.rsqrt(jnp.mean(x * x, axis=-1, keepdims=True) + EPS)


def _full(a):
    return pl.BlockSpec(a.shape, lambda *_: (0,) * a.ndim)


def _stream_inputs(xa, xb, n_a, tm, d):
    if xb is None:
        return [xa], [pl.BlockSpec((tm, d), lambda i: (i, 0))]
    return [xa, xb], [pl.BlockSpec((tm, d), lambda i: (jnp.minimum(i, n_a - 1), 0)),
                      pl.BlockSpec((tm, d), lambda i: (jnp.maximum(i - n_a, 0), 0))]


def _mod_row(d, tiles_per_row, n_rows, j):
    return pl.BlockSpec((None, 1, d), lambda i: (jnp.minimum(i // tiles_per_row, n_rows - 1), 0, j))


def _store_token_tiles_cols(ref, x, j0, nd):
    rows, w = x.shape
    for j in range(w // LANES):
        ref[pl.ds(j0 + j, rows, stride=nd), :] = x[:, j * LANES:(j + 1) * LANES]


def _store_token_tiles(ref, x):
    _store_token_tiles_cols(ref, x, 0, x.shape[1] // LANES)


def _load_token_tiles(ref, rows, nd):
    return jnp.concatenate([ref[pl.ds(j, rows, stride=nd), :] for j in range(nd)], axis=-1)


def _mod_kernel(c_ref, w_ref, b_ref, o_ref):
    o_ref[...] = _dot3(_silu(c_ref[...]), w_ref[...]) + b_ref[...]


def _modulation(cc, w_mod, b_mod):
    depth, d, n = w_mod.shape
    tn = 1536
    return pl.pallas_call(
        _mod_kernel,
        out_shape=jax.ShapeDtypeStruct((depth, MOD_ROWS, n), F32),
        grid=(depth, n // tn),
        in_specs=[
            pl.BlockSpec((MOD_ROWS, d), lambda l, j: (0, 0)),
            pl.BlockSpec((None, d, tn), lambda l, j: (l, 0, j)),
            pl.BlockSpec((None, 1, tn), lambda l, j: (l, 0, j)),
        ],
        out_specs=pl.BlockSpec((None, MOD_ROWS, tn), lambda l, j: (l, 0, j)),
        compiler_params=_cparams("arbitrary", "arbitrary"),
        name="modulation",
    )(cc, w_mod, b_mod.reshape(depth, 1, n))


def _log_forget(z, lb, lb_is_zero):
    ls = jnp.minimum(z, 0.0) - jnp.log(1.0 + jnp.exp(-jnp.abs(z)))
    if lb_is_zero:
        return ls
    return jnp.maximum(jnp.log(lb + (1.0 - lb) * jnp.exp(ls)), ls)


def _stream_tile(xa_ref, xb_ref, n_a_tiles):
    if xb_ref is None:
        return xa_ref[...]
    return jnp.where(pl.program_id(0) < n_a_tiles, xa_ref[...], xb_ref[...])


def _inproj_kernel(*refs, lb_is_zero, n_a_tiles, two_inputs):
    xa_ref, xb_ref = (refs[0], refs[1]) if two_inputs else (refs[0], None)
    (sh_ref, sc_ref, n1_ref, w_ref, lb_ref, sgn_ref,
     q_ref, i_ref, lff_ref, lfb_ref, sg_ref, gu_ref, vn_ref) = refs[2 if two_inputs else 1:]
    h = _rms(_stream_tile(xa_ref, xb_ref, n_a_tiles)) * n1_ref[...]
    hb = (h * (1.0 + sc_ref[...]) + sh_ref[...]).astype(BF16)

    def proj(j):
        return _dot(hb, w_ref[:, j * HW:(j + 1) * HW])

    q_ref[...] = proj(0).astype(BF16)
    lff_ref[...] = _log_forget(proj(1), lb_ref[0:1, :], lb_is_zero)
    lfb_ref[...] = _log_forget(proj(2), lb_ref[1:2, :], lb_is_zero)
    i_ref[...] = proj(3).astype(BF16)
    sg_ref[...] = _silu(proj(4)).astype(BF16)
    gu_ref[...] = jax.nn.gelu(proj(5)).astype(BF16)
    vn_ref[...] = (_rms(jax.nn.gelu(proj(6))) * sgn_ref[...]).astype(BF16)


def _inproj(xa, xb, t_a, t_b, mod3, tokens_per_mod_row, n1, w_in, lb, sgu_gain, lb_is_zero):
    d = xa.shape[1]
    tm = PROJ_ROWS
    n_a, n_b = t_a // tm, t_b // tm
    t = t_a + t_b
    tok = pl.BlockSpec((tm, HW), lambda i: (i, 0))
    row = functools.partial(_mod_row, d, tokens_per_mod_row // tm, mod3.shape[0])
    x_in, x_spec = _stream_inputs(xa, xb, n_a, tm, d)
    return pl.pallas_call(
        functools.partial(_inproj_kernel, lb_is_zero=lb_is_zero, n_a_tiles=n_a, two_inputs=xb is not None),
        out_shape=[jax.ShapeDtypeStruct((t, HW), dt) for dt in (BF16, BF16, F32, F32, BF16, BF16, BF16)],
        grid=(n_a + n_b,),
        in_specs=x_spec + [row(0), row(1), _full(n1), _full(w_in), _full(lb), _full(sgu_gain)],
        out_specs=[tok] * 7,
        compiler_params=_cparams("arbitrary"),
        name="inproj",
    )(*x_in, mod3, mod3, n1, w_in, lb, sgu_gain)


def _chunk_cumsum(x, reverse):
    c, w = x.shape
    g = c // SUBLANES
    x3 = x.reshape(g, SUBLANES, w)
    sub = lax.broadcasted_iota(jnp.int32, x3.shape, 1)
    for s in (1, 2, 4):
        if reverse:
            x3 = x3 + jnp.where(sub < SUBLANES - s, pltpu.roll(x3, SUBLANES - s, axis=1), 0.0)
        else:
            x3 = x3 + jnp.where(sub >= s, pltpu.roll(x3, s, axis=1), 0.0)
    edge = 0 if reverse else SUBLANES - 1
    tot = x3[:, edge:edge + 1, :]
    offs = [None] * g
    acc = jnp.zeros((1, w), F32)
    for gi in (reversed(range(g)) if reverse else range(g)):
        offs[gi] = acc
        acc = acc + tot[gi]
    x3 = x3 + jnp.stack(offs, axis=0)
    return x3.reshape(c, w)


def _scan_chunk(q_ref, i_ref, lf_ref, o_ref, st_ref, r0, reverse):
    c = HGRN_CHUNK
    rows = lax.broadcasted_iota(jnp.int32, (c, c), 0)
    cols = lax.broadcasted_iota(jnp.int32, (c, c), 1)
    incl = (cols >= rows) if reverse else (cols <= rows)
    ref_row = c // 2 if reverse else c // 2 - 1
    tot_row = 0 if reverse else c - 1
    lf = lf_ref[pl.ds(r0, c), :] * LOG2_E
    cum = _chunk_cumsum(lf, reverse)
    ref = cum[ref_row:ref_row + 1, :]
    tot = cum[tot_row:tot_row + 1, :]
    k = 1.0 - jnp.exp2(lf)
    qf = q_ref[pl.ds(r0, c), :].astype(F32)
    iv = i_ref[pl.ds(r0, c), :]
    q_in = (qf * jnp.exp2(cum - ref)).astype(BF16)
    k_in = (k * jnp.exp2(ref - cum)).astype(BF16)
    k_st = (k * jnp.exp2(tot - cum)).astype(BF16)
    q_st = (qf * jnp.exp2(cum)).astype(BF16)
    dec = jnp.exp2(tot)
    for h in range(HEADS):
        sl = slice(h * HEAD_DIM, (h + 1) * HEAD_DIM)
        sc = _dot_nt(q_in[:, sl], k_in[:, sl])
        sc = jnp.where(incl, sc, 0.0).astype(BF16)
        st = st_ref[h]
        o_ref[pl.ds(r0, c), sl] = _dot(sc, iv[:, sl]) + _dot_nt(q_st[:, sl], st.astype(BF16))
        st_ref[h] = st * dec[:, sl] + _dot_tn(iv[:, sl], k_st[:, sl])


def _hgrn_kernel(qf_ref, if_ref, lff_ref, qb_ref, ib_ref, lfb_ref, of_ref, ob_ref, stf_ref, stb_ref, *, tt):
    @pl.when(pl.program_id(1) == 0)
    def _():
        stf_ref[...] = jnp.zeros_like(stf_ref)
        stb_ref[...] = jnp.zeros_like(stb_ref)

    nchunks = tt // HGRN_CHUNK
    for ci in range(nchunks):
        _scan_chunk(qf_ref, if_ref, lff_ref, of_ref, stf_ref, ci * HGRN_CHUNK, False)
        _scan_chunk(qb_ref, ib_ref, lfb_ref, ob_ref, stb_ref, (nchunks - 1 - ci) * HGRN_CHUNK, True)


def _hgrn(q, i, lf_f, lf_b, batch, seq, ctx_len):
    t = q.shape[0]
    tt = SCAN_ROWS
    nt, nc = seq // tt, ctx_len // tt
    ctx0 = batch * nt

    def fwd(b, s):
        return (jnp.where(s < nc, ctx0 + b * nc + s, b * nt + (s - nc)), 0)

    def bwd(b, s):
        return (jnp.where(s < nc, ctx0 + b * nc + (nc - 1 - s), b * nt + (nt - 1 - (s - nc))), 0)

    tf, tb = pl.BlockSpec((tt, HW), fwd), pl.BlockSpec((tt, HW), bwd)
    return pl.pallas_call(
        functools.partial(_hgrn_kernel, tt=tt),
        out_shape=[jax.ShapeDtypeStruct((t, HW), F32)] * 2,
        grid=(batch, nc + nt),
        in_specs=[tf, tf, tf, tb, tb, tb],
        out_specs=[tf, tb],
        scratch_shapes=[pltpu.VMEM((HEADS, HEAD_DIM, HEAD_DIM), F32)] * 2,
        compiler_params=_cparams("arbitrary", "arbitrary"),
        name="hgrn",
    )(q, i, lf_f, q, i, lf_b)


def _route(logits):
    lane = lax.broadcasted_iota(jnp.int32, logits.shape, 1).astype(F32)
    neg = -jnp.inf
    is_group = lane < N_GROUPS
    gl = jnp.where(is_group, logits, neg)
    gmax = jnp.max(gl, axis=-1, keepdims=True)
    g_sel = jnp.min(jnp.where(gl == gmax, lane, float(ROUTE_LANES)), axis=-1, keepdims=True)
    den = jnp.sum(jnp.where(is_group, jnp.exp(logits - gmax), 0.0), axis=-1, keepdims=True)
    p_sel = 1.0 / den
    first = N_GROUPS + EXPERTS_PER_GROUP * g_sel
    el = jnp.where((lane >= first) & (lane < first + EXPERTS_PER_GROUP), logits, neg)
    t1 = jnp.max(el, axis=-1, keepdims=True)
    i1 = jnp.min(jnp.where(el == t1, lane, float(ROUTE_LANES)), axis=-1, keepdims=True)
    el2 = jnp.where(lane == i1, neg, el)
    t2 = jnp.max(el2, axis=-1, keepdims=True)
    i2 = jnp.min(jnp.where(el2 == t2, lane, float(ROUTE_LANES)), axis=-1, keepdims=True)
    e2 = jnp.exp(t2 - t1)
    w1 = p_sel / (1.0 + e2)
    w2 = p_sel * e2 / (1.0 + e2)
    rec = jnp.where(lane == 0.0, i1 - N_GROUPS, 0.0)
    rec = jnp.where(lane == 1.0, i2 - N_GROUPS, rec)
    rec = jnp.where(lane == 2.0, w1, rec)
    return jnp.where(lane == 3.0, w2, rec)


def _mixer_kernel(of_ref, ob_ref, sg_ref, gu_ref, vn_ref, *refs, tm, n_a_tiles, two_inputs):
    xa_ref, xb_ref = (refs[0], refs[1]) if two_inputs else (refs[0], None)
    (hgain_ref, ws_ref, bs_ref, wo_ref, g1_ref, n2_ref, sh2_ref, sc2_ref, wrh_ref, wrl_ref, br_ref,
     xo_ref, h2_ref, rt_ref, rtt_ref, cat_ref) = refs[2 if two_inputs else 1:]
    o = of_ref[...] + ob_ref[...]
    for h in range(HEADS):
        sl = slice(h * HEAD_DIM, (h + 1) * HEAD_DIM)
        hg = _rms(o[:, sl]) * hgain_ref[:, sl] * sg_ref[:, sl].astype(F32)
        cat_ref[:, sl] = hg.astype(BF16)
    for cc in range(tm // SGU_CHUNK):
        rows = slice(cc * SGU_CHUNK, (cc + 1) * SGU_CHUNK)
        for h in range(HEADS):
            sl = slice(h * HEAD_DIM, (h + 1) * HEAD_DIM)
            mixed = _dot(ws_ref[h], vn_ref[rows, sl]) + bs_ref[h]
            cat_ref[rows, HW + h * HEAD_DIM:HW + (h + 1) * HEAD_DIM] = (
                gu_ref[rows, sl].astype(F32) * mixed).astype(BF16)
    xn = _stream_tile(xa_ref, xb_ref, n_a_tiles) + g1_ref[...] * _dot(cat_ref[...], wo_ref[...])
    xo_ref[...] = xn
    h2 = _rms(xn) * n2_ref[...]
    h2 = h2 * (1.0 + sc2_ref[...]) + sh2_ref[...]
    _store_token_tiles(h2_ref, h2)
    hi, lo = _split2(h2)
    logits = _dot(hi, wrh_ref[...]) + (_dot(lo, wrh_ref[...]) + _dot(hi, wrl_ref[...])) + br_ref[...]
    rec = _route(logits)
    rt_ref[...] = rec
    rtt_ref[...] = rec.T[:SUBLANES, :]


def _mixer(o_f, o_b, sg, gu, vn, xa, xb, t_a, t_b, mod3, tokens_per_mod_row, hgain, w_s, b_s, w_out, n2,
           wr_hi, wr_lo, br):
    d = xa.shape[1]
    tm = MIX_ROWS
    n_a, n_b = t_a // tm, t_b // tm
    t = t_a + t_b
    nd = d // LANES
    row = functools.partial(_mod_row, d, tokens_per_mod_row // tm, mod3.shape[0])
    tok = pl.BlockSpec((tm, HW), lambda i: (i, 0))
    wide = pl.BlockSpec((tm, d), lambda i: (i, 0))
    x_in, x_spec = _stream_inputs(xa, xb, n_a, tm, d)
    return pl.pallas_call(
        functools.partial(_mixer_kernel, tm=tm, n_a_tiles=n_a, two_inputs=xb is not None),
        out_shape=[jax.ShapeDtypeStruct((t, d), F32), jax.ShapeDtypeStruct((t * nd, LANES), F32),
                   jax.ShapeDtypeStruct((t, ROUTE_LANES), F32), jax.ShapeDtypeStruct((SUBLANES, t), F32)],
        grid=(n_a + n_b,),
        in_specs=[tok, tok, tok, tok, tok] + x_spec + [_full(hgain), _full(w_s), _full(b_s), _full(w_out),
                  row(2), _full(n2), row(3), row(4), _full(wr_hi), _full(wr_lo), _full(br)],
        out_specs=[wide, pl.BlockSpec((tm * nd, LANES), lambda i: (i, 0)),
                   pl.BlockSpec((tm, ROUTE_LANES), lambda i: (i, 0)), pl.BlockSpec((SUBLANES, tm), lambda i: (0, i))],
        scratch_shapes=[pltpu.VMEM((tm, 2 * HW), BF16)],
        compiler_params=_cparams("arbitrary"),
        name="mixer",
    )(o_f, o_b, sg, gu, vn, *x_in, hgain, w_s, b_s, w_out, mod3, n2, mod3, mod3, wr_hi, wr_lo, br)


MOE_ROWS = 256
MOE_LAG = 3
RING = 3
DMA_GROUPS = 8


def _moe_kernel(last_ref, sexp_ref, tab_hbm, h_hbm, wg_ref, wu_ref, wd_ref, y_hbm,
                idx_ref, xbuf, ybuf, zbuf, xb_ref, hm_ref, wgb, wub, wdb, sem_idx, sem_g, sem_s, sem_z):
    i = pl.program_id(0)
    last = last_ref[0]
    bm = MOE_ROWS
    de = wgb.shape[1]
    d = wgb.shape[0]
    nd = d // LANES

    def idx_copy(step, slot):
        return pltpu.make_async_copy(tab_hbm.at[step], idx_ref.at[slot], sem_idx.at[slot])

    def gathered(slot):
        return pltpu.make_async_copy(h_hbm.at[pl.ds(0, bm * nd)], xbuf.at[slot], sem_g.at[slot])

    def scattered(slot):
        return pltpu.make_async_copy(ybuf.at[slot], y_hbm.at[pl.ds(0, bm * nd)], sem_s.at[slot])

    def step(k):
        gslot = k
        cslot = (k + 1) % RING
        sslot = k

        if k == 0:
            @pl.when(i == 0)
            def _():
                xbuf[...] = jnp.zeros_like(xbuf)
                ybuf[...] = jnp.zeros_like(ybuf)
                zbuf[...] = jnp.zeros_like(zbuf)
                idx_copy(0, 0).start()

        idx_copy(i, k).wait()

        @pl.when(i < last)
        def _():
            idx_copy(i + 1, (k + 1) % RING).start()

        @pl.when(i >= 2)
        def _():
            gathered(cslot).wait()
            scattered(cslot).wait()

        @pl.when((i == 0) | (sexp_ref[i] != sexp_ref[jnp.maximum(i - 1, 0)]))
        def _():
            wgb[...] = wg_ref[...].astype(BF16)
            wub[...] = wu_ref[...].astype(BF16)
            wdb[...] = wd_ref[...].astype(BF16)

        per = bm // (DMA_GROUPS // 2)

        def scatter_group(g):
            for r in range(g * per, (g + 1) * per):
                dst = pl.multiple_of(idx_ref[k, 1, r], nd)
                pltpu.make_async_copy(ybuf.at[sslot, pl.ds(r * nd, nd)], y_hbm.at[pl.ds(dst, nd)],
                                      sem_s.at[sslot]).start(priority=r % 2)

        def gather_group(g):
            for r in range(g * per, (g + 1) * per):
                src = pl.multiple_of(idx_ref[k, 0, r], nd)
                pltpu.make_async_copy(h_hbm.at[pl.ds(src, nd)], xbuf.at[gslot, pl.ds(r * nd, nd)],
                                      sem_g.at[gslot]).start(priority=r % 2)

        dma_groups = [functools.partial(scatter_group, g) for g in range(DMA_GROUPS // 2)]
        dma_groups += [functools.partial(gather_group, g) for g in range(DMA_GROUPS // 2)]

        def issue_some():
            if dma_groups:
                dma_groups.pop(0)()

        xsrc = xbuf.at[cslot]
        for j in range(nd):
            xb_ref[:, j * LANES:(j + 1) * LANES] = xsrc[pl.ds(j, bm, stride=nd), :].astype(BF16)
        nh = 2
        for j in range(nh):
            cs = slice(j * de // nh, (j + 1) * de // nh)
            issue_some()
            gate = _dot(xb_ref[...], wgb[:, cs])
            issue_some()
            hm_ref[:, cs] = (_silu(gate) * _dot(xb_ref[...], wub[:, cs])).astype(BF16)
        ydst = ybuf.at[cslot]
        n_down = min(4, nd)
        for j in range(n_down):
            issue_some()
            _store_token_tiles_cols(ydst, _dot(hm_ref[...], wdb[:, j * d // n_down:(j + 1) * d // n_down]),
                                    j * nd // n_down, nd)
        while dma_groups:
            issue_some()

        @pl.when(i == last)
        def _():
            gathered(gslot).wait()
            gathered((k + 2) % RING).wait()
            scattered(sslot).wait()
            scattered((k + 2) % RING).wait()

    for k in range(RING):
        pl.when((i <= last) & (i % RING == k))(functools.partial(step, k))

    @pl.when(i > last)
    def _():
        fill = pltpu.make_async_copy(zbuf, y_hbm.at[pl.ds((i - MOE_LAG) * (bm * nd), bm * nd)], sem_z.at[0])
        fill.start()
        fill.wait()


def _moe(h2, tab, last, sexp, n_out_rows, w_gate, w_up, w_down, layer):
    d, de = w_gate.shape[-2:]
    nd = d // LANES
    n_steps = tab.shape[0]
    grid_spec = pltpu.PrefetchScalarGridSpec(
        num_scalar_prefetch=2,
        grid=(n_steps,),
        in_specs=[
            pl.BlockSpec(memory_space=pl.ANY),
            pl.BlockSpec(memory_space=pl.ANY),
            pl.BlockSpec((None, None, d, de), lambda i, la, se: (layer, se[i], 0, 0)),
            pl.BlockSpec((None, None, d, de), lambda i, la, se: (layer, se[i], 0, 0)),
            pl.BlockSpec((None, None, de, d), lambda i, la, se: (layer, se[i], 0, 0)),
        ],
        out_specs=pl.BlockSpec(memory_space=pl.ANY),
        scratch_shapes=[
            pltpu.SMEM((RING, 2, MOE_ROWS), jnp.int32),
            pltpu.VMEM((RING, MOE_ROWS * nd, LANES), F32),
            pltpu.VMEM((RING, MOE_ROWS * nd, LANES), F32),
            pltpu.VMEM((MOE_ROWS * nd, LANES), F32),
            pltpu.VMEM((MOE_ROWS, d), BF16),
            pltpu.VMEM((MOE_ROWS, de), BF16),
            pltpu.VMEM((d, de), BF16),
            pltpu.VMEM((d, de), BF16),
            pltpu.VMEM((de, d), BF16),
            pltpu.SemaphoreType.DMA((RING,)),
            pltpu.SemaphoreType.DMA((RING,)),
            pltpu.SemaphoreType.DMA((RING,)),
            pltpu.SemaphoreType.DMA((1,)),
        ],
    )
    return pl.pallas_call(
        _moe_kernel,
        out_shape=jax.ShapeDtypeStruct((n_out_rows * nd, LANES), F32),
        grid_spec=grid_spec,
        compiler_params=_cparams("arbitrary"),
        name="moe",
    )(last, sexp, tab, h2, w_gate, w_up, w_down)


def _dispatch_tables(expert_flat, ttot, nd):
    bm = MOE_ROWS
    n_slots = expert_flat.shape[0]
    n_blocks = -(-n_slots // bm) + N_EXPERTS
    n_steps = n_blocks + MOE_LAG
    n_main = 1 << (n_slots.bit_length() - 1)
    pieces = []
    for lo, n in ((0, n_main), (n_main, n_slots - n_main)):
        if n:
            e = expert_flat[lo:lo + n]
            _, order = lax.sort_key_val(e, lax.iota(jnp.int32, n))
            cnt = jnp.sum(e[:, None] == jnp.arange(N_EXPERTS, dtype=jnp.int32)[None, :], axis=0, dtype=jnp.int32)
            pieces.append((order + lo, cnt))
    counts = sum(cnt for _, cnt in pieces)
    padded = (counts + bm - 1) // bm * bm
    pad_end = jnp.cumsum(padded)
    pad_start = pad_end - padded
    start = jnp.cumsum(counts) - counts
    blk_row0 = jnp.arange(n_blocks, dtype=jnp.int32) * bm
    bexp = jnp.minimum(jnp.sum(pad_end[None, :] <= blk_row0[:, None], axis=1), N_EXPERTS - 1).astype(jnp.int32)
    lane = jnp.arange(bm, dtype=jnp.int32)[None, :]
    off = (blk_row0 - pad_start[bexp])[:, None] + lane
    valid = off < counts[bexp][:, None]
    src, rem, base = jnp.zeros_like(off), off, 0
    for order, cnt in pieces:
        first = (jnp.cumsum(cnt) - cnt)[bexp][:, None]
        here = (rem >= 0) & (rem < cnt[bexp][:, None])
        src = jnp.where(here, base + first + rem, src)
        rem = rem - cnt[bexp][:, None]
        base += order.shape[0]
    slot = jnp.concatenate([order for order, _ in pieces])[jnp.clip(src, 0, n_slots - 1)]
    pad_rank = blk_row0[:, None] + lane - (start[bexp] + counts[bexp])[:, None]
    gsrc = jnp.where(valid, slot % ttot, 0)
    sdst = jnp.where(valid, slot, n_slots + pad_rank)
    spare = n_blocks * bm + jnp.arange(MOE_LAG * bm, dtype=jnp.int32).reshape(MOE_LAG, bm)
    gtab = jnp.concatenate([gsrc, jnp.zeros((MOE_LAG, bm), jnp.int32)], axis=0)
    stab = jnp.concatenate([spare, sdst], axis=0)
    tab = (jnp.stack([gtab, stab], axis=1) * nd).astype(jnp.int32)
    n_used = jnp.sum(padded) // bm
    last = (n_used + MOE_LAG - 1).astype(jnp.int32).reshape(1)
    sexp = bexp[jnp.clip(jnp.arange(n_steps) - (MOE_LAG - 1), 0, n_blocks - 1)]
    return tab, last, sexp, n_steps * bm


def _combine_kernel(x_ref, y0_ref, y1_ref, rt_ref, g2_ref, fn_ref, o_ref, *, final):
    w = rt_ref[...]
    tm, d = x_ref.shape
    nd = d // LANES
    f = w[:, 2:3] * _load_token_tiles(y0_ref, tm, nd) + w[:, 3:4] * _load_token_tiles(y1_ref, tm, nd)
    xn = x_ref[...] + g2_ref[...] * f
    if final:
        xn = _rms(xn) * fn_ref[...]
    o_ref[...] = xn


def _combine(x2d, y2, t, ttot, route, mod3, tokens_per_mod_row, final_norm, final):
    d = x2d.shape[1]
    tm = MIX_ROWS
    b1 = ttot // tm
    nd = d // LANES
    return pl.pallas_call(
        functools.partial(_combine_kernel, final=final),
        out_shape=jax.ShapeDtypeStruct((t, d), F32),
        grid=(t // tm,),
        in_specs=[pl.BlockSpec((tm, d), lambda i: (i, 0)),
                  pl.BlockSpec((tm * nd, LANES), lambda i: (i, 0)),
                  pl.BlockSpec((tm * nd, LANES), lambda i: (i + b1, 0)),
                  pl.BlockSpec((tm, ROUTE_LANES), lambda i: (i, 0)),
                  _mod_row(d, tokens_per_mod_row // tm, mod3.shape[0], 5),
                  _full(final_norm)],
        out_specs=pl.BlockSpec((tm, d), lambda i: (i, 0)),
        compiler_params=_cparams("arbitrary"),
        name="combine",
    )(x2d, y2, y2, route, mod3, final_norm)


def kernel(x, c, ctx, c_ctx, norm1, norm2, w_mod, b_mod, w_in, lb_logits, hgrn_norm, sgu_norm, sgu_w, sgu_b,
           w_out, w_group, b_group, w_router, b_router, w_gate, w_up, w_down, final_norm):
    b, l, d = x.shape
    lc = ctx.shape[1]
    depth = w_mod.shape[0]
    t_lat, t_ctx = b * l, b * lc
    nd = d // LANES

    lb_cum = jnp.cumsum(jax.nn.softmax(lb_logits.astype(F32), axis=0), axis=0)
    lower_bound = jnp.maximum(lb_cum - lb_cum[0:1], 0.0)

    cc = jnp.zeros((MOD_ROWS, d), F32).at[:b].set(c).at[b].set(c_ctx)
    mod = _modulation(cc, w_mod, b_mod)

    w_route = jnp.concatenate([w_group, w_router], axis=-1)
    w_route = jnp.pad(w_route, ((0, 0), (0, 0), (0, ROUTE_LANES - w_route.shape[-1])))
    wr_hi = w_route.astype(BF16)
    wr_lo = (w_route - wr_hi.astype(F32)).astype(BF16)
    b_route = jnp.concatenate([b_group, b_router], axis=-1)
    b_route = jnp.pad(b_route, ((0, 0), (0, ROUTE_LANES - b_route.shape[-1])))[:, None, :]
    b_s = jnp.broadcast_to(sgu_b[..., None], sgu_b.shape + (HEAD_DIM,)).astype(F32)

    w_in_b, w_out_b, sgu_w_b = w_in.astype(BF16), w_out.astype(BF16), sgu_w.astype(BF16)

    xa, xb, t_a, t_b = x.reshape(t_lat, d), ctx.reshape(t_ctx, d), t_lat, t_ctx
    fn = final_norm.reshape(1, d)

    for layer in range(depth):
        last = layer == depth - 1
        mod3 = mod[layer, :b + 1].reshape(b + 1, 1, N_MOD * d)
        n1 = norm1[layer].reshape(1, d)
        n2 = norm2[layer].reshape(1, d)
        sgn = sgu_norm[layer].reshape(1, HW)
        hgain = hgrn_norm[layer].reshape(1, HW)

        q, i, lf_f, lf_b, sg, gu, vn = _inproj(xa, xb, t_a, t_b, mod3, l, n1, w_in_b[layer],
                                               lower_bound[layer], sgn, layer == 0)
        o_f, o_b = _hgrn(q, i, lf_f, lf_b, b, l, lc)

        if last:
            xb, t_a, t_b = None, t_lat, 0
        ttot = t_a + t_b
        xs, h2, route, route_t = _mixer(o_f, o_b, sg, gu, vn, xa, xb, t_a, t_b, mod3, l, hgain, sgu_w_b[layer],
                                        b_s[layer], w_out_b[layer], n2, wr_hi[layer], wr_lo[layer],
                                        b_route[layer])

        expert_flat = route_t[:TOP_K].astype(jnp.int32).reshape(-1)
        tab, last_step, sexp, n_out_rows = _dispatch_tables(expert_flat, ttot, nd)
        y2 = _moe(h2, tab, last_step, sexp, n_out_rows, w_gate, w_up, w_down, layer)
        xs = _combine(xs, y2, ttot, ttot, route, mod3, l, fn, final=last)

        xa, xb, t_a, t_b = xs, None, ttot, 0

    return xs.reshape(b, l, d)
```

```python
import functools

import jax
import jax.numpy as jnp
from jax import lax
from jax.experimental import pallas as pl
from jax.experimental.pallas import tpu as pltpu

F32 = jnp.float32
BF16 = jnp.bfloat16

EPS = 1e-6
LOG2_E = 1.4426950408889634
HEADS = 4
HEAD_DIM = 128
HW = HEADS * HEAD_DIM
HGRN_CHUNK = 64
SGU_CHUNK = 128
N_GROUPS = 4
EXPERTS_PER_GROUP = 8
N_EXPERTS = N_GROUPS * EXPERTS_PER_GROUP
TOP_K = 2
N_MOD = 6
LANES = 128
SUBLANES = 8
ROUTE_LANES = LANES
MOD_ROWS = 16

PROJ_ROWS = 512
SCAN_ROWS = 256
MIX_ROWS = 512

VMEM_LIMIT = 48 * 1024 * 1024


def _cparams(*sem):
    return pltpu.CompilerParams(dimension_semantics=sem, vmem_limit_bytes=VMEM_LIMIT)


def _split2(a):
    hi = a.astype(BF16)
    lo = (a - hi.astype(F32)).astype(BF16)
    return hi, lo


def _dot(a, b):
    return jnp.dot(a, b, preferred_element_type=F32)


def _dot_nt(a, b):
    return lax.dot_general(a, b, (((1,), (1,)), ((), ())), preferred_element_type=F32)


def _dot_tn(a, b):
    return lax.dot_general(a, b, (((0,), (0,)), ((), ())), preferred_element_type=F32)


def _dot3(a, b):
    ah, al = _split2(a)
    bh, bl = _split2(b)
    return _dot(ah, bh) + (_dot(al, bh) + _dot(ah, bl))


def _silu(x):
    return x / (1.0 + jnp.exp(-x))


def _rms(x):
    return x * lax.rsqrt(jnp.mean(x * x, axis=-1, keepdims=True) + EPS)


def _full(a):
    return pl.BlockSpec(a.shape, lambda *_: (0,) * a.ndim)


def _stream_inputs(xa, xb, n_a, tm, d):
    if xb is None:
        return [xa], [pl.BlockSpec((tm, d), lambda i: (i, 0))]
    return [xa, xb], [pl.BlockSpec((tm, d), lambda i: (jnp.minimum(i, n_a - 1), 0)),
                      pl.BlockSpec((tm, d), lambda i: (jnp.maximum(i - n_a, 0), 0))]


def _mod_row(d, tiles_per_row, n_rows, j):
    return pl.BlockSpec((None, 1, d), lambda i: (jnp.minimum(i // tiles_per_row, n_rows - 1), 0, j))


def _store_token_tiles_cols(ref, x, j0, nd):
    rows, w = x.shape
    for j in range(w // LANES):
        ref[pl.ds(j0 + j, rows, stride=nd), :] = x[:, j * LANES:(j + 1) * LANES]


def _store_token_tiles(ref, x):
    _store_token_tiles_cols(ref, x, 0, x.shape[1] // LANES)


def _load_token_tiles(ref, rows, nd):
    return jnp.concatenate([ref[pl.ds(j, rows, stride=nd), :] for j in range(nd)], axis=-1)


def _mod_kernel(c_ref, w_ref, b_ref, o_ref):
    o_ref[...] = _dot3(_silu(c_ref[...]), w_ref[...]) + b_ref[...]


def _modulation(cc, w_mod, b_mod):
    depth, d, n = w_mod.shape
    tn = 1536
    return pl.pallas_call(
        _mod_kernel,
        out_shape=jax.ShapeDtypeStruct((depth, MOD_ROWS, n), F32),
        grid=(depth, n // tn),
        in_specs=[
            pl.BlockSpec((MOD_ROWS, d), lambda l, j: (0, 0)),
            pl.BlockSpec((None, d, tn), lambda l, j: (l, 0, j)),
            pl.BlockSpec((None, 1, tn), lambda l, j: (l, 0, j)),
        ],
        out_specs=pl.BlockSpec((None, MOD_ROWS, tn), lambda l, j: (l, 0, j)),
        compiler_params=_cparams("arbitrary", "arbitrary"),
        name="modulation",
    )(cc, w_mod, b_mod.reshape(depth, 1, n))


def _log_forget(z, lb, lb_is_zero):
    ls = jnp.minimum(z, 0.0) - jnp.log(1.0 + jnp.exp(-jnp.abs(z)))
    if lb_is_zero:
        return ls
    return jnp.maximum(jnp.log(lb + (1.0 - lb) * jnp.exp(ls)), ls)


def _stream_tile(xa_ref, xb_ref, n_a_tiles):
    if xb_ref is None:
        return xa_ref[...]
    return jnp.where(pl.program_id(0) < n_a_tiles, xa_ref[...], xb_ref[...])


def _moe_residual(x_ref, y0_ref, y1_ref, rt_ref, g2_ref):
    w = rt_ref[...]
    tm, d = x_ref.shape
    nd = d // LANES
    f = w[:, 2:3] * _load_token_tiles(y0_ref, tm, nd) + w[:, 3:4] * _load_token_tiles(y1_ref, tm, nd)
    return x_ref[...] + g2_ref[...] * f


def _inproj_kernel(*refs, lb_is_zero, n_a_tiles, n_x):
    if n_x == 5:
        x = _moe_residual(*refs[:5])
    else:
        x = _stream_tile(refs[0], refs[1] if n_x == 2 else None, n_a_tiles)
    (sh_ref, sc_ref, n1_ref, w_ref, lb_ref, sgn_ref,
     q_ref, i_ref, lff_ref, lfb_ref, sg_ref, gu_ref, vn_ref, *xo_ref) = refs[n_x:]
    if xo_ref:
        xo_ref[0][...] = x
    h = _rms(x) * n1_ref[...]
    hb = (h * (1.0 + sc_ref[...]) + sh_ref[...]).astype(BF16)

    def proj(j):
        return _dot(hb, w_ref[:, j * HW:(j + 1) * HW])

    q_ref[...] = proj(0).astype(BF16)
    lff_ref[...] = _log_forget(proj(1), lb_ref[0:1, :], lb_is_zero)
    lfb_ref[...] = _log_forget(proj(2), lb_ref[1:2, :], lb_is_zero)
    i_ref[...] = proj(3).astype(BF16)
    sg_ref[...] = _silu(proj(4)).astype(BF16)
    gu_ref[...] = jax.nn.gelu(proj(5)).astype(BF16)
    vn_ref[...] = (_rms(jax.nn.gelu(proj(6))) * sgn_ref[...]).astype(BF16)


def _inproj(xa, xb, t_a, t_b, mod3, tokens_per_mod_row, n1, w_in, lb, sgu_gain, lb_is_zero, moe=None):
    d = xa.shape[1]
    tm = PROJ_ROWS
    n_a, n_b = t_a // tm, t_b // tm
    t = t_a + t_b
    nd = d // LANES
    tok = pl.BlockSpec((tm, HW), lambda i: (i, 0))
    wide = pl.BlockSpec((tm, d), lambda i: (i, 0))
    row = functools.partial(_mod_row, d, tokens_per_mod_row // tm, mod3.shape[0])
    out_shape = [jax.ShapeDtypeStruct((t, HW), dt) for dt in (BF16, BF16, F32, F32, BF16, BF16, BF16)]
    out_specs = [tok] * 7
    if moe is None:
        x_in, x_spec = _stream_inputs(xa, xb, n_a, tm, d)
    else:
        y2, ttot, route, mod3_prev = moe
        b1 = ttot // tm
        x_in = [xa, y2, y2, route, mod3_prev]
        x_spec = [wide,
                  pl.BlockSpec((tm * nd, LANES), lambda i: (i, 0)),
                  pl.BlockSpec((tm * nd, LANES), lambda i: (i + b1, 0)),
                  pl.BlockSpec((tm, ROUTE_LANES), lambda i: (i, 0)),
                  _mod_row(d, tokens_per_mod_row // tm, mod3_prev.shape[0], 5)]
        out_shape.append(jax.ShapeDtypeStruct((t, d), F32))
        out_specs.append(wide)
    return pl.pallas_call(
        functools.partial(_inproj_kernel, lb_is_zero=lb_is_zero, n_a_tiles=n_a, n_x=len(x_in)),
        out_shape=out_shape,
        grid=(n_a + n_b,),
        in_specs=x_spec + [row(0), row(1), _full(n1), _full(w_in), _full(lb), _full(sgu_gain)],
        out_specs=out_specs,
        compiler_params=_cparams("arbitrary"),
        name="inproj",
    )(*x_in, mod3, mod3, n1, w_in, lb, sgu_gain)


def _chunk_cumsum(x, reverse):
    c, w = x.shape
    g = c // SUBLANES
    x3 = x.reshape(g, SUBLANES, w)
    sub = lax.broadcasted_iota(jnp.int32, x3.shape, 1)
    for s in (1, 2, 4):
        if reverse:
            x3 = x3 + jnp.where(sub < SUBLANES - s, pltpu.roll(x3, SUBLANES - s, axis=1), 0.0)
        else:
            x3 = x3 + jnp.where(sub >= s, pltpu.roll(x3, s, axis=1), 0.0)
    edge = 0 if reverse else SUBLANES - 1
    tot = x3[:, edge:edge + 1, :]
    offs = [None] * g
    acc = jnp.zeros((1, w), F32)
    for gi in (reversed(range(g)) if reverse else range(g)):
        offs[gi] = acc
        acc = acc + tot[gi]
    x3 = x3 + jnp.stack(offs, axis=0)
    return x3.reshape(c, w)


def _scan_chunk(q_ref, i_ref, lf_ref, o_ref, st_ref, r0, reverse):
    c = HGRN_CHUNK
    rows = lax.broadcasted_iota(jnp.int32, (c, c), 0)
    cols = lax.broadcasted_iota(jnp.int32, (c, c), 1)
    incl = (cols >= rows) if reverse else (cols <= rows)
    ref_row = c // 2 if reverse else c // 2 - 1
    tot_row = 0 if reverse else c - 1
    lf = lf_ref[pl.ds(r0, c), :] * LOG2_E
    cum = _chunk_cumsum(lf, reverse)
    ref = cum[ref_row:ref_row + 1, :]
    tot = cum[tot_row:tot_row + 1, :]
    k = 1.0 - jnp.exp2(lf)
    qf = q_ref[pl.ds(r0, c), :].astype(F32)
    iv = i_ref[pl.ds(r0, c), :]
    q_in = (qf * jnp.exp2(cum - ref)).astype(BF16)
    k_in = (k * jnp.exp2(ref - cum)).astype(BF16)
    k_st = (k * jnp.exp2(tot - cum)).astype(BF16)
    q_st = (qf * jnp.exp2(cum)).astype(BF16)
    dec = jnp.exp2(tot)
    for h in range(HEADS):
        sl = slice(h * HEAD_DIM, (h + 1) * HEAD_DIM)
        sc = _dot_nt(q_in[:, sl], k_in[:, sl])
        sc = jnp.where(incl, sc, 0.0).astype(BF16)
        st = st_ref[h]
        o_ref[pl.ds(r0, c), sl] = _dot(sc, iv[:, sl]) + _dot_nt(q_st[:, sl], st.astype(BF16))
        st_ref[h] = st * dec[:, sl] + _dot_tn(iv[:, sl], k_st[:, sl])


def _hgrn_kernel(qf_ref, if_ref, lff_ref, qb_ref, ib_ref, lfb_ref, of_ref, ob_ref, stf_ref, stb_ref, *, tt):
    @pl.when(pl.program_id(1) == 0)
    def _():
        stf_ref[...] = jnp.zeros_like(stf_ref)
        stb_ref[...] = jnp.zeros_like(stb_ref)

    nchunks = tt // HGRN_CHUNK
    for ci in range(nchunks):
        _scan_chunk(qf_ref, if_ref, lff_ref, of_ref, stf_ref, ci * HGRN_CHUNK, False)
        _scan_chunk(qb_ref, ib_ref, lfb_ref, ob_ref, stb_ref, (nchunks - 1 - ci) * HGRN_CHUNK, True)


def _hgrn(q, i, lf_f, lf_b, batch, seq, ctx_len):
    t = q.shape[0]
    tt = SCAN_ROWS
    nt, nc = seq // tt, ctx_len // tt
    ctx0 = batch * nt

    def fwd(b, s):
        return (jnp.where(s < nc, ctx0 + b * nc + s, b * nt + (s - nc)), 0)

    def bwd(b, s):
        return (jnp.where(s < nc, ctx0 + b * nc + (nc - 1 - s), b * nt + (nt - 1 - (s - nc))), 0)

    tf, tb = pl.BlockSpec((tt, HW), fwd), pl.BlockSpec((tt, HW), bwd)
    return pl.pallas_call(
        functools.partial(_hgrn_kernel, tt=tt),
        out_shape=[jax.ShapeDtypeStruct((t, HW), F32)] * 2,
        grid=(batch, nc + nt),
        in_specs=[tf, tf, tf, tb, tb, tb],
        out_specs=[tf, tb],
        scratch_shapes=[pltpu.VMEM((HEADS, HEAD_DIM, HEAD_DIM), F32)] * 2,
        compiler_params=_cparams("arbitrary", "arbitrary"),
        name="hgrn",
    )(q, i, lf_f, q, i, lf_b)


def _route(logits):
    lane = lax.broadcasted_iota(jnp.int32, logits.shape, 1).astype(F32)
    neg = -jnp.inf
    is_group = lane < N_GROUPS
    gl = jnp.where(is_group, logits, neg)
    gmax = jnp.max(gl, axis=-1, keepdims=True)
    g_sel = jnp.min(jnp.where(gl == gmax, lane, float(ROUTE_LANES)), axis=-1, keepdims=True)
    den = jnp.sum(jnp.where(is_group, jnp.exp(logits - gmax), 0.0), axis=-1, keepdims=True)
    p_sel = 1.0 / den
    first = N_GROUPS + EXPERTS_PER_GROUP * g_sel
    el = jnp.where((lane >= first) & (lane < first + EXPERTS_PER_GROUP), logits, neg)
    t1 = jnp.max(el, axis=-1, keepdims=True)
    i1 = jnp.min(jnp.where(el == t1, lane, float(ROUTE_LANES)), axis=-1, keepdims=True)
    el2 = jnp.where(lane == i1, neg, el)
    t2 = jnp.max(el2, axis=-1, keepdims=True)
    i2 = jnp.min(jnp.where(el2 == t2, lane, float(ROUTE_LANES)), axis=-1, keepdims=True)
    e2 = jnp.exp(t2 - t1)
    w1 = p_sel / (1.0 + e2)
    w2 = p_sel * e2 / (1.0 + e2)
    rec = jnp.where(lane == 0.0, i1 - N_GROUPS, 0.0)
    rec = jnp.where(lane == 1.0, i2 - N_GROUPS, rec)
    rec = jnp.where(lane == 2.0, w1, rec)
    return jnp.where(lane == 3.0, w2, rec)


def _mixer_kernel(of_ref, ob_ref, sg_ref, gu_ref, vn_ref, *refs, tm, n_a_tiles, two_inputs):
    xa_ref, xb_ref = (refs[0], refs[1]) if two_inputs else (refs[0], None)
    (hgain_ref, ws_ref, bs_ref, wo_ref, g1_ref, n2_ref, sh2_ref, sc2_ref, wrh_ref, wrl_ref, br_ref,
     xo_ref, h2_ref, rt_ref, rtt_ref, cat_ref) = refs[2 if two_inputs else 1:]
    o = of_ref[...] + ob_ref[...]
    for h in range(HEADS):
        sl = slice(h * HEAD_DIM, (h + 1) * HEAD_DIM)
        hg = _rms(o[:, sl]) * hgain_ref[:, sl] * sg_ref[:, sl].astype(F32)
        cat_ref[:, sl] = hg.astype(BF16)
    for cc in range(tm // SGU_CHUNK):
        rows = slice(cc * SGU_CHUNK, (cc + 1) * SGU_CHUNK)
        for h in range(HEADS):
            sl = slice(h * HEAD_DIM, (h + 1) * HEAD_DIM)
            mixed = _dot(ws_ref[h], vn_ref[rows, sl]) + bs_ref[h]
            cat_ref[rows, HW + h * HEAD_DIM:HW + (h + 1) * HEAD_DIM] = (
                gu_ref[rows, sl].astype(F32) * mixed).astype(BF16)
    xn = _stream_tile(xa_ref, xb_ref, n_a_tiles) + g1_ref[...] * _dot(cat_ref[...], wo_ref[...])
    xo_ref[...] = xn
    h2 = _rms(xn) * n2_ref[...]
    h2 = h2 * (1.0 + sc2_ref[...]) + sh2_ref[...]
    _store_token_tiles(h2_ref, h2)
    hi, lo = _split2(h2)
    logits = _dot(hi, wrh_ref[...]) + (_dot(lo, wrh_ref[...]) + _dot(hi, wrl_ref[...])) + br_ref[...]
    rec = _route(logits)
    rt_ref[...] = rec
    rtt_ref[...] = rec.T[:SUBLANES, :]


def _mixer(o_f, o_b, sg, gu, vn, xa, xb, t_a, t_b, mod3, tokens_per_mod_row, hgain, w_s, b_s, w_out, n2,
           wr_hi, wr_lo, br):
    d = xa.shape[1]
    tm = MIX_ROWS
    n_a, n_b = t_a // tm, t_b // tm
    t = t_a + t_b
    nd = d // LANES
    row = functools.partial(_mod_row, d, tokens_per_mod_row // tm, mod3.shape[0])
    tok = pl.BlockSpec((tm, HW), lambda i: (i, 0))
    wide = pl.BlockSpec((tm, d), lambda i: (i, 0))
    x_in, x_spec = _stream_inputs(xa, xb, n_a, tm, d)
    return pl.pallas_call(
        functools.partial(_mixer_kernel, tm=tm, n_a_tiles=n_a, two_inputs=xb is not None),
        out_shape=[jax.ShapeDtypeStruct((t, d), F32), jax.ShapeDtypeStruct((t * nd, LANES), F32),
                   jax.ShapeDtypeStruct((t, ROUTE_LANES), F32), jax.ShapeDtypeStruct((SUBLANES, t), F32)],
        grid=(n_a + n_b,),
        in_specs=[tok, tok, tok, tok, tok] + x_spec + [_full(hgain), _full(w_s), _full(b_s), _full(w_out),
                  row(2), _full(n2), row(3), row(4), _full(wr_hi), _full(wr_lo), _full(br)],
        out_specs=[wide, pl.BlockSpec((tm * nd, LANES), lambda i: (i, 0)),
                   pl.BlockSpec((tm, ROUTE_LANES), lambda i: (i, 0)), pl.BlockSpec((SUBLANES, tm), lambda i: (0, i))],
        scratch_shapes=[pltpu.VMEM((tm, 2 * HW), BF16)],
        compiler_params=_cparams("arbitrary"),
        name="mixer",
    )(o_f, o_b, sg, gu, vn, *x_in, hgain, w_s, b_s, w_out, mod3, n2, mod3, mod3, wr_hi, wr_lo, br)


MOE_ROWS = 256
MOE_LAG = 3
RING = 3
DMA_GROUPS = 8


def _moe_kernel(last_ref, sexp_ref, tab_hbm, h_hbm, wg_ref, wu_ref, wd_ref, y_hbm,
                idx_ref, xbuf, ybuf, zbuf, xb_ref, hm_ref, wgb, wub, wdb, sem_idx, sem_g, sem_s, sem_z):
    i = pl.program_id(0)
    last = last_ref[0]
    bm = MOE_ROWS
    de = wgb.shape[1]
    d = wgb.shape[0]
    nd = d // LANES

    def idx_copy(step, slot):
        return pltpu.make_async_copy(tab_hbm.at[step], idx_ref.at[slot], sem_idx.at[slot])

    def gathered(slot):
        return pltpu.make_async_copy(h_hbm.at[pl.ds(0, bm * nd)], xbuf.at[slot], sem_g.at[slot])

    def scattered(slot):
        return pltpu.make_async_copy(ybuf.at[slot], y_hbm.at[pl.ds(0, bm * nd)], sem_s.at[slot])

    def step(k):
        gslot = k
        cslot = (k + 1) % RING
        sslot = k

        if k == 0:
            @pl.when(i == 0)
            def _():
                xbuf[...] = jnp.zeros_like(xbuf)
                ybuf[...] = jnp.zeros_like(ybuf)
                zbuf[...] = jnp.zeros_like(zbuf)
                idx_copy(0, 0).start()

        idx_copy(i, k).wait()

        @pl.when(i < last)
        def _():
            idx_copy(i + 1, (k + 1) % RING).start()

        @pl.when(i >= 2)
        def _():
            gathered(cslot).wait()
            scattered(cslot).wait()

        @pl.when((i == 0) | (sexp_ref[i] != sexp_ref[jnp.maximum(i - 1, 0)]))
        def _():
            wgb[...] = wg_ref[...].astype(BF16)
            wub[...] = wu_ref[...].astype(BF16)
            wdb[...] = wd_ref[...].astype(BF16)

        per = bm // (DMA_GROUPS // 2)

        def scatter_group(g):
            for r in range(g * per, (g + 1) * per):
                dst = pl.multiple_of(idx_ref[k, 1, r], nd)
                pltpu.make_async_copy(ybuf.at[sslot, pl.ds(r * nd, nd)], y_hbm.at[pl.ds(dst, nd)],
                                      sem_s.at[sslot]).start(priority=r % 2)

        def gather_group(g):
            for r in range(g * per, (g + 1) * per):
                src = pl.multiple_of(idx_ref[k, 0, r], nd)
                pltpu.make_async_copy(h_hbm.at[pl.ds(src, nd)], xbuf.at[gslot, pl.ds(r * nd, nd)],
                                      sem_g.at[gslot]).start(priority=r % 2)

        dma_groups = [functools.partial(scatter_group, g) for g in range(DMA_GROUPS // 2)]
        dma_groups += [functools.partial(gather_group, g) for g in range(DMA_GROUPS // 2)]

        def issue_some():
            if dma_groups:
                dma_groups.pop(0)()

        xsrc = xbuf.at[cslot]
        for j in range(nd):
            xb_ref[:, j * LANES:(j + 1) * LANES] = xsrc[pl.ds(j, bm, stride=nd), :].astype(BF16)
        nh = 2
        for j in range(nh):
            cs = slice(j * de // nh, (j + 1) * de // nh)
            issue_some()
            gate = _dot(xb_ref[...], wgb[:, cs])
            issue_some()
            hm_ref[:, cs] = (_silu(gate) * _dot(xb_ref[...], wub[:, cs])).astype(BF16)
        ydst = ybuf.at[cslot]
        n_down = min(4, nd)
        for j in range(n_down):
            issue_some()
            _store_token_tiles_cols(ydst, _dot(hm_ref[...], wdb[:, j * d // n_down:(j + 1) * d // n_down]),
                                    j * nd // n_down, nd)
        while dma_groups:
            issue_some()

        @pl.when(i == last)
        def _():
            gathered(gslot).wait()
            gathered((k + 2) % RING).wait()
            scattered(sslot).wait()
            scattered((k + 2) % RING).wait()

    for k in range(RING):
        pl.when((i <= last) & (i % RING == k))(functools.partial(step, k))

    @pl.when(i > last)
    def _():
        fill = pltpu.make_async_copy(zbuf, y_hbm.at[pl.ds((i - MOE_LAG) * (bm * nd), bm * nd)], sem_z.at[0])
        fill.start()
        fill.wait()


def _moe(h2, tab, last, sexp, n_out_rows, w_gate, w_up, w_down, layer):
    d, de = w_gate.shape[-2:]
    nd = d // LANES
    n_steps = tab.shape[0]
    grid_spec = pltpu.PrefetchScalarGridSpec(
        num_scalar_prefetch=2,
        grid=(n_steps,),
        in_specs=[
            pl.BlockSpec(memory_space=pl.ANY),
            pl.BlockSpec(memory_space=pl.ANY),
            pl.BlockSpec((None, None, d, de), lambda i, la, se: (layer, se[i], 0, 0)),
            pl.BlockSpec((None, None, d, de), lambda i, la, se: (layer, se[i], 0, 0)),
            pl.BlockSpec((None, None, de, d), lambda i, la, se: (layer, se[i], 0, 0)),
        ],
        out_specs=pl.BlockSpec(memory_space=pl.ANY),
        scratch_shapes=[
            pltpu.SMEM((RING, 2, MOE_ROWS), jnp.int32),
            pltpu.VMEM((RING, MOE_ROWS * nd, LANES), F32),
            pltpu.VMEM((RING, MOE_ROWS * nd, LANES), F32),
            pltpu.VMEM((MOE_ROWS * nd, LANES), F32),
            pltpu.VMEM((MOE_ROWS, d), BF16),
            pltpu.VMEM((MOE_ROWS, de), BF16),
            pltpu.VMEM((d, de), BF16),
            pltpu.VMEM((d, de), BF16),
            pltpu.VMEM((de, d), BF16),
            pltpu.SemaphoreType.DMA((RING,)),
            pltpu.SemaphoreType.DMA((RING,)),
            pltpu.SemaphoreType.DMA((RING,)),
            pltpu.SemaphoreType.DMA((1,)),
        ],
    )
    return pl.pallas_call(
        _moe_kernel,
        out_shape=jax.ShapeDtypeStruct((n_out_rows * nd, LANES), F32),
        grid_spec=grid_spec,
        compiler_params=_cparams("arbitrary"),
        name="moe",
    )(last, sexp, tab, h2, w_gate, w_up, w_down)


def _dispatch_tables(expert_flat, ttot, nd):
    bm = MOE_ROWS
    n_slots = expert_flat.shape[0]
    n_blocks = -(-n_slots // bm) + N_EXPERTS
    n_steps = n_blocks + MOE_LAG
    n_main = 1 << (n_slots.bit_length() - 1)
    pieces = []
    for lo, n in ((0, n_main), (n_main, n_slots - n_main)):
        if n:
            e = expert_flat[lo:lo + n]
            _, order = lax.sort_key_val(e, lax.iota(jnp.int32, n))
            cnt = jnp.sum(e[:, None] == jnp.arange(N_EXPERTS, dtype=jnp.int32)[None, :], axis=0, dtype=jnp.int32)
            pieces.append((order + lo, cnt))
    counts = sum(cnt for _, cnt in pieces)
    padded = (counts + bm - 1) // bm * bm
    pad_end = jnp.cumsum(padded)
    pad_start = pad_end - padded
    start = jnp.cumsum(counts) - counts
    blk_row0 = jnp.arange(n_blocks, dtype=jnp.int32) * bm
    bexp = jnp.minimum(jnp.sum(pad_end[None, :] <= blk_row0[:, None], axis=1), N_EXPERTS - 1).astype(jnp.int32)
    lane = jnp.arange(bm, dtype=jnp.int32)[None, :]
    off = (blk_row0 - pad_start[bexp])[:, None] + lane
    valid = off < counts[bexp][:, None]
    src, rem, base = jnp.zeros_like(off), off, 0
    for order, cnt in pieces:
        first = (jnp.cumsum(cnt) - cnt)[bexp][:, None]
        here = (rem >= 0) & (rem < cnt[bexp][:, None])
        src = jnp.where(here, base + first + rem, src)
        rem = rem - cnt[bexp][:, None]
        base += order.shape[0]
    slot = jnp.concatenate([order for order, _ in pieces])[jnp.clip(src, 0, n_slots - 1)]
    pad_rank = blk_row0[:, None] + lane - (start[bexp] + counts[bexp])[:, None]
    gsrc = jnp.where(valid, slot % ttot, 0)
    sdst = jnp.where(valid, slot, n_slots + pad_rank)
    spare = n_blocks * bm + jnp.arange(MOE_LAG * bm, dtype=jnp.int32).reshape(MOE_LAG, bm)
    gtab = jnp.concatenate([gsrc, jnp.zeros((MOE_LAG, bm), jnp.int32)], axis=0)
    stab = jnp.concatenate([spare, sdst], axis=0)
    tab = (jnp.stack([gtab, stab], axis=1) * nd).astype(jnp.int32)
    n_used = jnp.sum(padded) // bm
    last = (n_used + MOE_LAG - 1).astype(jnp.int32).reshape(1)
    sexp = bexp[jnp.clip(jnp.arange(n_steps) - (MOE_LAG - 1), 0, n_blocks - 1)]
    return tab, last, sexp, n_steps * bm


def _combine_kernel(x_ref, y0_ref, y1_ref, rt_ref, g2_ref, fn_ref, o_ref, *, final):
    xn = _moe_residual(x_ref, y0_ref, y1_ref, rt_ref, g2_ref)
    if final:
        xn = _rms(xn) * fn_ref[...]
    o_ref[...] = xn


def _combine(x2d, y2, t, ttot, route, mod3, tokens_per_mod_row, final_norm, final):
    d = x2d.shape[1]
    tm = MIX_ROWS
    b1 = ttot // tm
    nd = d // LANES
    return pl.pallas_call(
        functools.partial(_combine_kernel, final=final),
        out_shape=jax.ShapeDtypeStruct((t, d), F32),
        grid=(t // tm,),
        in_specs=[pl.BlockSpec((tm, d), lambda i: (i, 0)),
                  pl.BlockSpec((tm * nd, LANES), lambda i: (i, 0)),
                  pl.BlockSpec((tm * nd, LANES), lambda i: (i + b1, 0)),
                  pl.BlockSpec((tm, ROUTE_LANES), lambda i: (i, 0)),
                  _mod_row(d, tokens_per_mod_row // tm, mod3.shape[0], 5),
                  _full(final_norm)],
        out_specs=pl.BlockSpec((tm, d), lambda i: (i, 0)),
        compiler_params=_cparams("arbitrary"),
        name="combine",
    )(x2d, y2, y2, route, mod3, final_norm)


def kernel(x, c, ctx, c_ctx, norm1, norm2, w_mod, b_mod, w_in, lb_logits, hgrn_norm, sgu_norm, sgu_w, sgu_b,
           w_out, w_group, b_group, w_router, b_router, w_gate, w_up, w_down, final_norm):
    b, l, d = x.shape
    lc = ctx.shape[1]
    depth = w_mod.shape[0]
    t_lat, t_ctx = b * l, b * lc
    nd = d // LANES

    lb_cum = jnp.cumsum(jax.nn.softmax(lb_logits.astype(F32), axis=0), axis=0)
    lower_bound = jnp.maximum(lb_cum - lb_cum[0:1], 0.0)

    cc = jnp.zeros((MOD_ROWS, d), F32).at[:b].set(c).at[b].set(c_ctx)
    mod = _modulation(cc, w_mod, b_mod)

    w_route = jnp.concatenate([w_group, w_router], axis=-1)
    w_route = jnp.pad(w_route, ((0, 0), (0, 0), (0, ROUTE_LANES - w_route.shape[-1])))
    wr_hi = w_route.astype(BF16)
    wr_lo = (w_route - wr_hi.astype(F32)).astype(BF16)
    b_route = jnp.concatenate([b_group, b_router], axis=-1)
    b_route = jnp.pad(b_route, ((0, 0), (0, ROUTE_LANES - b_route.shape[-1])))[:, None, :]
    b_s = jnp.broadcast_to(sgu_b[..., None], sgu_b.shape + (HEAD_DIM,)).astype(F32)

    w_in_b, w_out_b, sgu_w_b = w_in.astype(BF16), w_out.astype(BF16), sgu_w.astype(BF16)

    xa, xb, t_a, t_b = x.reshape(t_lat, d), ctx.reshape(t_ctx, d), t_lat, t_ctx
    fn = final_norm.reshape(1, d)
    moe = None

    for layer in range(depth):
        last = layer == depth - 1
        mod3 = mod[layer, :b + 1].reshape(b + 1, 1, N_MOD * d)
        n1 = norm1[layer].reshape(1, d)
        n2 = norm2[layer].reshape(1, d)
        sgn = sgu_norm[layer].reshape(1, HW)
        hgain = hgrn_norm[layer].reshape(1, HW)

        q, i, lf_f, lf_b, sg, gu, vn, *x_new = _inproj(xa, xb, t_a, t_b, mod3, l, n1, w_in_b[layer],
                                                       lower_bound[layer], sgn, layer == 0, moe)
        if x_new:
            xa = x_new[0]
        o_f, o_b = _hgrn(q, i, lf_f, lf_b, b, l, lc)

        if last:
            xb, t_a, t_b = None, t_lat, 0
        ttot = t_a + t_b
        xs, h2, route, route_t = _mixer(o_f, o_b, sg, gu, vn, xa, xb, t_a, t_b, mod3, l, hgain, sgu_w_b[layer],
                                        b_s[layer], w_out_b[layer], n2, wr_hi[layer], wr_lo[layer],
                                        b_route[layer])

        expert_flat = route_t[:TOP_K].astype(jnp.int32).reshape(-1)
        tab, last_step, sexp, n_out_rows = _dispatch_tables(expert_flat, ttot, nd)
        y2 = _moe(h2, tab, last_step, sexp, n_out_rows, w_gate, w_up, w_down, layer)

        xa, xb, t_a, t_b = xs, None, ttot, 0
        moe = (y2, ttot, route, mod3)

    return _combine(xs, y2, ttot, ttot, route, mod3, l, fn, final=True).reshape(b, l, d)
```

```python
import functools

import jax
import jax.numpy as jnp
from jax import lax
from jax.experimental import pallas as pl
from jax.experimental.pallas import tpu as pltpu

F32 = jnp.float32
BF16 = jnp.bfloat16

EPS = 1e-6
LOG2_E = 1.4426950408889634
HEADS = 4
HEAD_DIM = 128
HW = HEADS * HEAD_DIM
HGRN_CHUNK = 64
SGU_CHUNK = 128
N_GROUPS = 4
EXPERTS_PER_GROUP = 8
N_EXPERTS = N_GROUPS * EXPERTS_PER_GROUP
TOP_K = 2
N_MOD = 6
LANES = 128
SUBLANES = 8
ROUTE_LANES = LANES
MOD_ROWS = 16

PROJ_ROWS = 512
SCAN_ROWS = 256
MIX_ROWS = 512

VMEM_LIMIT = 48 * 1024 * 1024


def _cparams(*sem):
    return pltpu.CompilerParams(dimension_semantics=sem, vmem_limit_bytes=VMEM_LIMIT)


def _split2(a):
    hi = a.astype(BF16)
    lo = (a - hi.astype(F32)).astype(BF16)
    return hi, lo


def _dot(a, b):
    return jnp.dot(a, b, preferred_element_type=F32)


def _dot_nt(a, b):
    return lax.dot_general(a, b, (((1,), (1,)), ((), ())), preferred_element_type=F32)


def _dot_tn(a, b):
    return lax.dot_general(a, b, (((0,), (0,)), ((), ())), preferred_element_type=F32)


def _dot3(a, b):
    ah, al = _split2(a)
    bh, bl = _split2(b)
    return _dot(ah, bh) + (_dot(al, bh) + _dot(ah, bl))


def _silu(x):
    return x / (1.0 + jnp.exp(-x))


def _rms(x):
    return x * lax.rsqrt(jnp.mean(x * x, axis=-1, keepdims=True) + EPS)


def _full(a):
    return pl.BlockSpec(a.shape, lambda *_: (0,) * a.ndim)


def _stream_inputs(xa, xb, n_a, tm, d):
    if xb is None:
        return [xa], [pl.BlockSpec((tm, d), lambda i: (i, 0))]
    return [xa, xb], [pl.BlockSpec((tm, d), lambda i: (jnp.minimum(i, n_a - 1), 0)),
                      pl.BlockSpec((tm, d), lambda i: (jnp.maximum(i - n_a, 0), 0))]


def _mod_row(d, tiles_per_row, n_rows, j):
    return pl.BlockSpec((None, 1, d), lambda i: (jnp.minimum(i // tiles_per_row, n_rows - 1), 0, j))


def _store_token_tiles_cols(ref, x, j0, nd):
    rows, w = x.shape
    for j in range(w // LANES):
        ref[pl.ds(j0 + j, rows, stride=nd), :] = x[:, j * LANES:(j + 1) * LANES]


def _store_token_tiles(ref, x):
    _store_token_tiles_cols(ref, x, 0, x.shape[1] // LANES)


def _load_token_tiles(ref, rows, nd):
    return jnp.concatenate([ref[pl.ds(j, rows, stride=nd), :] for j in range(nd)], axis=-1)


def _mod_kernel(c_ref, w_ref, b_ref, o_ref):
    o_ref[...] = _dot3(_silu(c_ref[...]), w_ref[...]) + b_ref[...]


def _modulation(cc, w_mod, b_mod):
    depth, d, n = w_mod.shape
    tn = 1536
    return pl.pallas_call(
        _mod_kernel,
        out_shape=jax.ShapeDtypeStruct((depth, MOD_ROWS, n), F32),
        grid=(depth, n // tn),
        in_specs=[
            pl.BlockSpec((MOD_ROWS, d), lambda l, j: (0, 0)),
            pl.BlockSpec((None, d, tn), lambda l, j: (l, 0, j)),
            pl.BlockSpec((None, 1, tn), lambda l, j: (l, 0, j)),
        ],
        out_specs=pl.BlockSpec((None, MOD_ROWS, tn), lambda l, j: (l, 0, j)),
        compiler_params=_cparams("arbitrary", "arbitrary"),
        name="modulation",
    )(cc, w_mod, b_mod.reshape(depth, 1, n))


def _log_forget(z, lb, lb_is_zero):
    ls = jnp.minimum(z, 0.0) - jnp.log(1.0 + jnp.exp(-jnp.abs(z)))
    if lb_is_zero:
        return ls
    return jnp.maximum(jnp.log(lb + (1.0 - lb) * jnp.exp(ls)), ls)


def _stream_tile(xa_ref, xb_ref, n_a_tiles):
    if xb_ref is None:
        return xa_ref[...]
    return jnp.where(pl.program_id(0) < n_a_tiles, xa_ref[...], xb_ref[...])


def _moe_residual(x_ref, y0_ref, y1_ref, rt_ref, g2_ref):
    w = rt_ref[...]
    tm, d = x_ref.shape
    nd = d // LANES
    f = w[:, 2:3] * _load_token_tiles(y0_ref, tm, nd) + w[:, 3:4] * _load_token_tiles(y1_ref, tm, nd)
    return x_ref[...] + g2_ref[...] * f


def _inproj_kernel(*refs, lb_is_zero, n_a_tiles, n_x):
    if n_x == 5:
        x = _moe_residual(*refs[:5])
    else:
        x = _stream_tile(refs[0], refs[1] if n_x == 2 else None, n_a_tiles)
    (sh_ref, sc_ref, n1_ref, w_ref, lb_ref, sgn_ref,
     q_ref, i_ref, lff_ref, lfb_ref, sg_ref, gu_ref, vn_ref, *xo_ref) = refs[n_x:]
    if xo_ref:
        xo_ref[0][...] = x
    h = _rms(x) * n1_ref[...]
    hb = (h * (1.0 + sc_ref[...]) + sh_ref[...]).astype(BF16)

    def proj(j):
        return _dot(hb, w_ref[:, j * HW:(j + 1) * HW])

    q_ref[...] = proj(0).astype(BF16)
    lff_ref[...] = _log_forget(proj(1), lb_ref[0:1, :], lb_is_zero)
    lfb_ref[...] = _log_forget(proj(2), lb_ref[1:2, :], lb_is_zero)
    i_ref[...] = proj(3).astype(BF16)
    sg_ref[...] = _silu(proj(4)).astype(BF16)
    gu_ref[...] = jax.nn.gelu(proj(5)).astype(BF16)
    vn_ref[...] = (_rms(jax.nn.gelu(proj(6))) * sgn_ref[...]).astype(BF16)


def _inproj(xa, xb, t_a, t_b, mod3, tokens_per_mod_row, n1, w_in, lb, sgu_gain, lb_is_zero, moe=None):
    d = xa.shape[1]
    tm = PROJ_ROWS
    n_a, n_b = t_a // tm, t_b // tm
    t = t_a + t_b
    nd = d // LANES
    tok = pl.BlockSpec((tm, HW), lambda i: (i, 0))
    wide = pl.BlockSpec((tm, d), lambda i: (i, 0))
    row = functools.partial(_mod_row, d, tokens_per_mod_row // tm, mod3.shape[0])
    out_shape = [jax.ShapeDtypeStruct((t, HW), dt) for dt in (BF16, BF16, F32, F32, BF16, BF16, BF16)]
    out_specs = [tok] * 7
    if moe is None:
        x_in, x_spec = _stream_inputs(xa, xb, n_a, tm, d)
    else:
        y2, ttot, route, mod3_prev = moe
        b1 = ttot // tm
        x_in = [xa, y2, y2, route, mod3_prev]
        x_spec = [wide,
                  pl.BlockSpec((tm * nd, LANES), lambda i: (i, 0)),
                  pl.BlockSpec((tm * nd, LANES), lambda i: (i + b1, 0)),
                  pl.BlockSpec((tm, ROUTE_LANES), lambda i: (i, 0)),
                  _mod_row(d, tokens_per_mod_row // tm, mod3_prev.shape[0], 5)]
        out_shape.append(jax.ShapeDtypeStruct((t, d), F32))
        out_specs.append(wide)
    return pl.pallas_call(
        functools.partial(_inproj_kernel, lb_is_zero=lb_is_zero, n_a_tiles=n_a, n_x=len(x_in)),
        out_shape=out_shape,
        grid=(n_a + n_b,),
        in_specs=x_spec + [row(0), row(1), _full(n1), _full(w_in), _full(lb), _full(sgu_gain)],
        out_specs=out_specs,
        compiler_params=_cparams("arbitrary"),
        name="inproj",
    )(*x_in, mod3, mod3, n1, w_in, lb, sgu_gain)


def _chunk_cumsum(x, reverse):
    c, w = x.shape
    g = c // SUBLANES
    x3 = x.reshape(g, SUBLANES, w)
    sub = lax.broadcasted_iota(jnp.int32, x3.shape, 1)
    for s in (1, 2, 4):
        if reverse:
            x3 = x3 + jnp.where(sub < SUBLANES - s, pltpu.roll(x3, SUBLANES - s, axis=1), 0.0)
        else:
            x3 = x3 + jnp.where(sub >= s, pltpu.roll(x3, s, axis=1), 0.0)
    edge = 0 if reverse else SUBLANES - 1
    tot = x3[:, edge:edge + 1, :]
    offs = [None] * g
    acc = jnp.zeros((1, w), F32)
    for gi in (reversed(range(g)) if reverse else range(g)):
        offs[gi] = acc
        acc = acc + tot[gi]
    x3 = x3 + jnp.stack(offs, axis=0)
    return x3.reshape(c, w)


def _scan_chunk(q_ref, i_ref, lf_ref, o_ref, st_ref, r0, reverse):
    c = HGRN_CHUNK
    rows = lax.broadcasted_iota(jnp.int32, (c, c), 0)
    cols = lax.broadcasted_iota(jnp.int32, (c, c), 1)
    incl = (cols >= rows) if reverse else (cols <= rows)
    ref_row = c // 2 if reverse else c // 2 - 1
    tot_row = 0 if reverse else c - 1
    lf = lf_ref[pl.ds(r0, c), :] * LOG2_E
    cum = _chunk_cumsum(lf, reverse)
    ref = cum[ref_row:ref_row + 1, :]
    tot = cum[tot_row:tot_row + 1, :]
    k = 1.0 - jnp.exp2(lf)
    qf = q_ref[pl.ds(r0, c), :].astype(F32)
    iv = i_ref[pl.ds(r0, c), :]
    q_in = (qf * jnp.exp2(cum - ref)).astype(BF16)
    k_in = (k * jnp.exp2(ref - cum)).astype(BF16)
    k_st = (k * jnp.exp2(tot - cum)).astype(BF16)
    q_st = (qf * jnp.exp2(cum)).astype(BF16)
    dec = jnp.exp2(tot)
    for h in range(HEADS):
        sl = slice(h * HEAD_DIM, (h + 1) * HEAD_DIM)
        sc = _dot_nt(q_in[:, sl], k_in[:, sl])
        sc = jnp.where(incl, sc, 0.0).astype(BF16)
        st = st_ref[h]
        o_ref[pl.ds(r0, c), sl] = _dot(sc, iv[:, sl]) + _dot_nt(q_st[:, sl], st.astype(BF16))
        st_ref[h] = st * dec[:, sl] + _dot_tn(iv[:, sl], k_st[:, sl])


def _hgrn_kernel(qf_ref, if_ref, lff_ref, qb_ref, ib_ref, lfb_ref, of_ref, ob_ref, stf_ref, stb_ref, *, tt):
    @pl.when(pl.program_id(1) == 0)
    def _():
        stf_ref[...] = jnp.zeros_like(stf_ref)
        stb_ref[...] = jnp.zeros_like(stb_ref)

    nchunks = tt // HGRN_CHUNK
    for ci in range(nchunks):
        _scan_chunk(qf_ref, if_ref, lff_ref, of_ref, stf_ref, ci * HGRN_CHUNK, False)
        _scan_chunk(qb_ref, ib_ref, lfb_ref, ob_ref, stb_ref, (nchunks - 1 - ci) * HGRN_CHUNK, True)


def _hgrn(q, i, lf_f, lf_b, batch, seq, ctx_len):
    t = q.shape[0]
    tt = SCAN_ROWS
    nt, nc = seq // tt, ctx_len // tt
    ctx0 = batch * nt

    def fwd(b, s):
        return (jnp.where(s < nc, ctx0 + b * nc + s, b * nt + (s - nc)), 0)

    def bwd(b, s):
        return (jnp.where(s < nc, ctx0 + b * nc + (nc - 1 - s), b * nt + (nt - 1 - (s - nc))), 0)

    tf, tb = pl.BlockSpec((tt, HW), fwd), pl.BlockSpec((tt, HW), bwd)
    return pl.pallas_call(
        functools.partial(_hgrn_kernel, tt=tt),
        out_shape=[jax.ShapeDtypeStruct((t, HW), F32)] * 2,
        grid=(batch, nc + nt),
        in_specs=[tf, tf, tf, tb, tb, tb],
        out_specs=[tf, tb],
        scratch_shapes=[pltpu.VMEM((HEADS, HEAD_DIM, HEAD_DIM), F32)] * 2,
        compiler_params=_cparams("arbitrary", "arbitrary"),
        name="hgrn",
    )(q, i, lf_f, q, i, lf_b)


def _route(logits):
    lane = lax.broadcasted_iota(jnp.int32, logits.shape, 1).astype(F32)
    neg = -jnp.inf
    is_group = lane < N_GROUPS
    gl = jnp.where(is_group, logits, neg)
    gmax = jnp.max(gl, axis=-1, keepdims=True)
    g_sel = jnp.min(jnp.where(gl == gmax, lane, float(ROUTE_LANES)), axis=-1, keepdims=True)
    den = jnp.sum(jnp.where(is_group, jnp.exp(logits - gmax), 0.0), axis=-1, keepdims=True)
    p_sel = 1.0 / den
    first = N_GROUPS + EXPERTS_PER_GROUP * g_sel
    el = jnp.where((lane >= first) & (lane < first + EXPERTS_PER_GROUP), logits, neg)
    t1 = jnp.max(el, axis=-1, keepdims=True)
    i1 = jnp.min(jnp.where(el == t1, lane, float(ROUTE_LANES)), axis=-1, keepdims=True)
    el2 = jnp.where(lane == i1, neg, el)
    t2 = jnp.max(el2, axis=-1, keepdims=True)
    i2 = jnp.min(jnp.where(el2 == t2, lane, float(ROUTE_LANES)), axis=-1, keepdims=True)
    e2 = jnp.exp(t2 - t1)
    w1 = p_sel / (1.0 + e2)
    w2 = p_sel * e2 / (1.0 + e2)
    rec = jnp.where(lane == 0.0, i1 - N_GROUPS, 0.0)
    rec = jnp.where(lane == 1.0, i2 - N_GROUPS, rec)
    rec = jnp.where(lane == 2.0, w1, rec)
    return jnp.where(lane == 3.0, w2, rec)


def _mixer_kernel(of_ref, ob_ref, sg_ref, gu_ref, vn_ref, *refs, tm, n_a_tiles, two_inputs):
    xa_ref, xb_ref = (refs[0], refs[1]) if two_inputs else (refs[0], None)
    (hgain_ref, ws_ref, bs_ref, wo_ref, g1_ref, n2_ref, sh2_ref, sc2_ref, wrh_ref, wrl_ref, br_ref,
     xo_ref, h2_ref, rt_ref, rtt_ref, cat_ref) = refs[2 if two_inputs else 1:]
    o = of_ref[...] + ob_ref[...]
    for h in range(HEADS):
        sl = slice(h * HEAD_DIM, (h + 1) * HEAD_DIM)
        hg = _rms(o[:, sl]) * hgain_ref[:, sl] * sg_ref[:, sl].astype(F32)
        cat_ref[:, sl] = hg.astype(BF16)
    for cc in range(tm // SGU_CHUNK):
        rows = slice(cc * SGU_CHUNK, (cc + 1) * SGU_CHUNK)
        for h in range(HEADS):
            sl = slice(h * HEAD_DIM, (h + 1) * HEAD_DIM)
            mixed = _dot(ws_ref[h], vn_ref[rows, sl]) + bs_ref[h]
            cat_ref[rows, HW + h * HEAD_DIM:HW + (h + 1) * HEAD_DIM] = (
                gu_ref[rows, sl].astype(F32) * mixed).astype(BF16)
    xn = _stream_tile(xa_ref, xb_ref, n_a_tiles) + g1_ref[...] * _dot(cat_ref[...], wo_ref[...])
    xo_ref[...] = xn
    h2 = _rms(xn) * n2_ref[...]
    h2 = h2 * (1.0 + sc2_ref[...]) + sh2_ref[...]
    _store_token_tiles(h2_ref, h2)
    hi, lo = _split2(h2)
    logits = _dot(hi, wrh_ref[...]) + (_dot(lo, wrh_ref[...]) + _dot(hi, wrl_ref[...])) + br_ref[...]
    rec = _route(logits)
    rt_ref[...] = rec
    rtt_ref[...] = rec.T[:SUBLANES, :]


def _mixer(o_f, o_b, sg, gu, vn, xa, xb, t_a, t_b, mod3, tokens_per_mod_row, hgain, w_s, b_s, w_out, n2,
           wr_hi, wr_lo, br):
    d = xa.shape[1]
    tm = MIX_ROWS
    n_a, n_b = t_a // tm, t_b // tm
    t = t_a + t_b
    nd = d // LANES
    row = functools.partial(_mod_row, d, tokens_per_mod_row // tm, mod3.shape[0])
    tok = pl.BlockSpec((tm, HW), lambda i: (i, 0))
    wide = pl.BlockSpec((tm, d), lambda i: (i, 0))
    x_in, x_spec = _stream_inputs(xa, xb, n_a, tm, d)
    return pl.pallas_call(
        functools.partial(_mixer_kernel, tm=tm, n_a_tiles=n_a, two_inputs=xb is not None),
        out_shape=[jax.ShapeDtypeStruct((t, d), F32), jax.ShapeDtypeStruct((t * nd, LANES), F32),
                   jax.ShapeDtypeStruct((t, ROUTE_LANES), F32), jax.ShapeDtypeStruct((SUBLANES, t), F32)],
        grid=(n_a + n_b,),
        in_specs=[tok, tok, tok, tok, tok] + x_spec + [_full(hgain), _full(w_s), _full(b_s), _full(w_out),
                  row(2), _full(n2), row(3), row(4), _full(wr_hi), _full(wr_lo), _full(br)],
        out_specs=[wide, pl.BlockSpec((tm * nd, LANES), lambda i: (i, 0)),
                   pl.BlockSpec((tm, ROUTE_LANES), lambda i: (i, 0)), pl.BlockSpec((SUBLANES, tm), lambda i: (0, i))],
        scratch_shapes=[pltpu.VMEM((tm, 2 * HW), BF16)],
        compiler_params=_cparams("arbitrary"),
        name="mixer",
    )(o_f, o_b, sg, gu, vn, *x_in, hgain, w_s, b_s, w_out, mod3, n2, mod3, mod3, wr_hi, wr_lo, br)


MOE_ROWS = 256
MOE_LAG = 3
RING = 3
DMA_GROUPS = 8


def _moe_kernel(last_ref, sexp_ref, tab_hbm, h_hbm, wg_ref, wu_ref, wd_ref, y_hbm,
                idx_ref, xbuf, ybuf, zbuf, xb_ref, hm_ref, wgb, wub, wdb, sem_idx, sem_g, sem_s, sem_z):
    i = pl.program_id(0)
    last = last_ref[0]
    bm = MOE_ROWS
    de = wgb.shape[1]
    d = wgb.shape[0]
    nd = d // LANES

    def idx_copy(step, slot):
        return pltpu.make_async_copy(tab_hbm.at[step], idx_ref.at[slot], sem_idx.at[slot])

    def gathered(slot):
        return pltpu.make_async_copy(h_hbm.at[pl.ds(0, bm * nd)], xbuf.at[slot], sem_g.at[slot])

    def scattered(slot):
        return pltpu.make_async_copy(ybuf.at[slot], y_hbm.at[pl.ds(0, bm * nd)], sem_s.at[slot])

    def step(k):
        gslot = k
        cslot = (k + 1) % RING
        sslot = k

        if k == 0:
            @pl.when(i == 0)
            def _():
                xbuf[...] = jnp.zeros_like(xbuf)
                ybuf[...] = jnp.zeros_like(ybuf)
                zbuf[...] = jnp.zeros_like(zbuf)
                idx_copy(0, 0).start()

        idx_copy(i, k).wait()

        @pl.when(i < last)
        def _():
            idx_copy(i + 1, (k + 1) % RING).start()

        @pl.when(i >= 2)
        def _():
            gathered(cslot).wait()
            scattered(cslot).wait()

        @pl.when((i == 0) | (sexp_ref[i] != sexp_ref[jnp.maximum(i - 1, 0)]))
        def _():
            wgb[...] = wg_ref[...].astype(BF16)
            wub[...] = wu_ref[...].astype(BF16)
            wdb[...] = wd_ref[...].astype(BF16)

        per = bm // (DMA_GROUPS // 2)

        def scatter_group(g):
            for r in range(g * per, (g + 1) * per):
                dst = pl.multiple_of(idx_ref[k, 1, r], nd)
                pltpu.make_async_copy(ybuf.at[sslot, pl.ds(r * nd, nd)], y_hbm.at[pl.ds(dst, nd)],
                                      sem_s.at[sslot]).start(priority=r % 2)

        def gather_group(g):
            for r in range(g * per, (g + 1) * per):
                src = pl.multiple_of(idx_ref[k, 0, r], nd)
                pltpu.make_async_copy(h_hbm.at[pl.ds(src, nd)], xbuf.at[gslot, pl.ds(r * nd, nd)],
                                      sem_g.at[gslot]).start(priority=r % 2)

        dma_groups = [functools.partial(scatter_group, g) for g in range(DMA_GROUPS // 2)]
        dma_groups += [functools.partial(gather_group, g) for g in range(DMA_GROUPS // 2)]

        def issue_some():
            if dma_groups:
                dma_groups.pop(0)()

        xsrc = xbuf.at[cslot]
        for j in range(nd):
            xb_ref[:, j * LANES:(j + 1) * LANES] = xsrc[pl.ds(j, bm, stride=nd), :].astype(BF16)
        nh = 2
        for j in range(nh):
            cs = slice(j * de // nh, (j + 1) * de // nh)
            issue_some()
            gate = _dot(xb_ref[...], wgb[:, cs])
            issue_some()
            hm_ref[:, cs] = (_silu(gate) * _dot(xb_ref[...], wub[:, cs])).astype(BF16)
        ydst = ybuf.at[cslot]
        n_down = min(4, nd)
        for j in range(n_down):
            issue_some()
            _store_token_tiles_cols(ydst, _dot(hm_ref[...], wdb[:, j * d // n_down:(j + 1) * d // n_down]),
                                    j * nd // n_down, nd)
        while dma_groups:
            issue_some()

        @pl.when(i == last)
        def _():
            gathered(gslot).wait()
            gathered((k + 2) % RING).wait()
            scattered(sslot).wait()
            scattered((k + 2) % RING).wait()

    for k in range(RING):
        pl.when((i <= last) & (i % RING == k))(functools.partial(step, k))

    @pl.when(i > last)
    def _():
        fill = pltpu.make_async_copy(zbuf, y_hbm.at[pl.ds((i - MOE_LAG) * (bm * nd), bm * nd)], sem_z.at[0])
        fill.start()
        fill.wait()


def _moe(h2, tab, last, sexp, n_out_rows, w_gate, w_up, w_down, layer):
    d, de = w_gate.shape[-2:]
    nd = d // LANES
    n_steps = tab.shape[0]
    grid_spec = pltpu.PrefetchScalarGridSpec(
        num_scalar_prefetch=2,
        grid=(n_steps,),
        in_specs=[
            pl.BlockSpec(memory_space=pl.ANY),
            pl.BlockSpec(memory_space=pl.ANY),
            pl.BlockSpec((None, None, d, de), lambda i, la, se: (layer, se[i], 0, 0)),
            pl.BlockSpec((None, None, d, de), lambda i, la, se: (layer, se[i], 0, 0)),
            pl.BlockSpec((None, None, de, d), lambda i, la, se: (layer, se[i], 0, 0)),
        ],
        out_specs=pl.BlockSpec(memory_space=pl.ANY),
        scratch_shapes=[
            pltpu.SMEM((RING, 2, MOE_ROWS), jnp.int32),
            pltpu.VMEM((RING, MOE_ROWS * nd, LANES), F32),
            pltpu.VMEM((RING, MOE_ROWS * nd, LANES), F32),
            pltpu.VMEM((MOE_ROWS * nd, LANES), F32),
            pltpu.VMEM((MOE_ROWS, d), BF16),
            pltpu.VMEM((MOE_ROWS, de), BF16),
            pltpu.VMEM((d, de), BF16),
            pltpu.VMEM((d, de), BF16),
            pltpu.VMEM((de, d), BF16),
            pltpu.SemaphoreType.DMA((RING,)),
            pltpu.SemaphoreType.DMA((RING,)),
            pltpu.SemaphoreType.DMA((RING,)),
            pltpu.SemaphoreType.DMA((1,)),
        ],
    )
    return pl.pallas_call(
        _moe_kernel,
        out_shape=jax.ShapeDtypeStruct((n_out_rows * nd, LANES), F32),
        grid_spec=grid_spec,
        compiler_params=_cparams("arbitrary"),
        name="moe",
    )(last, sexp, tab, h2, w_gate, w_up, w_down)


def _dispatch_tables(expert_flat, ttot, nd):
    bm = MOE_ROWS
    n_slots = expert_flat.shape[0]
    n_blocks = -(-n_slots // bm) + N_EXPERTS
    n_steps = n_blocks + MOE_LAG
    n_main = 1 << (n_slots.bit_length() - 1)
    pieces = []
    for lo, n in ((0, n_main), (n_main, n_slots - n_main)):
        if n:
            e = expert_flat[lo:lo + n]
            _, order = lax.sort_key_val(e, lax.iota(jnp.int32, n))
            cnt = jnp.sum(e[:, None] == jnp.arange(N_EXPERTS, dtype=jnp.int32)[None, :], axis=0, dtype=jnp.int32)
            pieces.append((order + lo, cnt))
    counts = sum(cnt for _, cnt in pieces)
    padded = (counts + bm - 1) // bm * bm
    pad_end = jnp.cumsum(padded)
    pad_start = pad_end - padded
    start = jnp.cumsum(counts) - counts
    blk_row0 = jnp.arange(n_blocks, dtype=jnp.int32) * bm
    bexp = jnp.minimum(jnp.sum(pad_end[None, :] <= blk_row0[:, None], axis=1), N_EXPERTS - 1).astype(jnp.int32)
    is_exp = bexp[:, None] == jnp.arange(N_EXPERTS, dtype=jnp.int32)[None, :]

    def of_block(per_expert):
        return jnp.sum(jnp.where(is_exp, per_expert[None, :], 0), axis=1, dtype=jnp.int32)

    blk_off, blk_cnt, blk_pad0 = of_block(-pad_start) + blk_row0, of_block(counts), of_block(start + counts)
    blk_piece = [(of_block(jnp.cumsum(cnt) - cnt), of_block(cnt)) for _, cnt in pieces]
    blk_off, blk_cnt, blk_pad0, blk_piece = lax.optimization_barrier((blk_off, blk_cnt, blk_pad0, blk_piece))
    lane = jnp.arange(bm, dtype=jnp.int32)[None, :]
    off = blk_off[:, None] + lane
    valid = off < blk_cnt[:, None]
    src, rem, base = jnp.zeros_like(off), off, 0
    for (order, _), (first, cnt_here) in zip(pieces, blk_piece):
        here = (rem >= 0) & (rem < cnt_here[:, None])
        src = jnp.where(here, base + first[:, None] + rem, src)
        rem = rem - cnt_here[:, None]
        base += order.shape[0]
    slot = jnp.concatenate([order for order, _ in pieces])[jnp.clip(src, 0, n_slots - 1)]
    pad_rank = blk_row0[:, None] + lane - blk_pad0[:, None]
    gsrc = jnp.where(valid, slot % ttot, 0)
    sdst = jnp.where(valid, slot, n_slots + pad_rank)
    spare = n_blocks * bm + jnp.arange(MOE_LAG * bm, dtype=jnp.int32).reshape(MOE_LAG, bm)
    gtab = jnp.concatenate([gsrc, jnp.zeros((MOE_LAG, bm), jnp.int32)], axis=0)
    stab = jnp.concatenate([spare, sdst], axis=0)
    tab = (jnp.stack([gtab, stab], axis=1) * nd).astype(jnp.int32)
    n_used = jnp.sum(padded) // bm
    last = (n_used + MOE_LAG - 1).astype(jnp.int32).reshape(1)
    sexp = bexp[jnp.clip(jnp.arange(n_steps) - (MOE_LAG - 1), 0, n_blocks - 1)]
    return tab, last, sexp, n_steps * bm


def _combine_kernel(x_ref, y0_ref, y1_ref, rt_ref, g2_ref, fn_ref, o_ref, *, final):
    xn = _moe_residual(x_ref, y0_ref, y1_ref, rt_ref, g2_ref)
    if final:
        xn = _rms(xn) * fn_ref[...]
    o_ref[...] = xn


def _combine(x2d, y2, t, ttot, route, mod3, tokens_per_mod_row, final_norm, final):
    d = x2d.shape[1]
    tm = MIX_ROWS
    b1 = ttot // tm
    nd = d // LANES
    return pl.pallas_call(
        functools.partial(_combine_kernel, final=final),
        out_shape=jax.ShapeDtypeStruct((t, d), F32),
        grid=(t // tm,),
        in_specs=[pl.BlockSpec((tm, d), lambda i: (i, 0)),
                  pl.BlockSpec((tm * nd, LANES), lambda i: (i, 0)),
                  pl.BlockSpec((tm * nd, LANES), lambda i: (i + b1, 0)),
                  pl.BlockSpec((tm, ROUTE_LANES), lambda i: (i, 0)),
                  _mod_row(d, tokens_per_mod_row // tm, mod3.shape[0], 5),
                  _full(final_norm)],
        out_specs=pl.BlockSpec((tm, d), lambda i: (i, 0)),
        compiler_params=_cparams("arbitrary"),
        name="combine",
    )(x2d, y2, y2, route, mod3, final_norm)


def kernel(x, c, ctx, c_ctx, norm1, norm2, w_mod, b_mod, w_in, lb_logits, hgrn_norm, sgu_norm, sgu_w, sgu_b,
           w_out, w_group, b_group, w_router, b_router, w_gate, w_up, w_down, final_norm):
    b, l, d = x.shape
    lc = ctx.shape[1]
    depth = w_mod.shape[0]
    t_lat, t_ctx = b * l, b * lc
    nd = d // LANES

    lb_cum = jnp.cumsum(jax.nn.softmax(lb_logits.astype(F32), axis=0), axis=0)
    lower_bound = jnp.maximum(lb_cum - lb_cum[0:1], 0.0)

    cc = jnp.zeros((MOD_ROWS, d), F32).at[:b].set(c).at[b].set(c_ctx)
    mod = _modulation(cc, w_mod, b_mod)

    w_route = jnp.concatenate([w_group, w_router], axis=-1)
    w_route = jnp.pad(w_route, ((0, 0), (0, 0), (0, ROUTE_LANES - w_route.shape[-1])))
    wr_hi = w_route.astype(BF16)
    wr_lo = (w_route - wr_hi.astype(F32)).astype(BF16)
    b_route = jnp.concatenate([b_group, b_router], axis=-1)
    b_route = jnp.pad(b_route, ((0, 0), (0, ROUTE_LANES - b_route.shape[-1])))[:, None, :]
    b_s = jnp.broadcast_to(sgu_b[..., None], sgu_b.shape + (HEAD_DIM,)).astype(F32)

    w_in_b, w_out_b, sgu_w_b = w_in.astype(BF16), w_out.astype(BF16), sgu_w.astype(BF16)

    xa, xb, t_a, t_b = x.reshape(t_lat, d), ctx.reshape(t_ctx, d), t_lat, t_ctx
    fn = final_norm.reshape(1, d)
    moe = None

    for layer in range(depth):
        last = layer == depth - 1
        mod3 = mod[layer, :b + 1].reshape(b + 1, 1, N_MOD * d)
        n1 = norm1[layer].reshape(1, d)
        n2 = norm2[layer].reshape(1, d)
        sgn = sgu_norm[layer].reshape(1, HW)
        hgain = hgrn_norm[layer].reshape(1, HW)

        q, i, lf_f, lf_b, sg, gu, vn, *x_new = _inproj(xa, xb, t_a, t_b, mod3, l, n1, w_in_b[layer],
                                                       lower_bound[layer], sgn, layer == 0, moe)
        if x_new:
            xa = x_new[0]
        o_f, o_b = _hgrn(q, i, lf_f, lf_b, b, l, lc)

        if last:
            xb, t_a, t_b = None, t_lat, 0
        ttot = t_a + t_b
        xs, h2, route, route_t = _mixer(o_f, o_b, sg, gu, vn, xa, xb, t_a, t_b, mod3, l, hgain, sgu_w_b[layer],
                                        b_s[layer], w_out_b[layer], n2, wr_hi[layer], wr_lo[layer],
                                        b_route[layer])

        expert_flat = route_t[:TOP_K].astype(jnp.int32).reshape(-1)
        tab, last_step, sexp, n_out_rows = _dispatch_tables(expert_flat, ttot, nd)
        y2 = _moe(h2, tab, last_step, sexp, n_out_rows, w_gate, w_up, w_down, layer)

        xa, xb, t_a, t_b = xs, None, ttot, 0
        moe = (y2, ttot, route, mod3)

    return _combine(xs, y2, ttot, ttot, route, mod3, l, fn, final=True).reshape(b, l, d)
```

```python
import functools

import jax
import jax.numpy as jnp
from jax import lax
from jax.experimental import pallas as pl
from jax.experimental.pallas import tpu as pltpu

F32 = jnp.float32
BF16 = jnp.bfloat16

EPS = 1e-6
LOG2_E = 1.4426950408889634
HEADS = 4
HEAD_DIM = 128
HW = HEADS * HEAD_DIM
HGRN_CHUNK = 64
SGU_CHUNK = 128
N_GROUPS = 4
EXPERTS_PER_GROUP = 8
N_EXPERTS = N_GROUPS * EXPERTS_PER_GROUP
TOP_K = 2
N_MOD = 6
LANES = 128
SUBLANES = 8
ROUTE_LANES = LANES
MOD_ROWS = 16

MOD_COLS = 1536
PROJ_ROWS = 512
SCAN_ROWS = 256
MIX_ROWS = 512

VMEM_LIMIT = 48 * 1024 * 1024


def _cparams(*sem):
    return pltpu.CompilerParams(dimension_semantics=sem, vmem_limit_bytes=VMEM_LIMIT)


def _split2(a):
    hi = a.astype(BF16)
    lo = (a - hi.astype(F32)).astype(BF16)
    return hi, lo


def _dot(a, b):
    return jnp.dot(a, b, preferred_element_type=F32)


def _dot_nt(a, b):
    return lax.dot_general(a, b, (((1,), (1,)), ((), ())), preferred_element_type=F32)


def _dot_tn(a, b):
    return lax.dot_general(a, b, (((0,), (0,)), ((), ())), preferred_element_type=F32)


def _dot3(a, b):
    ah, al = _split2(a)
    bh, bl = _split2(b)
    return _dot(ah, bh) + (_dot(al, bh) + _dot(ah, bl))


def _silu(x):
    return x / (1.0 + jnp.exp(-x))


def _rms(x):
    return x * lax.rsqrt(jnp.mean(x * x, axis=-1, keepdims=True) + EPS)


def _full(a):
    return pl.BlockSpec(a.shape, lambda *_: (0,) * a.ndim)


def _stream_inputs(xa, xb, n_a, tm, d):
    if xb is None:
        return [xa], [pl.BlockSpec((tm, d), lambda i: (i, 0))]
    return [xa, xb], [pl.BlockSpec((tm, d), lambda i: (jnp.minimum(i, n_a - 1), 0)),
                      pl.BlockSpec((tm, d), lambda i: (jnp.maximum(i - n_a, 0), 0))]


def _mod_row(d, tiles_per_row, n_rows, j):
    return pl.BlockSpec((None, 1, d), lambda i: (jnp.minimum(i // tiles_per_row, n_rows - 1), 0, j))


def _store_token_tiles_cols(ref, x, j0, nd):
    rows, w = x.shape
    for j in range(w // LANES):
        ref[pl.ds(j0 + j, rows, stride=nd), :] = x[:, j * LANES:(j + 1) * LANES]


def _store_token_tiles(ref, x):
    _store_token_tiles_cols(ref, x, 0, x.shape[1] // LANES)


def _load_token_tiles(ref, rows, nd):
    return jnp.concatenate([ref[pl.ds(j, rows, stride=nd), :] for j in range(nd)], axis=-1)


def _mod_kernel(c_ref, w_ref, b_ref, o_ref):
    o_ref[...] = _dot3(_silu(c_ref[...]), w_ref[...]) + b_ref[...]


def _modulation(cc, w_mod, b_mod):
    depth, d, n = w_mod.shape
    tn = min(MOD_COLS, n)
    return pl.pallas_call(
        _mod_kernel,
        out_shape=jax.ShapeDtypeStruct((depth, MOD_ROWS, n), F32),
        grid=(depth, n // tn),
        in_specs=[
            pl.BlockSpec((MOD_ROWS, d), lambda l, j: (0, 0)),
            pl.BlockSpec((None, d, tn), lambda l, j: (l, 0, j)),
            pl.BlockSpec((None, 1, tn), lambda l, j: (l, 0, j)),
        ],
        out_specs=pl.BlockSpec((None, MOD_ROWS, tn), lambda l, j: (l, 0, j)),
        compiler_params=_cparams("arbitrary", "arbitrary"),
        name="modulation",
    )(cc, w_mod, b_mod.reshape(depth, 1, n))


def _log_forget(z, lb, lb_is_zero):
    ls = jnp.minimum(z, 0.0) - jnp.log(1.0 + jnp.exp(-jnp.abs(z)))
    if lb_is_zero:
        return ls
    return jnp.maximum(jnp.log(lb + (1.0 - lb) * jnp.exp(ls)), ls)


def _stream_tile(xa_ref, xb_ref, n_a_tiles):
    if xb_ref is None:
        return xa_ref[...]
    return jnp.where(pl.program_id(0) < n_a_tiles, xa_ref[...], xb_ref[...])


def _moe_residual(x_ref, y0_ref, y1_ref, rt_ref, g2_ref):
    w = rt_ref[...]
    tm, d = x_ref.shape
    nd = d // LANES
    f = w[:, 2:3] * _load_token_tiles(y0_ref, tm, nd) + w[:, 3:4] * _load_token_tiles(y1_ref, tm, nd)
    return x_ref[...] + g2_ref[...] * f


def _inproj_kernel(*refs, lb_is_zero, n_a_tiles, n_x):
    if n_x == 5:
        x = _moe_residual(*refs[:5])
    else:
        x = _stream_tile(refs[0], refs[1] if n_x == 2 else None, n_a_tiles)
    (sh_ref, sc_ref, n1_ref, w_ref, lb_ref, sgn_ref,
     q_ref, i_ref, lff_ref, lfb_ref, sg_ref, gu_ref, vn_ref, *xo_ref) = refs[n_x:]
    if xo_ref:
        xo_ref[0][...] = x
    h = _rms(x) * n1_ref[...]
    hb = (h * (1.0 + sc_ref[...]) + sh_ref[...]).astype(BF16)

    def proj(j):
        return _dot(hb, w_ref[:, j * HW:(j + 1) * HW])

    q_ref[...] = proj(0).astype(BF16)
    lff_ref[...] = _log_forget(proj(1), lb_ref[0:1, :], lb_is_zero)
    lfb_ref[...] = _log_forget(proj(2), lb_ref[1:2, :], lb_is_zero)
    i_ref[...] = proj(3).astype(BF16)
    sg_ref[...] = _silu(proj(4)).astype(BF16)
    gu_ref[...] = jax.nn.gelu(proj(5)).astype(BF16)
    vn_ref[...] = (_rms(jax.nn.gelu(proj(6))) * sgn_ref[...]).astype(BF16)


def _inproj(xa, xb, t_a, t_b, mod3, tokens_per_mod_row, n1, w_in, lb, sgu_gain, lb_is_zero, moe=None):
    d = xa.shape[1]
    tm = PROJ_ROWS
    n_a, n_b = t_a // tm, t_b // tm
    t = t_a + t_b
    nd = d // LANES
    tok = pl.BlockSpec((tm, HW), lambda i: (i, 0))
    wide = pl.BlockSpec((tm, d), lambda i: (i, 0))
    row = functools.partial(_mod_row, d, tokens_per_mod_row // tm, mod3.shape[0])
    out_shape = [jax.ShapeDtypeStruct((t, HW), dt) for dt in (BF16, BF16, F32, F32, BF16, BF16, BF16)]
    out_specs = [tok] * 7
    if moe is None:
        x_in, x_spec = _stream_inputs(xa, xb, n_a, tm, d)
    else:
        y2, ttot, route, mod3_prev = moe
        b1 = ttot // tm
        x_in = [xa, y2, y2, route, mod3_prev]
        x_spec = [wide,
                  pl.BlockSpec((tm * nd, LANES), lambda i: (i, 0)),
                  pl.BlockSpec((tm * nd, LANES), lambda i: (i + b1, 0)),
                  pl.BlockSpec((tm, ROUTE_LANES), lambda i: (i, 0)),
                  _mod_row(d, tokens_per_mod_row // tm, mod3_prev.shape[0], 5)]
        out_shape.append(jax.ShapeDtypeStruct((t, d), F32))
        out_specs.append(wide)
    return pl.pallas_call(
        functools.partial(_inproj_kernel, lb_is_zero=lb_is_zero, n_a_tiles=n_a, n_x=len(x_in)),
        out_shape=out_shape,
        grid=(n_a + n_b,),
        in_specs=x_spec + [row(0), row(1), _full(n1), _full(w_in), _full(lb), _full(sgu_gain)],
        out_specs=out_specs,
        compiler_params=_cparams("arbitrary"),
        name="inproj",
    )(*x_in, mod3, mod3, n1, w_in, lb, sgu_gain)


def _chunk_cumsum(x, reverse):
    c, w = x.shape
    g = c // SUBLANES
    x3 = x.reshape(g, SUBLANES, w)
    sub = lax.broadcasted_iota(jnp.int32, x3.shape, 1)
    for s in (1, 2, 4):
        if reverse:
            x3 = x3 + jnp.where(sub < SUBLANES - s, pltpu.roll(x3, SUBLANES - s, axis=1), 0.0)
        else:
            x3 = x3 + jnp.where(sub >= s, pltpu.roll(x3, s, axis=1), 0.0)
    edge = 0 if reverse else SUBLANES - 1
    tot = x3[:, edge:edge + 1, :]
    offs = [None] * g
    acc = jnp.zeros((1, w), F32)
    for gi in (reversed(range(g)) if reverse else range(g)):
        offs[gi] = acc
        acc = acc + tot[gi]
    x3 = x3 + jnp.stack(offs, axis=0)
    return x3.reshape(c, w)


def _scan_chunk(q_ref, i_ref, lf_ref, o_ref, st_ref, r0, reverse):
    c = HGRN_CHUNK
    rows = lax.broadcasted_iota(jnp.int32, (c, c), 0)
    cols = lax.broadcasted_iota(jnp.int32, (c, c), 1)
    incl = (cols >= rows) if reverse else (cols <= rows)
    ref_row = c // 2 if reverse else c // 2 - 1
    tot_row = 0 if reverse else c - 1
    lf = lf_ref[pl.ds(r0, c), :] * LOG2_E
    cum = _chunk_cumsum(lf, reverse)
    ref = cum[ref_row:ref_row + 1, :]
    tot = cum[tot_row:tot_row + 1, :]
    k = 1.0 - jnp.exp2(lf)
    qf = q_ref[pl.ds(r0, c), :].astype(F32)
    iv = i_ref[pl.ds(r0, c), :]
    q_in = (qf * jnp.exp2(cum - ref)).astype(BF16)
    k_in = (k * jnp.exp2(ref - cum)).astype(BF16)
    k_st = (k * jnp.exp2(tot - cum)).astype(BF16)
    q_st = (qf * jnp.exp2(cum)).astype(BF16)
    dec = jnp.exp2(tot)
    for h in range(HEADS):
        sl = slice(h * HEAD_DIM, (h + 1) * HEAD_DIM)
        sc = _dot_nt(q_in[:, sl], k_in[:, sl])
        sc = jnp.where(incl, sc, 0.0).astype(BF16)
        st = st_ref[h]
        o_ref[pl.ds(r0, c), sl] = _dot(sc, iv[:, sl]) + _dot_nt(q_st[:, sl], st.astype(BF16))
        st_ref[h] = st * dec[:, sl] + _dot_tn(iv[:, sl], k_st[:, sl])


def _hgrn_kernel(qf_ref, if_ref, lff_ref, qb_ref, ib_ref, lfb_ref, of_ref, ob_ref, stf_ref, stb_ref, *, tt):
    @pl.when(pl.program_id(1) == 0)
    def _():
        stf_ref[...] = jnp.zeros_like(stf_ref)
        stb_ref[...] = jnp.zeros_like(stb_ref)

    nchunks = tt // HGRN_CHUNK
    for ci in range(nchunks):
        _scan_chunk(qf_ref, if_ref, lff_ref, of_ref, stf_ref, ci * HGRN_CHUNK, False)
        _scan_chunk(qb_ref, ib_ref, lfb_ref, ob_ref, stb_ref, (nchunks - 1 - ci) * HGRN_CHUNK, True)


def _hgrn(q, i, lf_f, lf_b, batch, seq, ctx_len):
    t = q.shape[0]
    tt = SCAN_ROWS
    nt, nc = seq // tt, ctx_len // tt
    ctx0 = batch * nt

    def fwd(b, s):
        return (jnp.where(s < nc, ctx0 + b * nc + s, b * nt + (s - nc)), 0)

    def bwd(b, s):
        return (jnp.where(s < nc, ctx0 + b * nc + (nc - 1 - s), b * nt + (nt - 1 - (s - nc))), 0)

    tf, tb = pl.BlockSpec((tt, HW), fwd), pl.BlockSpec((tt, HW), bwd)
    return pl.pallas_call(
        functools.partial(_hgrn_kernel, tt=tt),
        out_shape=[jax.ShapeDtypeStruct((t, HW), F32)] * 2,
        grid=(batch, nc + nt),
        in_specs=[tf, tf, tf, tb, tb, tb],
        out_specs=[tf, tb],
        scratch_shapes=[pltpu.VMEM((HEADS, HEAD_DIM, HEAD_DIM), F32)] * 2,
        compiler_params=_cparams("arbitrary", "arbitrary"),
        name="hgrn",
    )(q, i, lf_f, q, i, lf_b)


def _route(logits):
    lane = lax.broadcasted_iota(jnp.int32, logits.shape, 1).astype(F32)
    neg = -jnp.inf
    is_group = lane < N_GROUPS
    gl = jnp.where(is_group, logits, neg)
    gmax = jnp.max(gl, axis=-1, keepdims=True)
    g_sel = jnp.min(jnp.where(gl == gmax, lane, float(ROUTE_LANES)), axis=-1, keepdims=True)
    den = jnp.sum(jnp.where(is_group, jnp.exp(logits - gmax), 0.0), axis=-1, keepdims=True)
    p_sel = 1.0 / den
    first = N_GROUPS + EXPERTS_PER_GROUP * g_sel
    el = jnp.where((lane >= first) & (lane < first + EXPERTS_PER_GROUP), logits, neg)
    t1 = jnp.max(el, axis=-1, keepdims=True)
    i1 = jnp.min(jnp.where(el == t1, lane, float(ROUTE_LANES)), axis=-1, keepdims=True)
    el2 = jnp.where(lane == i1, neg, el)
    t2 = jnp.max(el2, axis=-1, keepdims=True)
    i2 = jnp.min(jnp.where(el2 == t2, lane, float(ROUTE_LANES)), axis=-1, keepdims=True)
    e2 = jnp.exp(t2 - t1)
    w1 = p_sel / (1.0 + e2)
    w2 = p_sel * e2 / (1.0 + e2)
    rec = jnp.where(lane == 0.0, i1 - N_GROUPS, 0.0)
    rec = jnp.where(lane == 1.0, i2 - N_GROUPS, rec)
    rec = jnp.where(lane == 2.0, w1, rec)
    return jnp.where(lane == 3.0, w2, rec)


def _mixer_kernel(of_ref, ob_ref, sg_ref, gu_ref, vn_ref, *refs, tm, n_a_tiles, two_inputs):
    xa_ref, xb_ref = (refs[0], refs[1]) if two_inputs else (refs[0], None)
    (hgain_ref, ws_ref, bs_ref, wo_ref, g1_ref, n2_ref, sh2_ref, sc2_ref, wrh_ref, wrl_ref, br_ref,
     xo_ref, h2_ref, rt_ref, rtt_ref, cat_ref) = refs[2 if two_inputs else 1:]
    o = of_ref[...] + ob_ref[...]
    for h in range(HEADS):
        sl = slice(h * HEAD_DIM, (h + 1) * HEAD_DIM)
        hg = _rms(o[:, sl]) * hgain_ref[:, sl] * sg_ref[:, sl].astype(F32)
        cat_ref[:, sl] = hg.astype(BF16)
    for cc in range(tm // SGU_CHUNK):
        rows = slice(cc * SGU_CHUNK, (cc + 1) * SGU_CHUNK)
        for h in range(HEADS):
            sl = slice(h * HEAD_DIM, (h + 1) * HEAD_DIM)
            mixed = _dot(ws_ref[h], vn_ref[rows, sl]) + bs_ref[h]
            cat_ref[rows, HW + h * HEAD_DIM:HW + (h + 1) * HEAD_DIM] = (
                gu_ref[rows, sl].astype(F32) * mixed).astype(BF16)
    xn = _stream_tile(xa_ref, xb_ref, n_a_tiles) + g1_ref[...] * _dot(cat_ref[...], wo_ref[...])
    xo_ref[...] = xn
    h2 = _rms(xn) * n2_ref[...]
    h2 = h2 * (1.0 + sc2_ref[...]) + sh2_ref[...]
    _store_token_tiles(h2_ref, h2)
    hi, lo = _split2(h2)
    logits = _dot(hi, wrh_ref[...]) + (_dot(lo, wrh_ref[...]) + _dot(hi, wrl_ref[...])) + br_ref[...]
    rec = _route(logits)
    rt_ref[...] = rec
    rtt_ref[...] = rec.T[:SUBLANES, :]


def _mixer(o_f, o_b, sg, gu, vn, xa, xb, t_a, t_b, mod3, tokens_per_mod_row, hgain, w_s, b_s, w_out, n2,
           wr_hi, wr_lo, br):
    d = xa.shape[1]
    tm = MIX_ROWS
    n_a, n_b = t_a // tm, t_b // tm
    t = t_a + t_b
    nd = d // LANES
    row = functools.partial(_mod_row, d, tokens_per_mod_row // tm, mod3.shape[0])
    tok = pl.BlockSpec((tm, HW), lambda i: (i, 0))
    wide = pl.BlockSpec((tm, d), lambda i: (i, 0))
    x_in, x_spec = _stream_inputs(xa, xb, n_a, tm, d)
    return pl.pallas_call(
        functools.partial(_mixer_kernel, tm=tm, n_a_tiles=n_a, two_inputs=xb is not None),
        out_shape=[jax.ShapeDtypeStruct((t, d), F32), jax.ShapeDtypeStruct((t * nd, LANES), F32),
                   jax.ShapeDtypeStruct((t, ROUTE_LANES), F32), jax.ShapeDtypeStruct((SUBLANES, t), F32)],
        grid=(n_a + n_b,),
        in_specs=[tok, tok, tok, tok, tok] + x_spec + [_full(hgain), _full(w_s), _full(b_s), _full(w_out),
                  row(2), _full(n2), row(3), row(4), _full(wr_hi), _full(wr_lo), _full(br)],
        out_specs=[wide, pl.BlockSpec((tm * nd, LANES), lambda i: (i, 0)),
                   pl.BlockSpec((tm, ROUTE_LANES), lambda i: (i, 0)), pl.BlockSpec((SUBLANES, tm), lambda i: (0, i))],
        scratch_shapes=[pltpu.VMEM((tm, 2 * HW), BF16)],
        compiler_params=_cparams("arbitrary"),
        name="mixer",
    )(o_f, o_b, sg, gu, vn, *x_in, hgain, w_s, b_s, w_out, mod3, n2, mod3, mod3, wr_hi, wr_lo, br)


MOE_ROWS = 256
MOE_LAG = 3
RING = 3
DMA_GROUPS = 8


def _moe_kernel(last_ref, sexp_ref, tab_hbm, h_hbm, wg_ref, wu_ref, wd_ref, y_hbm,
                idx_ref, xbuf, ybuf, zbuf, xb_ref, hm_ref, wgb, wub, wdb, sem_idx, sem_g, sem_s, sem_z):
    i = pl.program_id(0)
    last = last_ref[0]
    bm = MOE_ROWS
    de = wgb.shape[1]
    d = wgb.shape[0]
    nd = d // LANES

    def idx_copy(step, slot):
        return pltpu.make_async_copy(tab_hbm.at[step], idx_ref.at[slot], sem_idx.at[slot])

    def gathered(slot):
        return pltpu.make_async_copy(h_hbm.at[pl.ds(0, bm * nd)], xbuf.at[slot], sem_g.at[slot])

    def scattered(slot):
        return pltpu.make_async_copy(ybuf.at[slot], y_hbm.at[pl.ds(0, bm * nd)], sem_s.at[slot])

    def step(k):
        gslot = k
        cslot = (k + 1) % RING
        sslot = k

        if k == 0:
            @pl.when(i == 0)
            def _():
                xbuf[...] = jnp.zeros_like(xbuf)
                ybuf[...] = jnp.zeros_like(ybuf)
                zbuf[...] = jnp.zeros_like(zbuf)
                idx_copy(0, 0).start()

        idx_copy(i, k).wait()

        @pl.when(i < last)
        def _():
            idx_copy(i + 1, (k + 1) % RING).start()

        @pl.when(i >= 2)
        def _():
            gathered(cslot).wait()
            scattered(cslot).wait()

        @pl.when((i == 0) | (sexp_ref[i] != sexp_ref[jnp.maximum(i - 1, 0)]))
        def _():
            wgb[...] = wg_ref[...].astype(BF16)
            wub[...] = wu_ref[...].astype(BF16)
            wdb[...] = wd_ref[...].astype(BF16)

        per = bm // (DMA_GROUPS // 2)

        def scatter_group(g):
            for r in range(g * per, (g + 1) * per):
                dst = pl.multiple_of(idx_ref[k, 1, r], nd)
                pltpu.make_async_copy(ybuf.at[sslot, pl.ds(r * nd, nd)], y_hbm.at[pl.ds(dst, nd)],
                                      sem_s.at[sslot]).start(priority=r % 2)

        def gather_group(g):
            for r in range(g * per, (g + 1) * per):
                src = pl.multiple_of(idx_ref[k, 0, r], nd)
                pltpu.make_async_copy(h_hbm.at[pl.ds(src, nd)], xbuf.at[gslot, pl.ds(r * nd, nd)],
                                      sem_g.at[gslot]).start(priority=r % 2)

        dma_groups = [functools.partial(scatter_group, g) for g in range(DMA_GROUPS // 2)]
        dma_groups += [functools.partial(gather_group, g) for g in range(DMA_GROUPS // 2)]

        def issue_some():
            if dma_groups:
                dma_groups.pop(0)()

        xsrc = xbuf.at[cslot]
        for j in range(nd):
            xb_ref[:, j * LANES:(j + 1) * LANES] = xsrc[pl.ds(j, bm, stride=nd), :].astype(BF16)
        nh = 2
        for j in range(nh):
            cs = slice(j * de // nh, (j + 1) * de // nh)
            issue_some()
            gate = _dot(xb_ref[...], wgb[:, cs])
            issue_some()
            hm_ref[:, cs] = (_silu(gate) * _dot(xb_ref[...], wub[:, cs])).astype(BF16)
        ydst = ybuf.at[cslot]
        n_down = min(4, nd)
        for j in range(n_down):
            issue_some()
            _store_token_tiles_cols(ydst, _dot(hm_ref[...], wdb[:, j * d // n_down:(j + 1) * d // n_down]),
                                    j * nd // n_down, nd)
        while dma_groups:
            issue_some()

        @pl.when(i == last)
        def _():
            gathered(gslot).wait()
            gathered((k + 2) % RING).wait()
            scattered(sslot).wait()
            scattered((k + 2) % RING).wait()

    for k in range(RING):
        pl.when((i <= last) & (i % RING == k))(functools.partial(step, k))

    @pl.when(i > last)
    def _():
        fill = pltpu.make_async_copy(zbuf, y_hbm.at[pl.ds((i - MOE_LAG) * (bm * nd), bm * nd)], sem_z.at[0])
        fill.start()
        fill.wait()


def _moe(h2, tab, last, sexp, n_out_rows, w_gate, w_up, w_down, layer):
    d, de = w_gate.shape[-2:]
    nd = d // LANES
    n_steps = tab.shape[0]
    grid_spec = pltpu.PrefetchScalarGridSpec(
        num_scalar_prefetch=2,
        grid=(n_steps,),
        in_specs=[
            pl.BlockSpec(memory_space=pl.ANY),
            pl.BlockSpec(memory_space=pl.ANY),
            pl.BlockSpec((None, None, d, de), lambda i, la, se: (layer, se[i], 0, 0)),
            pl.BlockSpec((None, None, d, de), lambda i, la, se: (layer, se[i], 0, 0)),
            pl.BlockSpec((None, None, de, d), lambda i, la, se: (layer, se[i], 0, 0)),
        ],
        out_specs=pl.BlockSpec(memory_space=pl.ANY),
        scratch_shapes=[
            pltpu.SMEM((RING, 2, MOE_ROWS), jnp.int32),
            pltpu.VMEM((RING, MOE_ROWS * nd, LANES), F32),
            pltpu.VMEM((RING, MOE_ROWS * nd, LANES), F32),
            pltpu.VMEM((MOE_ROWS * nd, LANES), F32),
            pltpu.VMEM((MOE_ROWS, d), BF16),
            pltpu.VMEM((MOE_ROWS, de), BF16),
            pltpu.VMEM((d, de), BF16),
            pltpu.VMEM((d, de), BF16),
            pltpu.VMEM((de, d), BF16),
            pltpu.SemaphoreType.DMA((RING,)),
            pltpu.SemaphoreType.DMA((RING,)),
            pltpu.SemaphoreType.DMA((RING,)),
            pltpu.SemaphoreType.DMA((1,)),
        ],
    )
    return pl.pallas_call(
        _moe_kernel,
        out_shape=jax.ShapeDtypeStruct((n_out_rows * nd, LANES), F32),
        grid_spec=grid_spec,
        compiler_params=_cparams("arbitrary"),
        name="moe",
    )(last, sexp, tab, h2, w_gate, w_up, w_down)


def _dispatch_tables(expert_flat, ttot, nd):
    bm = MOE_ROWS
    n_slots = expert_flat.shape[0]
    n_blocks = -(-n_slots // bm) + N_EXPERTS
    n_steps = n_blocks + MOE_LAG
    n_main = 1 << (n_slots.bit_length() - 1)
    pieces = []
    for lo, n in ((0, n_main), (n_main, n_slots - n_main)):
        if n:
            e = expert_flat[lo:lo + n]
            _, order = lax.sort_key_val(e, lax.iota(jnp.int32, n))
            cnt = jnp.sum(e[:, None] == jnp.arange(N_EXPERTS, dtype=jnp.int32)[None, :], axis=0, dtype=jnp.int32)
            pieces.append((order + lo, cnt))
    counts = sum(cnt for _, cnt in pieces)
    padded = (counts + bm - 1) // bm * bm
    pad_end = jnp.cumsum(padded)
    pad_start = pad_end - padded
    start = jnp.cumsum(counts) - counts
    blk_row0 = jnp.arange(n_blocks, dtype=jnp.int32) * bm
    bexp = jnp.minimum(jnp.sum(pad_end[None, :] <= blk_row0[:, None], axis=1), N_EXPERTS - 1).astype(jnp.int32)
    is_exp = bexp[:, None] == jnp.arange(N_EXPERTS, dtype=jnp.int32)[None, :]

    def of_block(per_expert):
        return jnp.sum(jnp.where(is_exp, per_expert[None, :], 0), axis=1, dtype=jnp.int32)

    blk_off, blk_cnt, blk_pad0 = of_block(-pad_start) + blk_row0, of_block(counts), of_block(start + counts)
    blk_piece = [(of_block(jnp.cumsum(cnt) - cnt), of_block(cnt)) for _, cnt in pieces]
    blk_off, blk_cnt, blk_pad0, blk_piece = lax.optimization_barrier((blk_off, blk_cnt, blk_pad0, blk_piece))
    lane = jnp.arange(bm, dtype=jnp.int32)[None, :]
    off = blk_off[:, None] + lane
    valid = off < blk_cnt[:, None]
    src, rem, base = jnp.zeros_like(off), off, 0
    for (order, _), (first, cnt_here) in zip(pieces, blk_piece):
        here = (rem >= 0) & (rem < cnt_here[:, None])
        src = jnp.where(here, base + first[:, None] + rem, src)
        rem = rem - cnt_here[:, None]
        base += order.shape[0]
    slot = jnp.concatenate([order for order, _ in pieces])[jnp.clip(src, 0, n_slots - 1)]
    pad_rank = blk_row0[:, None] + lane - blk_pad0[:, None]
    gsrc = jnp.where(valid, slot % ttot, 0)
    sdst = jnp.where(valid, slot, n_slots + pad_rank)
    spare = n_blocks * bm + jnp.arange(MOE_LAG * bm, dtype=jnp.int32).reshape(MOE_LAG, bm)
    gtab = jnp.concatenate([gsrc, jnp.zeros((MOE_LAG, bm), jnp.int32)], axis=0)
    stab = jnp.concatenate([spare, sdst], axis=0)
    tab = (jnp.stack([gtab, stab], axis=1) * nd).astype(jnp.int32)
    n_used = jnp.sum(padded) // bm
    last = (n_used + MOE_LAG - 1).astype(jnp.int32).reshape(1)
    sexp = bexp[jnp.clip(jnp.arange(n_steps) - (MOE_LAG - 1), 0, n_blocks - 1)]
    return tab, last, sexp, n_steps * bm


def _final_kernel(x_ref, y0_ref, y1_ref, rt_ref, g2_ref, fn_ref, o_ref):
    o_ref[...] = _rms(_moe_residual(x_ref, y0_ref, y1_ref, rt_ref, g2_ref)) * fn_ref[...]


def _final(x2d, y2, ttot, route, mod3, tokens_per_mod_row, final_norm):
    t, d = x2d.shape
    tm = MIX_ROWS
    b1 = ttot // tm
    nd = d // LANES
    return pl.pallas_call(
        _final_kernel,
        out_shape=jax.ShapeDtypeStruct((t, d), F32),
        grid=(t // tm,),
        in_specs=[pl.BlockSpec((tm, d), lambda i: (i, 0)),
                  pl.BlockSpec((tm * nd, LANES), lambda i: (i, 0)),
                  pl.BlockSpec((tm * nd, LANES), lambda i: (i + b1, 0)),
                  pl.BlockSpec((tm, ROUTE_LANES), lambda i: (i, 0)),
                  _mod_row(d, tokens_per_mod_row // tm, mod3.shape[0], 5),
                  _full(final_norm)],
        out_specs=pl.BlockSpec((tm, d), lambda i: (i, 0)),
        compiler_params=_cparams("arbitrary"),
        name="final",
    )(x2d, y2, y2, route, mod3, final_norm)


def kernel(x, c, ctx, c_ctx, norm1, norm2, w_mod, b_mod, w_in, lb_logits, hgrn_norm, sgu_norm, sgu_w, sgu_b,
           w_out, w_group, b_group, w_router, b_router, w_gate, w_up, w_down, final_norm):
    b, l, d = x.shape
    lc = ctx.shape[1]
    depth = w_mod.shape[0]
    t_lat, t_ctx = b * l, b * lc
    nd = d // LANES
    assert d % LANES == 0 and l % max(PROJ_ROWS, MIX_ROWS, SCAN_ROWS) == 0 and lc % SCAN_ROWS == 0, (d, l, lc)
    assert t_ctx % max(PROJ_ROWS, MIX_ROWS) == 0 and b < MOD_ROWS, (b, lc)
    assert w_in.shape[-1] == 7 * HW and w_out.shape[-2] == 2 * HW, (w_in.shape, w_out.shape)

    lb_cum = jnp.cumsum(jax.nn.softmax(lb_logits.astype(F32), axis=0), axis=0)
    lower_bound = jnp.maximum(lb_cum - lb_cum[0:1], 0.0)

    cc = jnp.zeros((MOD_ROWS, d), F32).at[:b].set(c).at[b].set(c_ctx)
    mod = _modulation(cc, w_mod, b_mod)

    w_route = jnp.concatenate([w_group, w_router], axis=-1)
    w_route = jnp.pad(w_route, ((0, 0), (0, 0), (0, ROUTE_LANES - w_route.shape[-1])))
    wr_hi = w_route.astype(BF16)
    wr_lo = (w_route - wr_hi.astype(F32)).astype(BF16)
    b_route = jnp.concatenate([b_group, b_router], axis=-1)
    b_route = jnp.pad(b_route, ((0, 0), (0, ROUTE_LANES - b_route.shape[-1])))[:, None, :]
    b_s = jnp.broadcast_to(sgu_b[..., None], sgu_b.shape + (HEAD_DIM,)).astype(F32)

    w_in_b, w_out_b, sgu_w_b = w_in.astype(BF16), w_out.astype(BF16), sgu_w.astype(BF16)

    xa, xb, t_a, t_b = x.reshape(t_lat, d), ctx.reshape(t_ctx, d), t_lat, t_ctx
    fn = final_norm.reshape(1, d)
    moe = None

    for layer in range(depth):
        last = layer == depth - 1
        mod3 = mod[layer, :b + 1].reshape(b + 1, 1, N_MOD * d)
        n1 = norm1[layer].reshape(1, d)
        n2 = norm2[layer].reshape(1, d)
        sgn = sgu_norm[layer].reshape(1, HW)
        hgain = hgrn_norm[layer].reshape(1, HW)

        q, i, lf_f, lf_b, sg, gu, vn, *x_new = _inproj(xa, xb, t_a, t_b, mod3, l, n1, w_in_b[layer],
                                                       lower_bound[layer], sgn, layer == 0, moe)
        if x_new:
            xa = x_new[0]
        o_f, o_b = _hgrn(q, i, lf_f, lf_b, b, l, lc)

        if last:
            xb, t_a, t_b = None, t_lat, 0
        ttot = t_a + t_b
        xs, h2, route, route_t = _mixer(o_f, o_b, sg, gu, vn, xa, xb, t_a, t_b, mod3, l, hgain, sgu_w_b[layer],
                                        b_s[layer], w_out_b[layer], n2, wr_hi[layer], wr_lo[layer],
                                        b_route[layer])

        expert_flat = route_t[:TOP_K].astype(jnp.int32).reshape(-1)
        tab, last_step, sexp, n_out_rows = _dispatch_tables(expert_flat, ttot, nd)
        y2 = _moe(h2, tab, last_step, sexp, n_out_rows, w_gate, w_up, w_down, layer)

        xa, xb, t_a, t_b = xs, None, ttot, 0
        moe = (y2, ttot, route, mod3)

    return _final(xs, y2, ttot, route, mod3, l, fn).reshape(b, l, d)
```

```python
import functools

import jax
import jax.numpy as jnp
from jax import lax
from jax.experimental import pallas as pl
from jax.experimental.pallas import tpu as pltpu

F32 = jnp.float32
BF16 = jnp.bfloat16

EPS = 1e-6
LOG2_E = 1.4426950408889634
HEADS = 4
HEAD_DIM = 128
HW = HEADS * HEAD_DIM
HGRN_CHUNK = 64
SGU_CHUNK = 128
N_GROUPS = 4
EXPERTS_PER_GROUP = 8
N_EXPERTS = N_GROUPS * EXPERTS_PER_GROUP
TOP_K = 2
N_MOD = 6
LANES = 128
SUBLANES = 8
ROUTE_LANES = LANES
MOD_ROWS = 16

MOD_COLS = 1536
PROJ_ROWS = 512
SCAN_ROWS = 256
MIX_ROWS = 512

VMEM_LIMIT = 48 * 1024 * 1024


def _cparams(*sem):
    return pltpu.CompilerParams(dimension_semantics=sem, vmem_limit_bytes=VMEM_LIMIT)


def _split2(a):
    hi = a.astype(BF16)
    lo = (a - hi.astype(F32)).astype(BF16)
    return hi, lo


def _dot(a, b):
    return jnp.dot(a, b, preferred_element_type=F32)


def _dot_nt(a, b):
    return lax.dot_general(a, b, (((1,), (1,)), ((), ())), preferred_element_type=F32)


def _dot_tn(a, b):
    return lax.dot_general(a, b, (((0,), (0,)), ((), ())), preferred_element_type=F32)


def _dot3(a, b):
    ah, al = _split2(a)
    bh, bl = _split2(b)
    return _dot(ah, bh) + (_dot(al, bh) + _dot(ah, bl))


def _silu(x):
    return x / (1.0 + jnp.exp(-x))


def _rms(x):
    return x * lax.rsqrt(jnp.mean(x * x, axis=-1, keepdims=True) + EPS)


def _full(a):
    return pl.BlockSpec(a.shape, lambda *_: (0,) * a.ndim)


def _stream_inputs(xa, xb, n_a, tm, d):
    if xb is None:
        return [xa], [pl.BlockSpec((tm, d), lambda i: (i, 0))]
    return [xa, xb], [pl.BlockSpec((tm, d), lambda i: (jnp.minimum(i, n_a - 1), 0)),
                      pl.BlockSpec((tm, d), lambda i: (jnp.maximum(i - n_a, 0), 0))]


def _mod_row(d, tiles_per_row, n_rows, j):
    return pl.BlockSpec((None, 1, d), lambda i: (jnp.minimum(i // tiles_per_row, n_rows - 1), 0, j))


def _store_token_tiles_cols(ref, x, j0, nd):
    rows, w = x.shape
    for j in range(w // LANES):
        ref[pl.ds(j0 + j, rows, stride=nd), :] = x[:, j * LANES:(j + 1) * LANES]


def _store_token_tiles(ref, x):
    _store_token_tiles_cols(ref, x, 0, x.shape[1] // LANES)


def _load_token_tiles(ref, rows, nd):
    return jnp.concatenate([ref[pl.ds(j, rows, stride=nd), :] for j in range(nd)], axis=-1)


def _mod_kernel(c_ref, w_ref, b_ref, o_ref):
    o_ref[...] = _dot3(_silu(c_ref[...]), w_ref[...]) + b_ref[...]


def _modulation(cc, w_mod, b_mod):
    depth, d, n = w_mod.shape
    tn = min(MOD_COLS, n)
    return pl.pallas_call(
        _mod_kernel,
        out_shape=jax.ShapeDtypeStruct((depth, MOD_ROWS, n), F32),
        grid=(depth, n // tn),
        in_specs=[
            pl.BlockSpec((MOD_ROWS, d), lambda l, j: (0, 0)),
            pl.BlockSpec((None, d, tn), lambda l, j: (l, 0, j)),
            pl.BlockSpec((None, 1, tn), lambda l, j: (l, 0, j)),
        ],
        out_specs=pl.BlockSpec((None, MOD_ROWS, tn), lambda l, j: (l, 0, j)),
        compiler_params=_cparams("arbitrary", "arbitrary"),
        name="modulation",
    )(cc, w_mod, b_mod.reshape(depth, 1, n))


def _log_forget(z, lb, lb_is_zero):
    ls = jnp.minimum(z, 0.0) - jnp.log(1.0 + jnp.exp(-jnp.abs(z)))
    if lb_is_zero:
        return ls
    return jnp.maximum(jnp.log(lb + (1.0 - lb) * jnp.exp(ls)), ls)


def _stream_tile(xa_ref, xb_ref, n_a_tiles):
    if xb_ref is None:
        return xa_ref[...]
    return jnp.where(pl.program_id(0) < n_a_tiles, xa_ref[...], xb_ref[...])


def _moe_residual(x_ref, y0_ref, y1_ref, rt_ref, g2_ref):
    w = rt_ref[...]
    tm, d = x_ref.shape
    nd = d // LANES
    f = w[:, 2:3] * _load_token_tiles(y0_ref, tm, nd) + w[:, 3:4] * _load_token_tiles(y1_ref, tm, nd)
    return x_ref[...] + g2_ref[...] * f


def _inproj_kernel(*refs, lb_is_zero, n_a_tiles, n_x):
    if n_x == 5:
        x = _moe_residual(*refs[:5])
    else:
        x = _stream_tile(refs[0], refs[1] if n_x == 2 else None, n_a_tiles)
    (sh_ref, sc_ref, n1_ref, w_ref, lb_ref, sgn_ref,
     q_ref, i_ref, lff_ref, lfb_ref, sg_ref, gu_ref, vn_ref, *xo_ref) = refs[n_x:]
    if xo_ref:
        xo_ref[0][...] = x
    h = _rms(x) * n1_ref[...]
    hb = (h * (1.0 + sc_ref[...]) + sh_ref[...]).astype(BF16)

    def proj(j):
        return _dot(hb, w_ref[:, j * HW:(j + 1) * HW])

    vn_ref[...] = (_rms(jax.nn.gelu(proj(6))) * sgn_ref[...]).astype(BF16)
    lff_ref[...] = _log_forget(proj(1), lb_ref[0:1, :], lb_is_zero)
    lfb_ref[...] = _log_forget(proj(2), lb_ref[1:2, :], lb_is_zero)
    gu_ref[...] = jax.nn.gelu(proj(5)).astype(BF16)
    sg_ref[...] = _silu(proj(4)).astype(BF16)
    q_ref[...] = proj(0).astype(BF16)
    i_ref[...] = proj(3).astype(BF16)


def _inproj(xa, xb, t_a, t_b, mod3, tokens_per_mod_row, n1, w_in, lb, sgu_gain, lb_is_zero, moe=None):
    d = xa.shape[1]
    tm = PROJ_ROWS
    n_a, n_b = t_a // tm, t_b // tm
    t = t_a + t_b
    nd = d // LANES
    tok = pl.BlockSpec((tm, HW), lambda i: (i, 0))
    wide = pl.BlockSpec((tm, d), lambda i: (i, 0))
    row = functools.partial(_mod_row, d, tokens_per_mod_row // tm, mod3.shape[0])
    out_shape = [jax.ShapeDtypeStruct((t, HW), dt) for dt in (BF16, BF16, F32, F32, BF16, BF16, BF16)]
    out_specs = [tok] * 7
    if moe is None:
        x_in, x_spec = _stream_inputs(xa, xb, n_a, tm, d)
    else:
        y2, ttot, route, mod3_prev = moe
        b1 = ttot // tm
        x_in = [xa, y2, y2, route, mod3_prev]
        x_spec = [wide,
                  pl.BlockSpec((tm * nd, LANES), lambda i: (i, 0)),
                  pl.BlockSpec((tm * nd, LANES), lambda i: (i + b1, 0)),
                  pl.BlockSpec((tm, ROUTE_LANES), lambda i: (i, 0)),
                  _mod_row(d, tokens_per_mod_row // tm, mod3_prev.shape[0], 5)]
        out_shape.append(jax.ShapeDtypeStruct((t, d), F32))
        out_specs.append(wide)
    return pl.pallas_call(
        functools.partial(_inproj_kernel, lb_is_zero=lb_is_zero, n_a_tiles=n_a, n_x=len(x_in)),
        out_shape=out_shape,
        grid=(n_a + n_b,),
        in_specs=x_spec + [row(0), row(1), _full(n1), _full(w_in), _full(lb), _full(sgu_gain)],
        out_specs=out_specs,
        compiler_params=_cparams("arbitrary"),
        name="inproj",
    )(*x_in, mod3, mod3, n1, w_in, lb, sgu_gain)


def _chunk_cumsum(x, reverse):
    c, w = x.shape
    g = c // SUBLANES
    x3 = x.reshape(g, SUBLANES, w)
    sub = lax.broadcasted_iota(jnp.int32, x3.shape, 1)
    for s in (1, 2, 4):
        if reverse:
            x3 = x3 + jnp.where(sub < SUBLANES - s, pltpu.roll(x3, SUBLANES - s, axis=1), 0.0)
        else:
            x3 = x3 + jnp.where(sub >= s, pltpu.roll(x3, s, axis=1), 0.0)
    edge = 0 if reverse else SUBLANES - 1
    tot = x3[:, edge:edge + 1, :]
    offs = [None] * g
    acc = jnp.zeros((1, w), F32)
    for gi in (reversed(range(g)) if reverse else range(g)):
        offs[gi] = acc
        acc = acc + tot[gi]
    x3 = x3 + jnp.stack(offs, axis=0)
    return x3.reshape(c, w)


def _scan_chunk(q_ref, i_ref, lf_ref, o_ref, st_ref, r0, reverse):
    c = HGRN_CHUNK
    rows = lax.broadcasted_iota(jnp.int32, (c, c), 0)
    cols = lax.broadcasted_iota(jnp.int32, (c, c), 1)
    incl = (cols >= rows) if reverse else (cols <= rows)
    ref_row = c // 2 if reverse else c // 2 - 1
    tot_row = 0 if reverse else c - 1
    lf = lf_ref[pl.ds(r0, c), :] * LOG2_E
    cum = _chunk_cumsum(lf, reverse)
    ref = cum[ref_row:ref_row + 1, :]
    tot = cum[tot_row:tot_row + 1, :]
    k = 1.0 - jnp.exp2(lf)
    qf = q_ref[pl.ds(r0, c), :].astype(F32)
    iv = i_ref[pl.ds(r0, c), :]
    q_in = (qf * jnp.exp2(cum - ref)).astype(BF16)
    k_in = (k * jnp.exp2(ref - cum)).astype(BF16)
    k_st = (k * jnp.exp2(tot - cum)).astype(BF16)
    q_st = (qf * jnp.exp2(cum)).astype(BF16)
    dec = jnp.exp2(tot)
    for h in range(HEADS):
        sl = slice(h * HEAD_DIM, (h + 1) * HEAD_DIM)
        sc = _dot_nt(q_in[:, sl], k_in[:, sl])
        sc = jnp.where(incl, sc, 0.0).astype(BF16)
        st = st_ref[h]
        o_ref[pl.ds(r0, c), sl] = _dot(sc, iv[:, sl]) + _dot_nt(q_st[:, sl], st.astype(BF16))
        st_ref[h] = st * dec[:, sl] + _dot_tn(iv[:, sl], k_st[:, sl])


def _hgrn_kernel(qf_ref, if_ref, lff_ref, qb_ref, ib_ref, lfb_ref, of_ref, ob_ref, stf_ref, stb_ref, *, tt):
    @pl.when(pl.program_id(1) == 0)
    def _():
        stf_ref[...] = jnp.zeros_like(stf_ref)
        stb_ref[...] = jnp.zeros_like(stb_ref)

    nchunks = tt // HGRN_CHUNK
    for ci in range(nchunks):
        _scan_chunk(qf_ref, if_ref, lff_ref, of_ref, stf_ref, ci * HGRN_CHUNK, False)
        _scan_chunk(qb_ref, ib_ref, lfb_ref, ob_ref, stb_ref, (nchunks - 1 - ci) * HGRN_CHUNK, True)


def _hgrn(q, i, lf_f, lf_b, batch, seq, ctx_len):
    t = q.shape[0]
    tt = SCAN_ROWS
    nt, nc = seq // tt, ctx_len // tt
    ctx0 = batch * nt

    def fwd(b, s):
        return (jnp.where(s < nc, ctx0 + b * nc + s, b * nt + (s - nc)), 0)

    def bwd(b, s):
        return (jnp.where(s < nc, ctx0 + b * nc + (nc - 1 - s), b * nt + (nt - 1 - (s - nc))), 0)

    tf, tb = pl.BlockSpec((tt, HW), fwd), pl.BlockSpec((tt, HW), bwd)
    return pl.pallas_call(
        functools.partial(_hgrn_kernel, tt=tt),
        out_shape=[jax.ShapeDtypeStruct((t, HW), F32)] * 2,
        grid=(batch, nc + nt),
        in_specs=[tf, tf, tf, tb, tb, tb],
        out_specs=[tf, tb],
        scratch_shapes=[pltpu.VMEM((HEADS, HEAD_DIM, HEAD_DIM), F32)] * 2,
        compiler_params=_cparams("arbitrary", "arbitrary"),
        name="hgrn",
    )(q, i, lf_f, q, i, lf_b)


def _route(logits):
    lane = lax.broadcasted_iota(jnp.int32, logits.shape, 1).astype(F32)
    neg = -jnp.inf
    is_group = lane < N_GROUPS
    gl = jnp.where(is_group, logits, neg)
    gmax = jnp.max(gl, axis=-1, keepdims=True)
    g_sel = jnp.min(jnp.where(gl == gmax, lane, float(ROUTE_LANES)), axis=-1, keepdims=True)
    den = jnp.sum(jnp.where(is_group, jnp.exp(logits - gmax), 0.0), axis=-1, keepdims=True)
    p_sel = 1.0 / den
    first = N_GROUPS + EXPERTS_PER_GROUP * g_sel
    el = jnp.where((lane >= first) & (lane < first + EXPERTS_PER_GROUP), logits, neg)
    t1 = jnp.max(el, axis=-1, keepdims=True)
    i1 = jnp.min(jnp.where(el == t1, lane, float(ROUTE_LANES)), axis=-1, keepdims=True)
    el2 = jnp.where(lane == i1, neg, el)
    t2 = jnp.max(el2, axis=-1, keepdims=True)
    i2 = jnp.min(jnp.where(el2 == t2, lane, float(ROUTE_LANES)), axis=-1, keepdims=True)
    e2 = jnp.exp(t2 - t1)
    w1 = p_sel / (1.0 + e2)
    w2 = p_sel * e2 / (1.0 + e2)
    rec = jnp.where(lane == 0.0, i1 - N_GROUPS, 0.0)
    rec = jnp.where(lane == 1.0, i2 - N_GROUPS, rec)
    rec = jnp.where(lane == 2.0, w1, rec)
    return jnp.where(lane == 3.0, w2, rec)


def _mixer_kernel(of_ref, ob_ref, sg_ref, gu_ref, vn_ref, *refs, tm, n_a_tiles, two_inputs):
    xa_ref, xb_ref = (refs[0], refs[1]) if two_inputs else (refs[0], None)
    (hgain_ref, ws_ref, bs_ref, wo_ref, g1_ref, n2_ref, sh2_ref, sc2_ref, wrc_ref, br_ref,
     xo_ref, h2_ref, rt_ref, rtt_ref, cat_ref) = refs[2 if two_inputs else 1:]
    o = of_ref[...] + ob_ref[...]
    for h in range(HEADS):
        sl = slice(h * HEAD_DIM, (h + 1) * HEAD_DIM)
        hg = _rms(o[:, sl]) * hgain_ref[:, sl] * sg_ref[:, sl].astype(F32)
        cat_ref[:, sl] = hg.astype(BF16)
    for cc in range(tm // SGU_CHUNK):
        rows = slice(cc * SGU_CHUNK, (cc + 1) * SGU_CHUNK)
        for h in range(HEADS):
            sl = slice(h * HEAD_DIM, (h + 1) * HEAD_DIM)
            mixed = _dot(ws_ref[h], vn_ref[rows, sl]) + bs_ref[h]
            cat_ref[rows, HW + h * HEAD_DIM:HW + (h + 1) * HEAD_DIM] = (
                gu_ref[rows, sl].astype(F32) * mixed).astype(BF16)
    xn = _stream_tile(xa_ref, xb_ref, n_a_tiles) + g1_ref[...] * _dot(cat_ref[...], wo_ref[...])
    xo_ref[...] = xn
    h2 = _rms(xn) * n2_ref[...]
    h2 = h2 * (1.0 + sc2_ref[...]) + sh2_ref[...]
    _store_token_tiles(h2_ref, h2)
    hi, lo = _split2(h2)
    both = _dot(hi, wrc_ref[...])
    logits = (both[:, :ROUTE_LANES] + both[:, ROUTE_LANES:]) + _dot(lo, wrc_ref[:, :ROUTE_LANES]) + br_ref[...]
    rec = _route(logits)
    rt_ref[...] = rec
    rtt_ref[...] = rec.T[:SUBLANES, :]


def _mixer(o_f, o_b, sg, gu, vn, xa, xb, t_a, t_b, mod3, tokens_per_mod_row, hgain, w_s, b_s, w_out, n2,
           wr_cat, br):
    d = xa.shape[1]
    tm = MIX_ROWS
    n_a, n_b = t_a // tm, t_b // tm
    t = t_a + t_b
    nd = d // LANES
    row = functools.partial(_mod_row, d, tokens_per_mod_row // tm, mod3.shape[0])
    tok = pl.BlockSpec((tm, HW), lambda i: (i, 0))
    wide = pl.BlockSpec((tm, d), lambda i: (i, 0))
    x_in, x_spec = _stream_inputs(xa, xb, n_a, tm, d)
    return pl.pallas_call(
        functools.partial(_mixer_kernel, tm=tm, n_a_tiles=n_a, two_inputs=xb is not None),
        out_shape=[jax.ShapeDtypeStruct((t, d), F32), jax.ShapeDtypeStruct((t * nd, LANES), F32),
                   jax.ShapeDtypeStruct((t, ROUTE_LANES), F32), jax.ShapeDtypeStruct((SUBLANES, t), F32)],
        grid=(n_a + n_b,),
        in_specs=[tok, tok, tok, tok, tok] + x_spec + [_full(hgain), _full(w_s), _full(b_s), _full(w_out),
                  row(2), _full(n2), row(3), row(4), _full(wr_cat), _full(br)],
        out_specs=[wide, pl.BlockSpec((tm * nd, LANES), lambda i: (i, 0)),
                   pl.BlockSpec((tm, ROUTE_LANES), lambda i: (i, 0)), pl.BlockSpec((SUBLANES, tm), lambda i: (0, i))],
        scratch_shapes=[pltpu.VMEM((tm, 2 * HW), BF16)],
        compiler_params=_cparams("arbitrary"),
        name="mixer",
    )(o_f, o_b, sg, gu, vn, *x_in, hgain, w_s, b_s, w_out, mod3, n2, mod3, mod3, wr_cat, br)


MOE_ROWS = 256
MOE_LAG = 3
RING = 3
DMA_GROUPS = 8


def _moe_kernel(last_ref, sexp_ref, tab_hbm, h_hbm, wg_ref, wu_ref, wd_ref, y_hbm,
                idx_ref, xbuf, ybuf, zbuf, xb_ref, hm_ref, wgb, wub, wdb, sem_idx, sem_g, sem_s, sem_z):
    i = pl.program_id(0)
    last = last_ref[0]
    bm = MOE_ROWS
    de = wgb.shape[1]
    d = wgb.shape[0]
    nd = d // LANES

    def idx_copy(step, slot):
        return pltpu.make_async_copy(tab_hbm.at[step], idx_ref.at[slot], sem_idx.at[slot])

    def gathered(slot):
        return pltpu.make_async_copy(h_hbm.at[pl.ds(0, bm * nd)], xbuf.at[slot], sem_g.at[slot])

    def scattered(slot):
        return pltpu.make_async_copy(ybuf.at[slot], y_hbm.at[pl.ds(0, bm * nd)], sem_s.at[slot])

    def step(k):
        gslot = k
        cslot = (k + 1) % RING
        sslot = k

        if k == 0:
            @pl.when(i == 0)
            def _():
                xbuf[...] = jnp.zeros_like(xbuf)
                ybuf[...] = jnp.zeros_like(ybuf)
                zbuf[...] = jnp.zeros_like(zbuf)
                idx_copy(0, 0).start()

        idx_copy(i, k).wait()

        @pl.when(i < last)
        def _():
            idx_copy(i + 1, (k + 1) % RING).start()

        @pl.when(i >= 2)
        def _():
            gathered(cslot).wait()
            scattered(cslot).wait()

        @pl.when((i == 0) | (sexp_ref[i] != sexp_ref[jnp.maximum(i - 1, 0)]))
        def _():
            wgb[...] = wg_ref[...].astype(BF16)
            wub[...] = wu_ref[...].astype(BF16)
            wdb[...] = wd_ref[...].astype(BF16)

        per = bm // (DMA_GROUPS // 2)

        def scatter_group(g):
            for r in range(g * per, (g + 1) * per):
                dst = pl.multiple_of(idx_ref[k, 1, r], nd)
                pltpu.make_async_copy(ybuf.at[sslot, pl.ds(r * nd, nd)], y_hbm.at[pl.ds(dst, nd)],
                                      sem_s.at[sslot]).start(priority=r % 2)

        def gather_group(g):
            for r in range(g * per, (g + 1) * per):
                src = pl.multiple_of(idx_ref[k, 0, r], nd)
                pltpu.make_async_copy(h_hbm.at[pl.ds(src, nd)], xbuf.at[gslot, pl.ds(r * nd, nd)],
                                      sem_g.at[gslot]).start(priority=r % 2)

        dma_groups = [functools.partial(scatter_group, g) for g in range(DMA_GROUPS // 2)]
        dma_groups += [functools.partial(gather_group, g) for g in range(DMA_GROUPS // 2)]

        def issue_some():
            if dma_groups:
                dma_groups.pop(0)()

        xsrc = xbuf.at[cslot]
        for j in range(nd):
            xb_ref[:, j * LANES:(j + 1) * LANES] = xsrc[pl.ds(j, bm, stride=nd), :].astype(BF16)
        nh = 2
        for j in range(nh):
            cs = slice(j * de // nh, (j + 1) * de // nh)
            issue_some()
            gate = _dot(xb_ref[...], wgb[:, cs])
            issue_some()
            hm_ref[:, cs] = (_silu(gate) * _dot(xb_ref[...], wub[:, cs])).astype(BF16)
        ydst = ybuf.at[cslot]
        n_down = min(4, nd)
        for j in range(n_down):
            issue_some()
            _store_token_tiles_cols(ydst, _dot(hm_ref[...], wdb[:, j * d // n_down:(j + 1) * d // n_down]),
                                    j * nd // n_down, nd)
        while dma_groups:
            issue_some()

        @pl.when(i == last)
        def _():
            gathered(gslot).wait()
            gathered((k + 2) % RING).wait()
            scattered(sslot).wait()
            scattered((k + 2) % RING).wait()

    for k in range(RING):
        pl.when((i <= last) & (i % RING == k))(functools.partial(step, k))

    @pl.when(i > last)
    def _():
        fill = pltpu.make_async_copy(zbuf, y_hbm.at[pl.ds((i - MOE_LAG) * (bm * nd), bm * nd)], sem_z.at[0])
        fill.start()
        fill.wait()


def _moe(h2, tab, last, sexp, n_out_rows, w_gate, w_up, w_down, layer):
    d, de = w_gate.shape[-2:]
    nd = d // LANES
    n_steps = tab.shape[0]
    grid_spec = pltpu.PrefetchScalarGridSpec(
        num_scalar_prefetch=2,
        grid=(n_steps,),
        in_specs=[
            pl.BlockSpec(memory_space=pl.ANY),
            pl.BlockSpec(memory_space=pl.ANY),
            pl.BlockSpec((None, None, d, de), lambda i, la, se: (layer, se[i], 0, 0)),
            pl.BlockSpec((None, None, d, de), lambda i, la, se: (layer, se[i], 0, 0)),
            pl.BlockSpec((None, None, de, d), lambda i, la, se: (layer, se[i], 0, 0)),
        ],
        out_specs=pl.BlockSpec(memory_space=pl.ANY),
        scratch_shapes=[
            pltpu.SMEM((RING, 2, MOE_ROWS), jnp.int32),
            pltpu.VMEM((RING, MOE_ROWS * nd, LANES), F32),
            pltpu.VMEM((RING, MOE_ROWS * nd, LANES), F32),
            pltpu.VMEM((MOE_ROWS * nd, LANES), F32),
            pltpu.VMEM((MOE_ROWS, d), BF16),
            pltpu.VMEM((MOE_ROWS, de), BF16),
            pltpu.VMEM((d, de), BF16),
            pltpu.VMEM((d, de), BF16),
            pltpu.VMEM((de, d), BF16),
            pltpu.SemaphoreType.DMA((RING,)),
            pltpu.SemaphoreType.DMA((RING,)),
            pltpu.SemaphoreType.DMA((RING,)),
            pltpu.SemaphoreType.DMA((1,)),
        ],
    )
    return pl.pallas_call(
        _moe_kernel,
        out_shape=jax.ShapeDtypeStruct((n_out_rows * nd, LANES), F32),
        grid_spec=grid_spec,
        compiler_params=_cparams("arbitrary"),
        name="moe",
    )(last, sexp, tab, h2, w_gate, w_up, w_down)


def _dispatch_tables(expert_flat, ttot, nd):
    bm = MOE_ROWS
    n_slots = expert_flat.shape[0]
    n_blocks = -(-n_slots // bm) + N_EXPERTS
    n_steps = n_blocks + MOE_LAG
    n_main = 1 << (n_slots.bit_length() - 1)
    pieces = []
    for lo, n in ((0, n_main), (n_main, n_slots - n_main)):
        if n:
            e = expert_flat[lo:lo + n]
            _, order = lax.sort_key_val(e, lax.iota(jnp.int32, n))
            cnt = jnp.sum(e[:, None] == jnp.arange(N_EXPERTS, dtype=jnp.int32)[None, :], axis=0, dtype=jnp.int32)
            pieces.append((order + lo, cnt))
    counts = sum(cnt for _, cnt in pieces)
    padded = (counts + bm - 1) // bm * bm
    pad_end = jnp.cumsum(padded)
    pad_start = pad_end - padded
    start = jnp.cumsum(counts) - counts
    blk_row0 = jnp.arange(n_blocks, dtype=jnp.int32) * bm
    bexp = jnp.minimum(jnp.sum(pad_end[None, :] <= blk_row0[:, None], axis=1), N_EXPERTS - 1).astype(jnp.int32)
    is_exp = bexp[:, None] == jnp.arange(N_EXPERTS, dtype=jnp.int32)[None, :]

    def of_block(per_expert):
        return jnp.sum(jnp.where(is_exp, per_expert[None, :], 0), axis=1, dtype=jnp.int32)

    blk_off, blk_cnt, blk_pad0 = of_block(-pad_start) + blk_row0, of_block(counts), of_block(start + counts)
    blk_piece = [(of_block(jnp.cumsum(cnt) - cnt), of_block(cnt)) for _, cnt in pieces]
    blk_off, blk_cnt, blk_pad0, blk_piece = lax.optimization_barrier((blk_off, blk_cnt, blk_pad0, blk_piece))
    lane = jnp.arange(bm, dtype=jnp.int32)[None, :]
    off = blk_off[:, None] + lane
    valid = off < blk_cnt[:, None]
    src, rem, base = jnp.zeros_like(off), off, 0
    for (order, _), (first, cnt_here) in zip(pieces, blk_piece):
        here = (rem >= 0) & (rem < cnt_here[:, None])
        src = jnp.where(here, base + first[:, None] + rem, src)
        rem = rem - cnt_here[:, None]
        base += order.shape[0]
    slot = jnp.concatenate([order for order, _ in pieces])[jnp.clip(src, 0, n_slots - 1)]
    pad_rank = blk_row0[:, None] + lane - blk_pad0[:, None]
    gsrc = jnp.where(valid, slot % ttot, 0)
    sdst = jnp.where(valid, slot, n_slots + pad_rank)
    spare = n_blocks * bm + jnp.arange(MOE_LAG * bm, dtype=jnp.int32).reshape(MOE_LAG, bm)
    gtab = jnp.concatenate([gsrc, jnp.zeros((MOE_LAG, bm), jnp.int32)], axis=0)
    stab = jnp.concatenate([spare, sdst], axis=0)
    tab = (jnp.stack([gtab, stab], axis=1) * nd).astype(jnp.int32)
    n_used = jnp.sum(padded) // bm
    last = (n_used + MOE_LAG - 1).astype(jnp.int32).reshape(1)
    sexp = bexp[jnp.clip(jnp.arange(n_steps) - (MOE_LAG - 1), 0, n_blocks - 1)]
    return tab, last, sexp, n_steps * bm


def _final_kernel(x_ref, y0_ref, y1_ref, rt_ref, g2_ref, fn_ref, o_ref):
    o_ref[...] = _rms(_moe_residual(x_ref, y0_ref, y1_ref, rt_ref, g2_ref)) * fn_ref[...]


def _final(x2d, y2, ttot, route, mod3, tokens_per_mod_row, final_norm):
    t, d = x2d.shape
    tm = MIX_ROWS
    b1 = ttot // tm
    nd = d // LANES
    return pl.pallas_call(
        _final_kernel,
        out_shape=jax.ShapeDtypeStruct((t, d), F32),
        grid=(t // tm,),
        in_specs=[pl.BlockSpec((tm, d), lambda i: (i, 0)),
                  pl.BlockSpec((tm * nd, LANES), lambda i: (i, 0)),
                  pl.BlockSpec((tm * nd, LANES), lambda i: (i + b1, 0)),
                  pl.BlockSpec((tm, ROUTE_LANES), lambda i: (i, 0)),
                  _mod_row(d, tokens_per_mod_row // tm, mod3.shape[0], 5),
                  _full(final_norm)],
        out_specs=pl.BlockSpec((tm, d), lambda i: (i, 0)),
        compiler_params=_cparams("arbitrary"),
        name="final",
    )(x2d, y2, y2, route, mod3, final_norm)


def kernel(x, c, ctx, c_ctx, norm1, norm2, w_mod, b_mod, w_in, lb_logits, hgrn_norm, sgu_norm, sgu_w, sgu_b,
           w_out, w_group, b_group, w_router, b_router, w_gate, w_up, w_down, final_norm):
    b, l, d = x.shape
    lc = ctx.shape[1]
    depth = w_mod.shape[0]
    t_lat, t_ctx = b * l, b * lc
    nd = d // LANES
    assert d % LANES == 0 and l % max(PROJ_ROWS, MIX_ROWS, SCAN_ROWS) == 0 and lc % SCAN_ROWS == 0, (d, l, lc)
    assert t_ctx % max(PROJ_ROWS, MIX_ROWS) == 0 and b < MOD_ROWS, (b, lc)
    assert w_in.shape[-1] == 7 * HW and w_out.shape[-2] == 2 * HW, (w_in.shape, w_out.shape)

    lb_cum = jnp.cumsum(jax.nn.softmax(lb_logits.astype(F32), axis=0), axis=0)
    lower_bound = jnp.maximum(lb_cum - lb_cum[0:1], 0.0)

    cc = jnp.zeros((MOD_ROWS, d), F32).at[:b].set(c).at[b].set(c_ctx)
    mod = _modulation(cc, w_mod, b_mod)

    w_route = jnp.concatenate([w_group, w_router], axis=-1)
    w_route = jnp.pad(w_route, ((0, 0), (0, 0), (0, ROUTE_LANES - w_route.shape[-1])))
    wr_hi = w_route.astype(BF16)
    wr_cat = jnp.concatenate([wr_hi, (w_route - wr_hi.astype(F32)).astype(BF16)], axis=-1)
    b_route = jnp.concatenate([b_group, b_router], axis=-1)
    b_route = jnp.pad(b_route, ((0, 0), (0, ROUTE_LANES - b_route.shape[-1])))[:, None, :]
    b_s = jnp.broadcast_to(sgu_b[..., None], sgu_b.shape + (HEAD_DIM,)).astype(F32)

    w_in_b, w_out_b, sgu_w_b = w_in.astype(BF16), w_out.astype(BF16), sgu_w.astype(BF16)

    xa, xb, t_a, t_b = x.reshape(t_lat, d), ctx.reshape(t_ctx, d), t_lat, t_ctx
    fn = final_norm.reshape(1, d)
    moe = None

    for layer in range(depth):
        last = layer == depth - 1
        mod3 = mod[layer, :b + 1].reshape(b + 1, 1, N_MOD * d)
        n1 = norm1[layer].reshape(1, d)
        n2 = norm2[layer].reshape(1, d)
        sgn = sgu_norm[layer].reshape(1, HW)
        hgain = hgrn_norm[layer].reshape(1, HW)

        q, i, lf_f, lf_b, sg, gu, vn, *x_new = _inproj(xa, xb, t_a, t_b, mod3, l, n1, w_in_b[layer],
                                                       lower_bound[layer], sgn, layer == 0, moe)
        if x_new:
            xa = x_new[0]
        o_f, o_b = _hgrn(q, i, lf_f, lf_b, b, l, lc)

        if last:
            xb, t_a, t_b = None, t_lat, 0
        ttot = t_a + t_b
        xs, h2, route, route_t = _mixer(o_f, o_b, sg, gu, vn, xa, xb, t_a, t_b, mod3, l, hgain, sgu_w_b[layer],
                                        b_s[layer], w_out_b[layer], n2, wr_cat[layer], b_route[layer])

        expert_flat = route_t[:TOP_K].astype(jnp.int32).reshape(-1)
        tab, last_step, sexp, n_out_rows = _dispatch_tables(expert_flat, ttot, nd)
        y2 = _moe(h2, tab, last_step, sexp, n_out_rows, w_gate, w_up, w_down, layer)

        xa, xb, t_a, t_b = xs, None, ttot, 0
        moe = (y2, ttot, route, mod3)

    return _final(xs, y2, ttot, route, mod3, l, fn).reshape(b, l, d)
```

```python
import functools

import jax
import jax.numpy as jnp
from jax import lax
from jax.experimental import pallas as pl
from jax.experimental.pallas import tpu as pltpu

F32 = jnp.float32
BF16 = jnp.bfloat16

EPS = 1e-6
LOG2_E = 1.4426950408889634
HEADS = 4
HEAD_DIM = 128
HW = HEADS * HEAD_DIM
HGRN_CHUNK = 64
SGU_CHUNK = 128
N_GROUPS = 4
EXPERTS_PER_GROUP = 8
N_EXPERTS = N_GROUPS * EXPERTS_PER_GROUP
TOP_K = 2
N_MOD = 6
LANES = 128
SUBLANES = 8
ROUTE_LANES = LANES
MOD_ROWS = 16

MOD_COLS = 1536
PROJ_ROWS = 512
SCAN_ROWS = 256
MIX_ROWS = 512

VMEM_LIMIT = 48 * 1024 * 1024


def _cparams(*sem):
    return pltpu.CompilerParams(dimension_semantics=sem, vmem_limit_bytes=VMEM_LIMIT)


def _split2(a):
    hi = a.astype(BF16)
    lo = (a - hi.astype(F32)).astype(BF16)
    return hi, lo


def _dot(a, b):
    return jnp.dot(a, b, preferred_element_type=F32)


def _dot_nt(a, b):
    return lax.dot_general(a, b, (((1,), (1,)), ((), ())), preferred_element_type=F32)


def _dot_tn(a, b):
    return lax.dot_general(a, b, (((0,), (0,)), ((), ())), preferred_element_type=F32)


def _dot3(a, b):
    ah, al = _split2(a)
    bh, bl = _split2(b)
    return _dot(ah, bh) + (_dot(al, bh) + _dot(ah, bl))


def _silu(x):
    return x / (1.0 + jnp.exp(-x))


def _rms(x):
    return x * lax.rsqrt(jnp.mean(x * x, axis=-1, keepdims=True) + EPS)


def _full(a):
    return pl.BlockSpec(a.shape, lambda *_: (0,) * a.ndim)


def _stream_inputs(xa, xb, n_a, tm, d):
    if xb is None:
        return [xa], [pl.BlockSpec((tm, d), lambda i: (i, 0))]
    return [xa, xb], [pl.BlockSpec((tm, d), lambda i: (jnp.minimum(i, n_a - 1), 0)),
                      pl.BlockSpec((tm, d), lambda i: (jnp.maximum(i - n_a, 0), 0))]


def _mod_row(d, tiles_per_row, n_rows, j):
    return pl.BlockSpec((None, 1, d), lambda i: (jnp.minimum(i // tiles_per_row, n_rows - 1), 0, j))


def _store_token_tiles_cols(ref, x, j0, nd):
    rows, w = x.shape
    for j in range(w // LANES):
        ref[pl.ds(j0 + j, rows, stride=nd), :] = x[:, j * LANES:(j + 1) * LANES]


def _store_token_tiles(ref, x):
    _store_token_tiles_cols(ref, x, 0, x.shape[1] // LANES)


def _load_token_tiles(ref, rows, nd):
    return jnp.concatenate([ref[pl.ds(j, rows, stride=nd), :] for j in range(nd)], axis=-1)


def _mod_kernel(c_ref, w_ref, b_ref, o_ref):
    o_ref[...] = _dot3(_silu(c_ref[...]), w_ref[...]) + b_ref[...]


def _modulation(cc, w_mod, b_mod):
    depth, d, n = w_mod.shape
    tn = min(MOD_COLS, n)
    return pl.pallas_call(
        _mod_kernel,
        out_shape=jax.ShapeDtypeStruct((depth, MOD_ROWS, n), F32),
        grid=(depth, n // tn),
        in_specs=[
            pl.BlockSpec((MOD_ROWS, d), lambda l, j: (0, 0)),
            pl.BlockSpec((None, d, tn), lambda l, j: (l, 0, j)),
            pl.BlockSpec((None, 1, tn), lambda l, j: (l, 0, j)),
        ],
        out_specs=pl.BlockSpec((None, MOD_ROWS, tn), lambda l, j: (l, 0, j)),
        compiler_params=_cparams("arbitrary", "arbitrary"),
        name="modulation",
    )(cc, w_mod, b_mod.reshape(depth, 1, n))


def _log_forget(z, lb, lb_is_zero):
    ls = jnp.minimum(z, 0.0) - jnp.log(1.0 + jnp.exp(-jnp.abs(z)))
    if lb_is_zero:
        return ls
    return jnp.maximum(jnp.log(lb + (1.0 - lb) * jnp.exp(ls)), ls)


def _stream_tile(xa_ref, xb_ref, n_a_tiles):
    if xb_ref is None:
        return xa_ref[...]
    return jnp.where(pl.program_id(0) < n_a_tiles, xa_ref[...], xb_ref[...])


def _moe_residual(x_ref, y0_ref, y1_ref, rt_ref, g2_ref):
    w = rt_ref[...]
    tm, d = x_ref.shape
    nd = d // LANES
    f = w[:, 2:3] * _load_token_tiles(y0_ref, tm, nd) + w[:, 3:4] * _load_token_tiles(y1_ref, tm, nd)
    return x_ref[...] + g2_ref[...] * f


def _inproj_kernel(*refs, lb_is_zero, n_a_tiles, n_x):
    if n_x == 5:
        x = _moe_residual(*refs[:5])
    else:
        x = _stream_tile(refs[0], refs[1] if n_x == 2 else None, n_a_tiles)
    (sh_ref, sc_ref, n1_ref, w_ref, lb_ref, sgn_ref,
     q_ref, i_ref, lff_ref, lfb_ref, sg_ref, gu_ref, vn_ref, *xo_ref) = refs[n_x:]
    if xo_ref:
        xo_ref[0][...] = x
    h = _rms(x) * n1_ref[...]
    hb = (h * (1.0 + sc_ref[...]) + sh_ref[...]).astype(BF16)

    def proj(j):
        return _dot(hb, w_ref[:, j * HW:(j + 1) * HW])

    vn_ref[...] = (_rms(jax.nn.gelu(proj(6))) * sgn_ref[...]).astype(BF16)
    lff_ref[...] = _log_forget(proj(1), lb_ref[0:1, :], lb_is_zero)
    lfb_ref[...] = _log_forget(proj(2), lb_ref[1:2, :], lb_is_zero)
    gu_ref[...] = jax.nn.gelu(proj(5)).astype(BF16)
    sg_ref[...] = _silu(proj(4)).astype(BF16)
    q_ref[...] = proj(0).astype(BF16)
    i_ref[...] = proj(3).astype(BF16)


def _inproj(xa, xb, t_a, t_b, mod3, tokens_per_mod_row, n1, w_in, lb, sgu_gain, lb_is_zero, moe=None):
    d = xa.shape[1]
    tm = PROJ_ROWS
    n_a, n_b = t_a // tm, t_b // tm
    t = t_a + t_b
    nd = d // LANES
    tok = pl.BlockSpec((tm, HW), lambda i: (i, 0))
    wide = pl.BlockSpec((tm, d), lambda i: (i, 0))
    row = functools.partial(_mod_row, d, tokens_per_mod_row // tm, mod3.shape[0])
    out_shape = [jax.ShapeDtypeStruct((t, HW), dt) for dt in (BF16, BF16, F32, F32, BF16, BF16, BF16)]
    out_specs = [tok] * 7
    if moe is None:
        x_in, x_spec = _stream_inputs(xa, xb, n_a, tm, d)
    else:
        y2, ttot, route, mod3_prev = moe
        b1 = ttot // tm
        x_in = [xa, y2, y2, route, mod3_prev]
        x_spec = [wide,
                  pl.BlockSpec((tm * nd, LANES), lambda i: (i, 0)),
                  pl.BlockSpec((tm * nd, LANES), lambda i: (i + b1, 0)),
                  pl.BlockSpec((tm, ROUTE_LANES), lambda i: (i, 0)),
                  _mod_row(d, tokens_per_mod_row // tm, mod3_prev.shape[0], 5)]
        out_shape.append(jax.ShapeDtypeStruct((t, d), F32))
        out_specs.append(wide)
    return pl.pallas_call(
        functools.partial(_inproj_kernel, lb_is_zero=lb_is_zero, n_a_tiles=n_a, n_x=len(x_in)),
        out_shape=out_shape,
        grid=(n_a + n_b,),
        in_specs=x_spec + [row(0), row(1), _full(n1), _full(w_in), _full(lb), _full(sgu_gain)],
        out_specs=out_specs,
        compiler_params=_cparams("arbitrary"),
        name="inproj",
    )(*x_in, mod3, mod3, n1, w_in, lb, sgu_gain)


def _chunk_cumsum(x, reverse):
    c, w = x.shape
    g = c // SUBLANES
    x3 = x.reshape(g, SUBLANES, w)
    sub = lax.broadcasted_iota(jnp.int32, x3.shape, 1)
    for s in (1, 2, 4):
        if reverse:
            x3 = x3 + jnp.where(sub < SUBLANES - s, pltpu.roll(x3, SUBLANES - s, axis=1), 0.0)
        else:
            x3 = x3 + jnp.where(sub >= s, pltpu.roll(x3, s, axis=1), 0.0)
    edge = 0 if reverse else SUBLANES - 1
    tot = x3[:, edge:edge + 1, :]
    offs = [None] * g
    acc = jnp.zeros((1, w), F32)
    for gi in (reversed(range(g)) if reverse else range(g)):
        offs[gi] = acc
        acc = acc + tot[gi]
    x3 = x3 + jnp.stack(offs, axis=0)
    return x3.reshape(c, w)


def _scan_chunk(q_ref, i_ref, lf_ref, o_ref, st_ref, r0, reverse):
    c = HGRN_CHUNK
    rows = lax.broadcasted_iota(jnp.int32, (c, c), 0)
    cols = lax.broadcasted_iota(jnp.int32, (c, c), 1)
    incl = (cols >= rows) if reverse else (cols <= rows)
    ref_row = c // 2 if reverse else c // 2 - 1
    tot_row = 0 if reverse else c - 1
    lf = lf_ref[pl.ds(r0, c), :] * LOG2_E
    cum = _chunk_cumsum(lf, reverse)
    ref = cum[ref_row:ref_row + 1, :]
    tot = cum[tot_row:tot_row + 1, :]
    k = 1.0 - jnp.exp2(lf)
    qf = q_ref[pl.ds(r0, c), :].astype(F32)
    iv = i_ref[pl.ds(r0, c), :]
    q_in = (qf * jnp.exp2(cum - ref)).astype(BF16)
    k_in = (k * jnp.exp2(ref - cum)).astype(BF16)
    k_st = (k * jnp.exp2(tot - cum)).astype(BF16)
    q_st = (qf * jnp.exp2(cum)).astype(BF16)
    dec = jnp.exp2(tot)
    for h in range(HEADS):
        sl = slice(h * HEAD_DIM, (h + 1) * HEAD_DIM)
        sc = _dot_nt(q_in[:, sl], k_in[:, sl])
        sc = jnp.where(incl, sc, 0.0).astype(BF16)
        st = st_ref[h]
        o_ref[pl.ds(r0, c), sl] = _dot(sc, iv[:, sl]) + _dot_nt(q_st[:, sl], st.astype(BF16))
        st_ref[h] = st * dec[:, sl] + _dot_tn(iv[:, sl], k_st[:, sl])


def _hgrn_kernel(qf_ref, if_ref, lff_ref, qb_ref, ib_ref, lfb_ref, of_ref, ob_ref, stf_ref, stb_ref, *, tt):
    @pl.when(pl.program_id(1) == 0)
    def _():
        stf_ref[...] = jnp.zeros_like(stf_ref)
        stb_ref[...] = jnp.zeros_like(stb_ref)

    nchunks = tt // HGRN_CHUNK
    for ci in range(nchunks):
        _scan_chunk(qf_ref, if_ref, lff_ref, of_ref, stf_ref, ci * HGRN_CHUNK, False)
        _scan_chunk(qb_ref, ib_ref, lfb_ref, ob_ref, stb_ref, (nchunks - 1 - ci) * HGRN_CHUNK, True)


def _hgrn(q, i, lf_f, lf_b, batch, seq, ctx_len):
    t = q.shape[0]
    tt = SCAN_ROWS
    nt, nc = seq // tt, ctx_len // tt
    ctx0 = batch * nt

    def fwd(b, s):
        return (jnp.where(s < nc, ctx0 + b * nc + s, b * nt + (s - nc)), 0)

    def bwd(b, s):
        return (jnp.where(s < nc, ctx0 + b * nc + (nc - 1 - s), b * nt + (nt - 1 - (s - nc))), 0)

    tf, tb = pl.BlockSpec((tt, HW), fwd), pl.BlockSpec((tt, HW), bwd)
    return pl.pallas_call(
        functools.partial(_hgrn_kernel, tt=tt),
        out_shape=[jax.ShapeDtypeStruct((t, HW), F32)] * 2,
        grid=(batch, nc + nt),
        in_specs=[tf, tf, tf, tb, tb, tb],
        out_specs=[tf, tb],
        scratch_shapes=[pltpu.VMEM((HEADS, HEAD_DIM, HEAD_DIM), F32)] * 2,
        compiler_params=_cparams("arbitrary", "arbitrary"),
        name="hgrn",
    )(q, i, lf_f, q, i, lf_b)


def _route(logits):
    lane = lax.broadcasted_iota(jnp.int32, logits.shape, 1).astype(F32)
    neg = -jnp.inf
    is_group = lane < N_GROUPS
    gl = jnp.where(is_group, logits, neg)
    gmax = jnp.max(gl, axis=-1, keepdims=True)
    g_sel = jnp.min(jnp.where(gl == gmax, lane, float(ROUTE_LANES)), axis=-1, keepdims=True)
    den = jnp.sum(jnp.where(is_group, jnp.exp(logits - gmax), 0.0), axis=-1, keepdims=True)
    p_sel = 1.0 / den
    first = N_GROUPS + EXPERTS_PER_GROUP * g_sel
    el = jnp.where((lane >= first) & (lane < first + EXPERTS_PER_GROUP), logits, neg)
    t1 = jnp.max(el, axis=-1, keepdims=True)
    i1 = jnp.min(jnp.where(el == t1, lane, float(ROUTE_LANES)), axis=-1, keepdims=True)
    el2 = jnp.where(lane == i1, neg, el)
    t2 = jnp.max(el2, axis=-1, keepdims=True)
    i2 = jnp.min(jnp.where(el2 == t2, lane, float(ROUTE_LANES)), axis=-1, keepdims=True)
    e2 = jnp.exp(t2 - t1)
    w1 = p_sel / (1.0 + e2)
    w2 = p_sel * e2 / (1.0 + e2)
    rec = jnp.where(lane == 0.0, i1 - N_GROUPS, 0.0)
    rec = jnp.where(lane == 1.0, i2 - N_GROUPS, rec)
    rec = jnp.where(lane == 2.0, w1, rec)
    return jnp.where(lane == 3.0, w2, rec)


def _mixer_kernel(of_ref, ob_ref, sg_ref, gu_ref, vn_ref, *refs, tm, n_a_tiles, two_inputs):
    xa_ref, xb_ref = (refs[0], refs[1]) if two_inputs else (refs[0], None)
    (hgain_ref, ws_ref, bs_ref, wo_ref, g1_ref, n2_ref, sh2_ref, sc2_ref, wrc_ref, br_ref,
     xo_ref, h2_ref, rt_ref, rtt_ref, cat_ref) = refs[2 if two_inputs else 1:]
    o = of_ref[...] + ob_ref[...]
    for h in range(HEADS):
        sl = slice(h * HEAD_DIM, (h + 1) * HEAD_DIM)
        hg = _rms(o[:, sl]) * hgain_ref[:, sl] * sg_ref[:, sl].astype(F32)
        cat_ref[:, sl] = hg.astype(BF16)
    for cc in range(tm // SGU_CHUNK):
        rows = slice(cc * SGU_CHUNK, (cc + 1) * SGU_CHUNK)
        for h in range(HEADS):
            sl = slice(h * HEAD_DIM, (h + 1) * HEAD_DIM)
            mixed = _dot(ws_ref[h], vn_ref[rows, sl]) + bs_ref[h]
            cat_ref[rows, HW + h * HEAD_DIM:HW + (h + 1) * HEAD_DIM] = (
                gu_ref[rows, sl].astype(F32) * mixed).astype(BF16)
    xn = _stream_tile(xa_ref, xb_ref, n_a_tiles) + g1_ref[...] * _dot(cat_ref[...], wo_ref[...])
    xo_ref[...] = xn
    h2 = _rms(xn) * n2_ref[...]
    h2 = h2 * (1.0 + sc2_ref[...]) + sh2_ref[...]
    _store_token_tiles(h2_ref, h2)
    hi, lo = _split2(h2)
    both = _dot(hi, wrc_ref[...])
    logits = (both[:, :ROUTE_LANES] + both[:, ROUTE_LANES:]) + _dot(lo, wrc_ref[:, :ROUTE_LANES]) + br_ref[...]
    rec = _route(logits)
    rt_ref[...] = rec
    rtt_ref[...] = rec.T[:SUBLANES, :]


def _mixer(o_f, o_b, sg, gu, vn, xa, xb, t_a, t_b, mod3, tokens_per_mod_row, hgain, w_s, b_s, w_out, n2,
           wr_cat, br):
    d = xa.shape[1]
    tm = MIX_ROWS
    n_a, n_b = t_a // tm, t_b // tm
    t = t_a + t_b
    nd = d // LANES
    row = functools.partial(_mod_row, d, tokens_per_mod_row // tm, mod3.shape[0])
    tok = pl.BlockSpec((tm, HW), lambda i: (i, 0))
    wide = pl.BlockSpec((tm, d), lambda i: (i, 0))
    x_in, x_spec = _stream_inputs(xa, xb, n_a, tm, d)
    return pl.pallas_call(
        functools.partial(_mixer_kernel, tm=tm, n_a_tiles=n_a, two_inputs=xb is not None),
        out_shape=[jax.ShapeDtypeStruct((t, d), F32), jax.ShapeDtypeStruct((t * nd, LANES), F32),
                   jax.ShapeDtypeStruct((t, ROUTE_LANES), F32), jax.ShapeDtypeStruct((SUBLANES, t), F32)],
        grid=(n_a + n_b,),
        in_specs=[tok, tok, tok, tok, tok] + x_spec + [_full(hgain), _full(w_s), _full(b_s), _full(w_out),
                  row(2), _full(n2), row(3), row(4), _full(wr_cat), _full(br)],
        out_specs=[wide, pl.BlockSpec((tm * nd, LANES), lambda i: (i, 0)),
                   pl.BlockSpec((tm, ROUTE_LANES), lambda i: (i, 0)), pl.BlockSpec((SUBLANES, tm), lambda i: (0, i))],
        scratch_shapes=[pltpu.VMEM((tm, 2 * HW), BF16)],
        compiler_params=_cparams("arbitrary"),
        name="mixer",
    )(o_f, o_b, sg, gu, vn, *x_in, hgain, w_s, b_s, w_out, mod3, n2, mod3, mod3, wr_cat, br)


MOE_ROWS = 256
MOE_LAG = 3
RING = 3
DMA_GROUPS = 8


def _moe_kernel(last_ref, sexp_ref, tab_hbm, h_hbm, wg_ref, wu_ref, wd_ref, y_hbm,
                idx_ref, xbuf, ybuf, zbuf, xb_ref, hm_ref, wgb, wub, wdb, sem_idx, sem_g, sem_s, sem_z):
    i = pl.program_id(0)
    last = last_ref[0]
    bm = MOE_ROWS
    de = wgb.shape[1]
    d = wgb.shape[0]
    nd = d // LANES

    def idx_copy(step, slot):
        return pltpu.make_async_copy(tab_hbm.at[step], idx_ref.at[slot], sem_idx.at[slot])

    def gathered(slot):
        return pltpu.make_async_copy(h_hbm.at[pl.ds(0, bm * nd)], xbuf.at[slot], sem_g.at[slot])

    def scattered(slot):
        return pltpu.make_async_copy(ybuf.at[slot], y_hbm.at[pl.ds(0, bm * nd)], sem_s.at[slot])

    def step(k):
        gslot = k
        cslot = (k + 1) % RING
        sslot = k

        if k == 0:
            @pl.when(i == 0)
            def _():
                xbuf[...] = jnp.zeros_like(xbuf)
                ybuf[...] = jnp.zeros_like(ybuf)
                zbuf[...] = jnp.zeros_like(zbuf)
                idx_copy(0, 0).start()

        idx_copy(i, k).wait()

        @pl.when(i < last)
        def _():
            idx_copy(i + 1, (k + 1) % RING).start()

        @pl.when(i >= 2)
        def _():
            gathered(cslot).wait()
            scattered(cslot).wait()

        @pl.when((i == 0) | (sexp_ref[i] != sexp_ref[jnp.maximum(i - 1, 0)]))
        def _():
            wgb[...] = wg_ref[...].astype(BF16)
            wub[...] = wu_ref[...].astype(BF16)
            wdb[...] = wd_ref[...].astype(BF16)

        per = bm // (DMA_GROUPS // 2)

        def scatter_group(g):
            for r in range(g * per, (g + 1) * per):
                dst = pl.multiple_of(idx_ref[k, 1, r], nd)
                pltpu.make_async_copy(ybuf.at[sslot, pl.ds(r * nd, nd)], y_hbm.at[pl.ds(dst, nd)],
                                      sem_s.at[sslot]).start()

        def gather_group(g):
            for r in range(g * per, (g + 1) * per):
                src = pl.multiple_of(idx_ref[k, 0, r], nd)
                pltpu.make_async_copy(h_hbm.at[pl.ds(src, nd)], xbuf.at[gslot, pl.ds(r * nd, nd)],
                                      sem_g.at[gslot]).start()

        dma_groups = [functools.partial(scatter_group, g) for g in range(DMA_GROUPS // 2)]
        dma_groups += [functools.partial(gather_group, g) for g in range(DMA_GROUPS // 2)]

        def issue_some():
            if dma_groups:
                dma_groups.pop(0)()

        xsrc = xbuf.at[cslot]
        for j in range(nd):
            xb_ref[:, j * LANES:(j + 1) * LANES] = xsrc[pl.ds(j, bm, stride=nd), :].astype(BF16)
        nh = 2
        for j in range(nh):
            cs = slice(j * de // nh, (j + 1) * de // nh)
            issue_some()
            gate = _dot(xb_ref[...], wgb[:, cs])
            issue_some()
            hm_ref[:, cs] = (_silu(gate) * _dot(xb_ref[...], wub[:, cs])).astype(BF16)
        ydst = ybuf.at[cslot]
        n_down = min(4, nd)
        for j in range(n_down):
            issue_some()
            _store_token_tiles_cols(ydst, _dot(hm_ref[...], wdb[:, j * d // n_down:(j + 1) * d // n_down]),
                                    j * nd // n_down, nd)
        while dma_groups:
            issue_some()

        @pl.when(i == last)
        def _():
            gathered(gslot).wait()
            gathered((k + 2) % RING).wait()
            scattered(sslot).wait()
            scattered((k + 2) % RING).wait()

    for k in range(RING):
        pl.when((i <= last) & (i % RING == k))(functools.partial(step, k))

    @pl.when(i > last)
    def _():
        fill = pltpu.make_async_copy(zbuf, y_hbm.at[pl.ds((i - MOE_LAG) * (bm * nd), bm * nd)], sem_z.at[0])
        fill.start()
        fill.wait()


def _moe(h2, tab, last, sexp, n_out_rows, w_gate, w_up, w_down, layer):
    d, de = w_gate.shape[-2:]
    nd = d // LANES
    n_steps = tab.shape[0]
    grid_spec = pltpu.PrefetchScalarGridSpec(
        num_scalar_prefetch=2,
        grid=(n_steps,),
        in_specs=[
            pl.BlockSpec(memory_space=pl.ANY),
            pl.BlockSpec(memory_space=pl.ANY),
            pl.BlockSpec((None, None, d, de), lambda i, la, se: (layer, se[i], 0, 0)),
            pl.BlockSpec((None, None, d, de), lambda i, la, se: (layer, se[i], 0, 0)),
            pl.BlockSpec((None, None, de, d), lambda i, la, se: (layer, se[i], 0, 0)),
        ],
        out_specs=pl.BlockSpec(memory_space=pl.ANY),
        scratch_shapes=[
            pltpu.SMEM((RING, 2, MOE_ROWS), jnp.int32),
            pltpu.VMEM((RING, MOE_ROWS * nd, LANES), F32),
            pltpu.VMEM((RING, MOE_ROWS * nd, LANES), F32),
            pltpu.VMEM((MOE_ROWS * nd, LANES), F32),
            pltpu.VMEM((MOE_ROWS, d), BF16),
            pltpu.VMEM((MOE_ROWS, de), BF16),
            pltpu.VMEM((d, de), BF16),
            pltpu.VMEM((d, de), BF16),
            pltpu.VMEM((de, d), BF16),
            pltpu.SemaphoreType.DMA((RING,)),
            pltpu.SemaphoreType.DMA((RING,)),
            pltpu.SemaphoreType.DMA((RING,)),
            pltpu.SemaphoreType.DMA((1,)),
        ],
    )
    return pl.pallas_call(
        _moe_kernel,
        out_shape=jax.ShapeDtypeStruct((n_out_rows * nd, LANES), F32),
        grid_spec=grid_spec,
        compiler_params=_cparams("arbitrary"),
        name="moe",
    )(last, sexp, tab, h2, w_gate, w_up, w_down)


def _dispatch_tables(expert_flat, ttot, nd):
    bm = MOE_ROWS
    n_slots = expert_flat.shape[0]
    n_blocks = -(-n_slots // bm) + N_EXPERTS
    n_steps = n_blocks + MOE_LAG
    n_main = 1 << (n_slots.bit_length() - 1)
    pieces = []
    for lo, n in ((0, n_main), (n_main, n_slots - n_main)):
        if n:
            e = expert_flat[lo:lo + n]
            _, order = lax.sort_key_val(e, lax.iota(jnp.int32, n))
            cnt = jnp.sum(e[:, None] == jnp.arange(N_EXPERTS, dtype=jnp.int32)[None, :], axis=0, dtype=jnp.int32)
            pieces.append((order + lo, cnt))
    counts = sum(cnt for _, cnt in pieces)
    padded = (counts + bm - 1) // bm * bm
    pad_end = jnp.cumsum(padded)
    pad_start = pad_end - padded
    start = jnp.cumsum(counts) - counts
    blk_row0 = jnp.arange(n_blocks, dtype=jnp.int32) * bm
    bexp = jnp.minimum(jnp.sum(pad_end[None, :] <= blk_row0[:, None], axis=1), N_EXPERTS - 1).astype(jnp.int32)
    is_exp = bexp[:, None] == jnp.arange(N_EXPERTS, dtype=jnp.int32)[None, :]

    def of_block(per_expert):
        return jnp.sum(jnp.where(is_exp, per_expert[None, :], 0), axis=1, dtype=jnp.int32)

    blk_off, blk_cnt, blk_pad0 = of_block(-pad_start) + blk_row0, of_block(counts), of_block(start + counts)
    blk_piece = [(of_block(jnp.cumsum(cnt) - cnt), of_block(cnt)) for _, cnt in pieces]
    blk_off, blk_cnt, blk_pad0, blk_piece = lax.optimization_barrier((blk_off, blk_cnt, blk_pad0, blk_piece))
    lane = jnp.arange(bm, dtype=jnp.int32)[None, :]
    off = blk_off[:, None] + lane
    valid = off < blk_cnt[:, None]
    src, rem, base = jnp.zeros_like(off), off, 0
    for (order, _), (first, cnt_here) in zip(pieces, blk_piece):
        here = (rem >= 0) & (rem < cnt_here[:, None])
        src = jnp.where(here, base + first[:, None] + rem, src)
        rem = rem - cnt_here[:, None]
        base += order.shape[0]
    slot = jnp.concatenate([order for order, _ in pieces])[jnp.clip(src, 0, n_slots - 1)]
    pad_rank = blk_row0[:, None] + lane - blk_pad0[:, None]
    gsrc = jnp.where(valid, slot % ttot, 0)
    sdst = jnp.where(valid, slot, n_slots + pad_rank)
    spare = n_blocks * bm + jnp.arange(MOE_LAG * bm, dtype=jnp.int32).reshape(MOE_LAG, bm)
    gtab = jnp.concatenate([gsrc, jnp.zeros((MOE_LAG, bm), jnp.int32)], axis=0)
    stab = jnp.concatenate([spare, sdst], axis=0)
    tab = (jnp.stack([gtab, stab], axis=1) * nd).astype(jnp.int32)
    n_used = jnp.sum(padded) // bm
    last = (n_used + MOE_LAG - 1).astype(jnp.int32).reshape(1)
    sexp = bexp[jnp.clip(jnp.arange(n_steps) - (MOE_LAG - 1), 0, n_blocks - 1)]
    return tab, last, sexp, n_steps * bm


def _final_kernel(x_ref, y0_ref, y1_ref, rt_ref, g2_ref, fn_ref, o_ref):
    o_ref[...] = _rms(_moe_residual(x_ref, y0_ref, y1_ref, rt_ref, g2_ref)) * fn_ref[...]


def _final(x2d, y2, ttot, route, mod3, tokens_per_mod_row, final_norm):
    t, d = x2d.shape
    tm = MIX_ROWS
    b1 = ttot // tm
    nd = d // LANES
    return pl.pallas_call(
        _final_kernel,
        out_shape=jax.ShapeDtypeStruct((t, d), F32),
        grid=(t // tm,),
        in_specs=[pl.BlockSpec((tm, d), lambda i: (i, 0)),
                  pl.BlockSpec((tm * nd, LANES), lambda i: (i, 0)),
                  pl.BlockSpec((tm * nd, LANES), lambda i: (i + b1, 0)),
                  pl.BlockSpec((tm, ROUTE_LANES), lambda i: (i, 0)),
                  _mod_row(d, tokens_per_mod_row // tm, mod3.shape[0], 5),
                  _full(final_norm)],
        out_specs=pl.BlockSpec((tm, d), lambda i: (i, 0)),
        compiler_params=_cparams("arbitrary"),
        name="final",
    )(x2d, y2, y2, route, mod3, final_norm)


def kernel(x, c, ctx, c_ctx, norm1, norm2, w_mod, b_mod, w_in, lb_logits, hgrn_norm, sgu_norm, sgu_w, sgu_b,
           w_out, w_group, b_group, w_router, b_router, w_gate, w_up, w_down, final_norm):
    b, l, d = x.shape
    lc = ctx.shape[1]
    depth = w_mod.shape[0]
    t_lat, t_ctx = b * l, b * lc
    nd = d // LANES
    assert d % LANES == 0 and l % max(PROJ_ROWS, MIX_ROWS, SCAN_ROWS) == 0 and lc % SCAN_ROWS == 0, (d, l, lc)
    assert t_ctx % max(PROJ_ROWS, MIX_ROWS) == 0 and b < MOD_ROWS, (b, lc)
    assert w_in.shape[-1] == 7 * HW and w_out.shape[-2] == 2 * HW, (w_in.shape, w_out.shape)

    lb_cum = jnp.cumsum(jax.nn.softmax(lb_logits.astype(F32), axis=0), axis=0)
    lower_bound = jnp.maximum(lb_cum - lb_cum[0:1], 0.0)

    cc = jnp.zeros((MOD_ROWS, d), F32).at[:b].set(c).at[b].set(c_ctx)
    mod = _modulation(cc, w_mod, b_mod)

    w_route = jnp.concatenate([w_group, w_router], axis=-1)
    w_route = jnp.pad(w_route, ((0, 0), (0, 0), (0, ROUTE_LANES - w_route.shape[-1])))
    wr_hi = w_route.astype(BF16)
    wr_cat = jnp.concatenate([wr_hi, (w_route - wr_hi.astype(F32)).astype(BF16)], axis=-1)
    b_route = jnp.concatenate([b_group, b_router], axis=-1)
    b_route = jnp.pad(b_route, ((0, 0), (0, ROUTE_LANES - b_route.shape[-1])))[:, None, :]
    b_s = jnp.broadcast_to(sgu_b[..., None], sgu_b.shape + (HEAD_DIM,)).astype(F32)

    w_in_b, w_out_b, sgu_w_b = w_in.astype(BF16), w_out.astype(BF16), sgu_w.astype(BF16)

    xa, xb, t_a, t_b = x.reshape(t_lat, d), ctx.reshape(t_ctx, d), t_lat, t_ctx
    fn = final_norm.reshape(1, d)
    moe = None

    for layer in range(depth):
        last = layer == depth - 1
        mod3 = mod[layer, :b + 1].reshape(b + 1, 1, N_MOD * d)
        n1 = norm1[layer].reshape(1, d)
        n2 = norm2[layer].reshape(1, d)
        sgn = sgu_norm[layer].reshape(1, HW)
        hgain = hgrn_norm[layer].reshape(1, HW)

        q, i, lf_f, lf_b, sg, gu, vn, *x_new = _inproj(xa, xb, t_a, t_b, mod3, l, n1, w_in_b[layer],
                                                       lower_bound[layer], sgn, layer == 0, moe)
        if x_new:
            xa = x_new[0]
        o_f, o_b = _hgrn(q, i, lf_f, lf_b, b, l, lc)

        if last:
            xb, t_a, t_b = None, t_lat, 0
        ttot = t_a + t_b
        xs, h2, route, route_t = _mixer(o_f, o_b, sg, gu, vn, xa, xb, t_a, t_b, mod3, l, hgain, sgu_w_b[layer],
                                        b_s[layer], w_out_b[layer], n2, wr_cat[layer], b_route[layer])

        expert_flat = route_t[:TOP_K].astype(jnp.int32).reshape(-1)
        tab, last_step, sexp, n_out_rows = _dispatch_tables(expert_flat, ttot, nd)
        y2 = _moe(h2, tab, last_step, sexp, n_out_rows, w_gate, w_up, w_down, layer)

        xa, xb, t_a, t_b = xs, None, ttot, 0
        moe = (y2, ttot, route, mod3)

    return _final(xs, y2, ttot, route, mod3, l, fn).reshape(b, l, d)
```

```python
import functools

import jax
import jax.numpy as jnp
from jax import lax
from jax.experimental import pallas as pl
from jax.experimental.pallas import tpu as pltpu

F32 = jnp.float32
BF16 = jnp.bfloat16

EPS = 1e-6
LOG2_E = 1.4426950408889634
HEADS = 4
HEAD_DIM = 128
HW = HEADS * HEAD_DIM
HGRN_CHUNK = 64
SGU_CHUNK = 128
N_GROUPS = 4
EXPERTS_PER_GROUP = 8
N_EXPERTS = N_GROUPS * EXPERTS_PER_GROUP
TOP_K = 2
N_MOD = 6
LANES = 128
SUBLANES = 8
ROUTE_LANES = LANES
MOD_ROWS = 16

MOD_COLS = 1536
PROJ_ROWS = 512
SCAN_ROWS = 256
MIX_ROWS = 512

VMEM_LIMIT = 48 * 1024 * 1024


def _cparams(*sem):
    return pltpu.CompilerParams(dimension_semantics=sem, vmem_limit_bytes=VMEM_LIMIT)


def _split2(a):
    hi = a.astype(BF16)
    lo = (a - hi.astype(F32)).astype(BF16)
    return hi, lo


def _dot(a, b):
    return jnp.dot(a, b, preferred_element_type=F32)


def _dot_nt(a, b):
    return lax.dot_general(a, b, (((1,), (1,)), ((), ())), preferred_element_type=F32)


def _dot_tn(a, b):
    return lax.dot_general(a, b, (((0,), (0,)), ((), ())), preferred_element_type=F32)


def _dot3(a, b):
    ah, al = _split2(a)
    bh, bl = _split2(b)
    return _dot(ah, bh) + (_dot(al, bh) + _dot(ah, bl))


def _silu(x):
    return x / (1.0 + jnp.exp(-x))


def _rms(x):
    return x * lax.rsqrt(jnp.mean(x * x, axis=-1, keepdims=True) + EPS)


def _full(a):
    return pl.BlockSpec(a.shape, lambda *_: (0,) * a.ndim)


def _stream_inputs(xa, xb, n_a, tm, d):
    if xb is None:
        return [xa], [pl.BlockSpec((tm, d), lambda i: (i, 0))]
    return [xa, xb], [pl.BlockSpec((tm, d), lambda i: (jnp.minimum(i, n_a - 1), 0)),
                      pl.BlockSpec((tm, d), lambda i: (jnp.maximum(i - n_a, 0), 0))]


def _mod_row(d, tiles_per_row, n_rows, j):
    return pl.BlockSpec((None, 1, d), lambda i: (jnp.minimum(i // tiles_per_row, n_rows - 1), 0, j))


def _store_token_tiles_cols(ref, x, j0, nd):
    rows, w = x.shape
    for j in range(w // LANES):
        ref[pl.ds(j0 + j, rows, stride=nd), :] = x[:, j * LANES:(j + 1) * LANES]


def _store_token_tiles(ref, x):
    _store_token_tiles_cols(ref, x, 0, x.shape[1] // LANES)


def _load_token_tiles(ref, rows, nd):
    return jnp.concatenate([ref[pl.ds(j, rows, stride=nd), :] for j in range(nd)], axis=-1)


def _mod_kernel(c_ref, w_ref, b_ref, o_ref):
    o_ref[...] = _dot3(_silu(c_ref[...]), w_ref[...]) + b_ref[...]


def _modulation(cc, w_mod, b_mod):
    depth, d, n = w_mod.shape
    tn = min(MOD_COLS, n)
    return pl.pallas_call(
        _mod_kernel,
        out_shape=jax.ShapeDtypeStruct((depth, MOD_ROWS, n), F32),
        grid=(depth, n // tn),
        in_specs=[
            pl.BlockSpec((MOD_ROWS, d), lambda l, j: (0, 0)),
            pl.BlockSpec((None, d, tn), lambda l, j: (l, 0, j)),
            pl.BlockSpec((None, 1, tn), lambda l, j: (l, 0, j)),
        ],
        out_specs=pl.BlockSpec((None, MOD_ROWS, tn), lambda l, j: (l, 0, j)),
        compiler_params=_cparams("arbitrary", "arbitrary"),
        name="modulation",
    )(cc, w_mod, b_mod.reshape(depth, 1, n))


def _log_forget(z, lb, lb_is_zero):
    ls = jnp.minimum(z, 0.0) - jnp.log(1.0 + jnp.exp(-jnp.abs(z)))
    if lb_is_zero:
        return ls
    return jnp.maximum(jnp.log(lb + (1.0 - lb) * jnp.exp(ls)), ls)


def _stream_tile(xa_ref, xb_ref, n_a_tiles):
    if xb_ref is None:
        return xa_ref[...]
    return jnp.where(pl.program_id(0) < n_a_tiles, xa_ref[...], xb_ref[...])


def _moe_residual(x_ref, y0_ref, y1_ref, rt_ref, g2_ref):
    w = rt_ref[...]
    tm, d = x_ref.shape
    nd = d // LANES
    f = w[:, 2:3] * _load_token_tiles(y0_ref, tm, nd) + w[:, 3:4] * _load_token_tiles(y1_ref, tm, nd)
    return x_ref[...] + g2_ref[...] * f


def _inproj_kernel(*refs, lb_is_zero, n_a_tiles, n_x):
    if n_x == 5:
        x = _moe_residual(*refs[:5])
    else:
        x = _stream_tile(refs[0], refs[1] if n_x == 2 else None, n_a_tiles)
    (sh_ref, sc_ref, n1_ref, w_ref, lb_ref, sgn_ref,
     q_ref, i_ref, lff_ref, lfb_ref, sg_ref, gu_ref, vn_ref, *xo_ref) = refs[n_x:]
    if xo_ref:
        xo_ref[0][...] = x
    h = _rms(x) * n1_ref[...]
    hb = (h * (1.0 + sc_ref[...]) + sh_ref[...]).astype(BF16)

    def proj(j):
        return _dot(hb, w_ref[:, j * HW:(j + 1) * HW])

    vn_ref[...] = (_rms(jax.nn.gelu(proj(6))) * sgn_ref[...]).astype(BF16)
    lff_ref[...] = _log_forget(proj(1), lb_ref[0:1, :], lb_is_zero)
    lfb_ref[...] = _log_forget(proj(2), lb_ref[1:2, :], lb_is_zero)
    gu_ref[...] = jax.nn.gelu(proj(5)).astype(BF16)
    sg_ref[...] = _silu(proj(4)).astype(BF16)
    q_ref[...] = proj(0).astype(BF16)
    i_ref[...] = proj(3).astype(BF16)


def _inproj(xa, xb, t_a, t_b, mod3, tokens_per_mod_row, n1, w_in, lb, sgu_gain, lb_is_zero, moe=None):
    d = xa.shape[1]
    tm = PROJ_ROWS
    n_a, n_b = t_a // tm, t_b // tm
    t = t_a + t_b
    nd = d // LANES
    tok = pl.BlockSpec((tm, HW), lambda i: (i, 0))
    wide = pl.BlockSpec((tm, d), lambda i: (i, 0))
    row = functools.partial(_mod_row, d, tokens_per_mod_row // tm, mod3.shape[0])
    out_shape = [jax.ShapeDtypeStruct((t, HW), dt) for dt in (BF16, BF16, F32, F32, BF16, BF16, BF16)]
    out_specs = [tok] * 7
    if moe is None:
        x_in, x_spec = _stream_inputs(xa, xb, n_a, tm, d)
    else:
        y2, ttot, route, mod3_prev = moe
        b1 = ttot // tm
        x_in = [xa, y2, y2, route, mod3_prev]
        x_spec = [wide,
                  pl.BlockSpec((tm * nd, LANES), lambda i: (i, 0)),
                  pl.BlockSpec((tm * nd, LANES), lambda i: (i + b1, 0)),
                  pl.BlockSpec((tm, ROUTE_LANES), lambda i: (i, 0)),
                  _mod_row(d, tokens_per_mod_row // tm, mod3_prev.shape[0], 5)]
        out_shape.append(jax.ShapeDtypeStruct((t, d), F32))
        out_specs.append(wide)
    return pl.pallas_call(
        functools.partial(_inproj_kernel, lb_is_zero=lb_is_zero, n_a_tiles=n_a, n_x=len(x_in)),
        out_shape=out_shape,
        grid=(n_a + n_b,),
        in_specs=x_spec + [row(0), row(1), _full(n1), _full(w_in), _full(lb), _full(sgu_gain)],
        out_specs=out_specs,
        compiler_params=_cparams("arbitrary"),
        name="inproj",
    )(*x_in, mod3, mod3, n1, w_in, lb, sgu_gain)


def _chunk_cumsum(x, reverse):
    c, w = x.shape
    g = c // SUBLANES
    x3 = x.reshape(g, SUBLANES, w)
    sub = lax.broadcasted_iota(jnp.int32, x3.shape, 1)
    for s in (1, 2, 4):
        if reverse:
            x3 = x3 + jnp.where(sub < SUBLANES - s, pltpu.roll(x3, SUBLANES - s, axis=1), 0.0)
        else:
            x3 = x3 + jnp.where(sub >= s, pltpu.roll(x3, s, axis=1), 0.0)
    edge = 0 if reverse else SUBLANES - 1
    tot = x3[:, edge:edge + 1, :]
    offs = [None] * g
    acc = jnp.zeros((1, w), F32)
    for gi in (reversed(range(g)) if reverse else range(g)):
        offs[gi] = acc
        acc = acc + tot[gi]
    x3 = x3 + jnp.stack(offs, axis=0)
    return x3.reshape(c, w)


def _scan_chunk(q_ref, i_ref, lf_ref, o_ref, st_ref, r0, reverse):
    c = HGRN_CHUNK
    rows = lax.broadcasted_iota(jnp.int32, (c, c), 0)
    cols = lax.broadcasted_iota(jnp.int32, (c, c), 1)
    incl = (cols >= rows) if reverse else (cols <= rows)
    ref_row = c // 2 if reverse else c // 2 - 1
    tot_row = 0 if reverse else c - 1
    lf = lf_ref[pl.ds(r0, c), :] * LOG2_E
    cum = _chunk_cumsum(lf, reverse)
    ref = cum[ref_row:ref_row + 1, :]
    tot = cum[tot_row:tot_row + 1, :]
    k = 1.0 - jnp.exp2(lf)
    qf = q_ref[pl.ds(r0, c), :].astype(F32)
    iv = i_ref[pl.ds(r0, c), :]
    q_in = (qf * jnp.exp2(cum - ref)).astype(BF16)
    k_in = (k * jnp.exp2(ref - cum)).astype(BF16)
    k_st = (k * jnp.exp2(tot - cum)).astype(BF16)
    q_st = (qf * jnp.exp2(cum)).astype(BF16)
    dec = jnp.exp2(tot)
    for h in range(HEADS):
        sl = slice(h * HEAD_DIM, (h + 1) * HEAD_DIM)
        sc = _dot_nt(q_in[:, sl], k_in[:, sl])
        sc = jnp.where(incl, sc, 0.0).astype(BF16)
        st = st_ref[h]
        o_ref[pl.ds(r0, c), sl] = _dot(sc, iv[:, sl]) + _dot_nt(q_st[:, sl], st.astype(BF16))
        st_ref[h] = st * dec[:, sl] + _dot_tn(iv[:, sl], k_st[:, sl])


def _hgrn_kernel(qf_ref, if_ref, lff_ref, qb_ref, ib_ref, lfb_ref, of_ref, ob_ref, stf_ref, stb_ref, *, tt):
    @pl.when(pl.program_id(1) == 0)
    def _():
        stf_ref[...] = jnp.zeros_like(stf_ref)
        stb_ref[...] = jnp.zeros_like(stb_ref)

    nchunks = tt // HGRN_CHUNK
    for ci in range(nchunks):
        _scan_chunk(qf_ref, if_ref, lff_ref, of_ref, stf_ref, ci * HGRN_CHUNK, False)
        _scan_chunk(qb_ref, ib_ref, lfb_ref, ob_ref, stb_ref, (nchunks - 1 - ci) * HGRN_CHUNK, True)


def _hgrn(q, i, lf_f, lf_b, batch, seq, ctx_len):
    t = q.shape[0]
    tt = SCAN_ROWS
    nt, nc = seq // tt, ctx_len // tt
    ctx0 = batch * nt

    def fwd(b, s):
        return (jnp.where(s < nc, ctx0 + b * nc + s, b * nt + (s - nc)), 0)

    def bwd(b, s):
        return (jnp.where(s < nc, ctx0 + b * nc + (nc - 1 - s), b * nt + (nt - 1 - (s - nc))), 0)

    tf, tb = pl.BlockSpec((tt, HW), fwd), pl.BlockSpec((tt, HW), bwd)
    return pl.pallas_call(
        functools.partial(_hgrn_kernel, tt=tt),
        out_shape=[jax.ShapeDtypeStruct((t, HW), F32)] * 2,
        grid=(batch, nc + nt),
        in_specs=[tf, tf, tf, tb, tb, tb],
        out_specs=[tf, tb],
        scratch_shapes=[pltpu.VMEM((HEADS, HEAD_DIM, HEAD_DIM), F32)] * 2,
        compiler_params=_cparams("arbitrary", "arbitrary"),
        name="hgrn",
    )(q, i, lf_f, q, i, lf_b)


def _route(logits):
    lane = lax.broadcasted_iota(jnp.int32, logits.shape, 1).astype(F32)
    neg = -jnp.inf
    is_group = lane < N_GROUPS
    gl = jnp.where(is_group, logits, neg)
    gmax = jnp.max(gl, axis=-1, keepdims=True)
    g_sel = jnp.min(jnp.where(gl == gmax, lane, float(ROUTE_LANES)), axis=-1, keepdims=True)
    den = jnp.sum(jnp.where(is_group, jnp.exp(logits - gmax), 0.0), axis=-1, keepdims=True)
    p_sel = 1.0 / den
    first = N_GROUPS + EXPERTS_PER_GROUP * g_sel
    el = jnp.where((lane >= first) & (lane < first + EXPERTS_PER_GROUP), logits, neg)
    t1 = jnp.max(el, axis=-1, keepdims=True)
    i1 = jnp.min(jnp.where(el == t1, lane, float(ROUTE_LANES)), axis=-1, keepdims=True)
    el2 = jnp.where(lane == i1, neg, el)
    t2 = jnp.max(el2, axis=-1, keepdims=True)
    i2 = jnp.min(jnp.where(el2 == t2, lane, float(ROUTE_LANES)), axis=-1, keepdims=True)
    e2 = jnp.exp(t2 - t1)
    w1 = p_sel / (1.0 + e2)
    w2 = p_sel * e2 / (1.0 + e2)
    rec = jnp.where(lane == 0.0, i1 - N_GROUPS, 0.0)
    rec = jnp.where(lane == 1.0, i2 - N_GROUPS, rec)
    rec = jnp.where(lane == 2.0, w1, rec)
    return jnp.where(lane == 3.0, w2, rec)


def _mixer_kernel(of_ref, ob_ref, sg_ref, gu_ref, vn_ref, *refs, tm, n_a_tiles, two_inputs):
    xa_ref, xb_ref = (refs[0], refs[1]) if two_inputs else (refs[0], None)
    (hgain_ref, ws_ref, bs_ref, wo_ref, g1_ref, n2_ref, sh2_ref, sc2_ref, wrc_ref, br_ref,
     xo_ref, h2_ref, rt_ref, rtt_ref, cat_ref) = refs[2 if two_inputs else 1:]
    o = of_ref[...] + ob_ref[...]
    for h in range(HEADS):
        sl = slice(h * HEAD_DIM, (h + 1) * HEAD_DIM)
        hg = _rms(o[:, sl]) * hgain_ref[:, sl] * sg_ref[:, sl].astype(F32)
        cat_ref[:, sl] = hg.astype(BF16)
    for cc in range(tm // SGU_CHUNK):
        rows = slice(cc * SGU_CHUNK, (cc + 1) * SGU_CHUNK)
        for h in range(HEADS):
            sl = slice(h * HEAD_DIM, (h + 1) * HEAD_DIM)
            mixed = _dot(ws_ref[h], vn_ref[rows, sl]) + bs_ref[h]
            cat_ref[rows, HW + h * HEAD_DIM:HW + (h + 1) * HEAD_DIM] = (
                gu_ref[rows, sl].astype(F32) * mixed).astype(BF16)
    xn = _stream_tile(xa_ref, xb_ref, n_a_tiles) + g1_ref[...] * _dot(cat_ref[...], wo_ref[...])
    xo_ref[...] = xn
    h2 = _rms(xn) * n2_ref[...]
    h2 = h2 * (1.0 + sc2_ref[...]) + sh2_ref[...]
    _store_token_tiles(h2_ref, h2)
    hi, lo = _split2(h2)
    both = _dot(hi, wrc_ref[...])
    logits = (both[:, :ROUTE_LANES] + both[:, ROUTE_LANES:]) + _dot(lo, wrc_ref[:, :ROUTE_LANES]) + br_ref[...]
    rec = _route(logits)
    rt_ref[...] = rec
    rtt_ref[...] = rec.T[:SUBLANES, :]


def _mixer(o_f, o_b, sg, gu, vn, xa, xb, t_a, t_b, mod3, tokens_per_mod_row, hgain, w_s, b_s, w_out, n2,
           wr_cat, br):
    d = xa.shape[1]
    tm = MIX_ROWS
    n_a, n_b = t_a // tm, t_b // tm
    t = t_a + t_b
    nd = d // LANES
    row = functools.partial(_mod_row, d, tokens_per_mod_row // tm, mod3.shape[0])
    tok = pl.BlockSpec((tm, HW), lambda i: (i, 0))
    wide = pl.BlockSpec((tm, d), lambda i: (i, 0))
    x_in, x_spec = _stream_inputs(xa, xb, n_a, tm, d)
    return pl.pallas_call(
        functools.partial(_mixer_kernel, tm=tm, n_a_tiles=n_a, two_inputs=xb is not None),
        out_shape=[jax.ShapeDtypeStruct((t, d), F32), jax.ShapeDtypeStruct((t * nd, LANES), F32),
                   jax.ShapeDtypeStruct((t, ROUTE_LANES), F32), jax.ShapeDtypeStruct((SUBLANES, t), F32)],
        grid=(n_a + n_b,),
        in_specs=[tok, tok, tok, tok, tok] + x_spec + [_full(hgain), _full(w_s), _full(b_s), _full(w_out),
                  row(2), _full(n2), row(3), row(4), _full(wr_cat), _full(br)],
        out_specs=[wide, pl.BlockSpec((tm * nd, LANES), lambda i: (i, 0)),
                   pl.BlockSpec((tm, ROUTE_LANES), lambda i: (i, 0)), pl.BlockSpec((SUBLANES, tm), lambda i: (0, i))],
        scratch_shapes=[pltpu.VMEM((tm, 2 * HW), BF16)],
        compiler_params=_cparams("arbitrary"),
        name="mixer",
    )(o_f, o_b, sg, gu, vn, *x_in, hgain, w_s, b_s, w_out, mod3, n2, mod3, mod3, wr_cat, br)


MOE_ROWS = 256
MOE_LAG = 3
RING = 3
DMA_GROUPS = 8


def _moe_kernel(last_ref, sexp_ref, tab_hbm, h_hbm, wg_ref, wu_ref, wd_ref, y_hbm,
                idx_ref, xbuf, ybuf, zbuf, xb_ref, hm_ref, wgb, wub, wdb, sem_idx, sem_g, sem_s, sem_z):
    i = pl.program_id(0)
    last = last_ref[0]
    bm = MOE_ROWS
    de = wgb.shape[1]
    d = wgb.shape[0]
    nd = d // LANES

    def idx_copy(step, slot):
        return pltpu.make_async_copy(tab_hbm.at[step], idx_ref.at[slot], sem_idx.at[slot])

    def gathered(slot):
        return pltpu.make_async_copy(h_hbm.at[pl.ds(0, bm * nd)], xbuf.at[slot], sem_g.at[slot])

    def scattered(slot):
        return pltpu.make_async_copy(ybuf.at[slot], y_hbm.at[pl.ds(0, bm * nd)], sem_s.at[slot])

    def step(k):
        gslot = k
        cslot = (k + 1) % RING
        sslot = k

        if k == 0:
            @pl.when(i == 0)
            def _():
                xbuf[...] = jnp.zeros_like(xbuf)
                ybuf[...] = jnp.zeros_like(ybuf)
                zbuf[...] = jnp.zeros_like(zbuf)
                idx_copy(0, 0).start()

        idx_copy(i, k).wait()

        @pl.when(i < last)
        def _():
            idx_copy(i + 1, (k + 1) % RING).start()

        @pl.when(i >= 2)
        def _():
            gathered(cslot).wait()
            scattered(cslot).wait()

        @pl.when((i == 0) | (sexp_ref[i] != sexp_ref[jnp.maximum(i - 1, 0)]))
        def _():
            wgb[...] = wg_ref[...].astype(BF16)
            wub[...] = wu_ref[...].astype(BF16)
            wdb[...] = wd_ref[...].astype(BF16)

        per = bm // (DMA_GROUPS // 2)

        def scatter_group(g):
            for r in range(g * per, (g + 1) * per):
                dst = pl.multiple_of(idx_ref[k, 1, r], nd)
                pltpu.make_async_copy(ybuf.at[sslot, pl.ds(r * nd, nd)], y_hbm.at[pl.ds(dst, nd)],
                                      sem_s.at[sslot]).start(priority=1)

        def gather_group(g):
            for r in range(g * per, (g + 1) * per):
                src = pl.multiple_of(idx_ref[k, 0, r], nd)
                pltpu.make_async_copy(h_hbm.at[pl.ds(src, nd)], xbuf.at[gslot, pl.ds(r * nd, nd)],
                                      sem_g.at[gslot]).start(priority=0)

        dma_groups = [functools.partial(scatter_group, g) for g in range(DMA_GROUPS // 2)]
        dma_groups += [functools.partial(gather_group, g) for g in range(DMA_GROUPS // 2)]

        def issue_some():
            if dma_groups:
                dma_groups.pop(0)()

        xsrc = xbuf.at[cslot]
        for j in range(nd):
            xb_ref[:, j * LANES:(j + 1) * LANES] = xsrc[pl.ds(j, bm, stride=nd), :].astype(BF16)
        nh = 2
        for j in range(nh):
            cs = slice(j * de // nh, (j + 1) * de // nh)
            issue_some()
            gate = _dot(xb_ref[...], wgb[:, cs])
            issue_some()
            hm_ref[:, cs] = (_silu(gate) * _dot(xb_ref[...], wub[:, cs])).astype(BF16)
        ydst = ybuf.at[cslot]
        n_down = min(4, nd)
        for j in range(n_down):
            issue_some()
            _store_token_tiles_cols(ydst, _dot(hm_ref[...], wdb[:, j * d // n_down:(j + 1) * d // n_down]),
                                    j * nd // n_down, nd)
        while dma_groups:
            issue_some()

        @pl.when(i == last)
        def _():
            gathered(gslot).wait()
            gathered((k + 2) % RING).wait()
            scattered(sslot).wait()
            scattered((k + 2) % RING).wait()

    for k in range(RING):
        pl.when((i <= last) & (i % RING == k))(functools.partial(step, k))

    @pl.when(i > last)
    def _():
        fill = pltpu.make_async_copy(zbuf, y_hbm.at[pl.ds((i - MOE_LAG) * (bm * nd), bm * nd)], sem_z.at[0])
        fill.start()
        fill.wait()


def _moe(h2, tab, last, sexp, n_out_rows, w_gate, w_up, w_down, layer):
    d, de = w_gate.shape[-2:]
    nd = d // LANES
    n_steps = tab.shape[0]
    grid_spec = pltpu.PrefetchScalarGridSpec(
        num_scalar_prefetch=2,
        grid=(n_steps,),
        in_specs=[
            pl.BlockSpec(memory_space=pl.ANY),
            pl.BlockSpec(memory_space=pl.ANY),
            pl.BlockSpec((None, None, d, de), lambda i, la, se: (layer, se[i], 0, 0)),
            pl.BlockSpec((None, None, d, de), lambda i, la, se: (layer, se[i], 0, 0)),
            pl.BlockSpec((None, None, de, d), lambda i, la, se: (layer, se[i], 0, 0)),
        ],
        out_specs=pl.BlockSpec(memory_space=pl.ANY),
        scratch_shapes=[
            pltpu.SMEM((RING, 2, MOE_ROWS), jnp.int32),
            pltpu.VMEM((RING, MOE_ROWS * nd, LANES), F32),
            pltpu.VMEM((RING, MOE_ROWS * nd, LANES), F32),
            pltpu.VMEM((MOE_ROWS * nd, LANES), F32),
            pltpu.VMEM((MOE_ROWS, d), BF16),
            pltpu.VMEM((MOE_ROWS, de), BF16),
            pltpu.VMEM((d, de), BF16),
            pltpu.VMEM((d, de), BF16),
            pltpu.VMEM((de, d), BF16),
            pltpu.SemaphoreType.DMA((RING,)),
            pltpu.SemaphoreType.DMA((RING,)),
            pltpu.SemaphoreType.DMA((RING,)),
            pltpu.SemaphoreType.DMA((1,)),
        ],
    )
    return pl.pallas_call(
        _moe_kernel,
        out_shape=jax.ShapeDtypeStruct((n_out_rows * nd, LANES), F32),
        grid_spec=grid_spec,
        compiler_params=_cparams("arbitrary"),
        name="moe",
    )(last, sexp, tab, h2, w_gate, w_up, w_down)


def _dispatch_tables(expert_flat, ttot, nd):
    bm = MOE_ROWS
    n_slots = expert_flat.shape[0]
    n_blocks = -(-n_slots // bm) + N_EXPERTS
    n_steps = n_blocks + MOE_LAG
    n_main = 1 << (n_slots.bit_length() - 1)
    pieces = []
    for lo, n in ((0, n_main), (n_main, n_slots - n_main)):
        if n:
            e = expert_flat[lo:lo + n]
            _, order = lax.sort_key_val(e, lax.iota(jnp.int32, n))
            cnt = jnp.sum(e[:, None] == jnp.arange(N_EXPERTS, dtype=jnp.int32)[None, :], axis=0, dtype=jnp.int32)
            pieces.append((order + lo, cnt))
    counts = sum(cnt for _, cnt in pieces)
    padded = (counts + bm - 1) // bm * bm
    pad_end = jnp.cumsum(padded)
    pad_start = pad_end - padded
    start = jnp.cumsum(counts) - counts
    blk_row0 = jnp.arange(n_blocks, dtype=jnp.int32) * bm
    bexp = jnp.minimum(jnp.sum(pad_end[None, :] <= blk_row0[:, None], axis=1), N_EXPERTS - 1).astype(jnp.int32)
    is_exp = bexp[:, None] == jnp.arange(N_EXPERTS, dtype=jnp.int32)[None, :]

    def of_block(per_expert):
        return jnp.sum(jnp.where(is_exp, per_expert[None, :], 0), axis=1, dtype=jnp.int32)

    blk_off, blk_cnt, blk_pad0 = of_block(-pad_start) + blk_row0, of_block(counts), of_block(start + counts)
    blk_piece = [(of_block(jnp.cumsum(cnt) - cnt), of_block(cnt)) for _, cnt in pieces]
    blk_off, blk_cnt, blk_pad0, blk_piece = lax.optimization_barrier((blk_off, blk_cnt, blk_pad0, blk_piece))
    lane = jnp.arange(bm, dtype=jnp.int32)[None, :]
    off = blk_off[:, None] + lane
    valid = off < blk_cnt[:, None]
    src, rem, base = jnp.zeros_like(off), off, 0
    for (order, _), (first, cnt_here) in zip(pieces, blk_piece):
        here = (rem >= 0) & (rem < cnt_here[:, None])
        src = jnp.where(here, base + first[:, None] + rem, src)
        rem = rem - cnt_here[:, None]
        base += order.shape[0]
    slot = jnp.concatenate([order for order, _ in pieces])[jnp.clip(src, 0, n_slots - 1)]
    pad_rank = blk_row0[:, None] + lane - blk_pad0[:, None]
    gsrc = jnp.where(valid, slot % ttot, 0)
    sdst = jnp.where(valid, slot, n_slots + pad_rank)
    spare = n_blocks * bm + jnp.arange(MOE_LAG * bm, dtype=jnp.int32).reshape(MOE_LAG, bm)
    gtab = jnp.concatenate([gsrc, jnp.zeros((MOE_LAG, bm), jnp.int32)], axis=0)
    stab = jnp.concatenate([spare, sdst], axis=0)
    tab = (jnp.stack([gtab, stab], axis=1) * nd).astype(jnp.int32)
    n_used = jnp.sum(padded) // bm
    last = (n_used + MOE_LAG - 1).astype(jnp.int32).reshape(1)
    sexp = bexp[jnp.clip(jnp.arange(n_steps) - (MOE_LAG - 1), 0, n_blocks - 1)]
    return tab, last, sexp, n_steps * bm


def _final_kernel(x_ref, y0_ref, y1_ref, rt_ref, g2_ref, fn_ref, o_ref):
    o_ref[...] = _rms(_moe_residual(x_ref, y0_ref, y1_ref, rt_ref, g2_ref)) * fn_ref[...]


def _final(x2d, y2, ttot, route, mod3, tokens_per_mod_row, final_norm):
    t, d = x2d.shape
    tm = MIX_ROWS
    b1 = ttot // tm
    nd = d // LANES
    return pl.pallas_call(
        _final_kernel,
        out_shape=jax.ShapeDtypeStruct((t, d), F32),
        grid=(t // tm,),
        in_specs=[pl.BlockSpec((tm, d), lambda i: (i, 0)),
                  pl.BlockSpec((tm * nd, LANES), lambda i: (i, 0)),
                  pl.BlockSpec((tm * nd, LANES), lambda i: (i + b1, 0)),
                  pl.BlockSpec((tm, ROUTE_LANES), lambda i: (i, 0)),
                  _mod_row(d, tokens_per_mod_row // tm, mod3.shape[0], 5),
                  _full(final_norm)],
        out_specs=pl.BlockSpec((tm, d), lambda i: (i, 0)),
        compiler_params=_cparams("arbitrary"),
        name="final",
    )(x2d, y2, y2, route, mod3, final_norm)


def kernel(x, c, ctx, c_ctx, norm1, norm2, w_mod, b_mod, w_in, lb_logits, hgrn_norm, sgu_norm, sgu_w, sgu_b,
           w_out, w_group, b_group, w_router, b_router, w_gate, w_up, w_down, final_norm):
    b, l, d = x.shape
    lc = ctx.shape[1]
    depth = w_mod.shape[0]
    t_lat, t_ctx = b * l, b * lc
    nd = d // LANES
    assert d % LANES == 0 and l % max(PROJ_ROWS, MIX_ROWS, SCAN_ROWS) == 0 and lc % SCAN_ROWS == 0, (d, l, lc)
    assert t_ctx % max(PROJ_ROWS, MIX_ROWS) == 0 and b < MOD_ROWS, (b, lc)
    assert w_in.shape[-1] == 7 * HW and w_out.shape[-2] == 2 * HW, (w_in.shape, w_out.shape)

    lb_cum = jnp.cumsum(jax.nn.softmax(lb_logits.astype(F32), axis=0), axis=0)
    lower_bound = jnp.maximum(lb_cum - lb_cum[0:1], 0.0)

    cc = jnp.zeros((MOD_ROWS, d), F32).at[:b].set(c).at[b].set(c_ctx)
    mod = _modulation(cc, w_mod, b_mod)

    w_route = jnp.concatenate([w_group, w_router], axis=-1)
    w_route = jnp.pad(w_route, ((0, 0), (0, 0), (0, ROUTE_LANES - w_route.shape[-1])))
    wr_hi = w_route.astype(BF16)
    wr_cat = jnp.concatenate([wr_hi, (w_route - wr_hi.astype(F32)).astype(BF16)], axis=-1)
    b_route = jnp.concatenate([b_group, b_router], axis=-1)
    b_route = jnp.pad(b_route, ((0, 0), (0, ROUTE_LANES - b_route.shape[-1])))[:, None, :]
    b_s = jnp.broadcast_to(sgu_b[..., None], sgu_b.shape + (HEAD_DIM,)).astype(F32)

    w_in_b, w_out_b, sgu_w_b = w_in.astype(BF16), w_out.astype(BF16), sgu_w.astype(BF16)

    xa, xb, t_a, t_b = x.reshape(t_lat, d), ctx.reshape(t_ctx, d), t_lat, t_ctx
    fn = final_norm.reshape(1, d)
    moe = None

    for layer in range(depth):
        last = layer == depth - 1
        mod3 = mod[layer, :b + 1].reshape(b + 1, 1, N_MOD * d)
        n1 = norm1[layer].reshape(1, d)
        n2 = norm2[layer].reshape(1, d)
        sgn = sgu_norm[layer].reshape(1, HW)
        hgain = hgrn_norm[layer].reshape(1, HW)

        q, i, lf_f, lf_b, sg, gu, vn, *x_new = _inproj(xa, xb, t_a, t_b, mod3, l, n1, w_in_b[layer],
                                                       lower_bound[layer], sgn, layer == 0, moe)
        if x_new:
            xa = x_new[0]
        o_f, o_b = _hgrn(q, i, lf_f, lf_b, b, l, lc)

        if last:
            xb, t_a, t_b = None, t_lat, 0
        ttot = t_a + t_b
        xs, h2, route, route_t = _mixer(o_f, o_b, sg, gu, vn, xa, xb, t_a, t_b, mod3, l, hgain, sgu_w_b[layer],
                                        b_s[layer], w_out_b[layer], n2, wr_cat[layer], b_route[layer])

        expert_flat = route_t[:TOP_K].astype(jnp.int32).reshape(-1)
        tab, last_step, sexp, n_out_rows = _dispatch_tables(expert_flat, ttot, nd)
        y2 = _moe(h2, tab, last_step, sexp, n_out_rows, w_gate, w_up, w_down, layer)

        xa, xb, t_a, t_b = xs, None, ttot, 0
        moe = (y2, ttot, route, mod3)

    return _final(xs, y2, ttot, route, mod3, l, fn).reshape(b, l, d)
```

```python
import functools

import jax
import jax.numpy as jnp
from jax import lax
from jax.experimental import pallas as pl
from jax.experimental.pallas import tpu as pltpu

F32 = jnp.float32
BF16 = jnp.bfloat16

EPS = 1e-6
LOG2_E = 1.4426950408889634
HEADS = 4
HEAD_DIM = 128
HW = HEADS * HEAD_DIM
HGRN_CHUNK = 64
SGU_CHUNK = 128
N_GROUPS = 4
EXPERTS_PER_GROUP = 8
N_EXPERTS = N_GROUPS * EXPERTS_PER_GROUP
TOP_K = 2
N_MOD = 6
LANES = 128
SUBLANES = 8
ROUTE_LANES = LANES
MOD_ROWS = 16

MOD_COLS = 1536
PROJ_ROWS = 512
SCAN_ROWS = 256
MIX_ROWS = 512

VMEM_LIMIT = 48 * 1024 * 1024


def _cparams(*sem):
    return pltpu.CompilerParams(dimension_semantics=sem, vmem_limit_bytes=VMEM_LIMIT)


def _split2(a):
    hi = a.astype(BF16)
    lo = (a - hi.astype(F32)).astype(BF16)
    return hi, lo


def _dot(a, b):
    return jnp.dot(a, b, preferred_element_type=F32)


def _dot_nt(a, b):
    return lax.dot_general(a, b, (((1,), (1,)), ((), ())), preferred_element_type=F32)


def _dot_tn(a, b):
    return lax.dot_general(a, b, (((0,), (0,)), ((), ())), preferred_element_type=F32)


def _dot3(a, b):
    ah, al = _split2(a)
    bh, bl = _split2(b)
    return _dot(ah, bh) + (_dot(al, bh) + _dot(ah, bl))


def _silu(x):
    return x / (1.0 + jnp.exp(-x))


def _rms(x):
    return x * lax.rsqrt(jnp.mean(x * x, axis=-1, keepdims=True) + EPS)


def _full(a):
    return pl.BlockSpec(a.shape, lambda *_: (0,) * a.ndim)


def _stream_inputs(xa, xb, n_a, tm, d):
    if xb is None:
        return [xa], [pl.BlockSpec((tm, d), lambda i: (i, 0))]
    return [xa, xb], [pl.BlockSpec((tm, d), lambda i: (jnp.minimum(i, n_a - 1), 0)),
                      pl.BlockSpec((tm, d), lambda i: (jnp.maximum(i - n_a, 0), 0))]


def _mod_row(d, tiles_per_row, n_rows, j):
    return pl.BlockSpec((None, 1, d), lambda i: (jnp.minimum(i // tiles_per_row, n_rows - 1), 0, j))


def _store_token_tiles_cols(ref, x, j0, nd):
    rows, w = x.shape
    for j in range(w // LANES):
        ref[pl.ds(j0 + j, rows, stride=nd), :] = x[:, j * LANES:(j + 1) * LANES]


def _store_token_tiles(ref, x):
    _store_token_tiles_cols(ref, x, 0, x.shape[1] // LANES)


def _load_token_tiles(ref, rows, nd):
    return jnp.concatenate([ref[pl.ds(j, rows, stride=nd), :] for j in range(nd)], axis=-1)


def _mod_kernel(c_ref, w_ref, b_ref, o_ref):
    o_ref[...] = _dot3(_silu(c_ref[...]), w_ref[...]) + b_ref[...]


def _modulation(cc, w_mod, b_mod):
    depth, d, n = w_mod.shape
    tn = min(MOD_COLS, n)
    return pl.pallas_call(
        _mod_kernel,
        out_shape=jax.ShapeDtypeStruct((depth, MOD_ROWS, n), F32),
        grid=(depth, n // tn),
        in_specs=[
            pl.BlockSpec((MOD_ROWS, d), lambda l, j: (0, 0)),
            pl.BlockSpec((None, d, tn), lambda l, j: (l, 0, j)),
            pl.BlockSpec((None, 1, tn), lambda l, j: (l, 0, j)),
        ],
        out_specs=pl.BlockSpec((None, MOD_ROWS, tn), lambda l, j: (l, 0, j)),
        compiler_params=_cparams("arbitrary", "arbitrary"),
        name="modulation",
    )(cc, w_mod, b_mod.reshape(depth, 1, n))


def _log_forget(z, lb, lb_is_zero):
    ls = jnp.minimum(z, 0.0) - jnp.log(1.0 + jnp.exp(-jnp.abs(z)))
    if lb_is_zero:
        return ls
    return jnp.maximum(jnp.log(lb + (1.0 - lb) * jnp.exp(ls)), ls)


def _stream_tile(xa_ref, xb_ref, n_a_tiles):
    if xb_ref is None:
        return xa_ref[...]
    return jnp.where(pl.program_id(0) < n_a_tiles, xa_ref[...], xb_ref[...])


def _moe_residual(x_ref, y0_ref, y1_ref, rt_ref, g2_ref):
    w = rt_ref[...]
    tm, d = x_ref.shape
    nd = d // LANES
    f = w[:, 2:3] * _load_token_tiles(y0_ref, tm, nd) + w[:, 3:4] * _load_token_tiles(y1_ref, tm, nd)
    return x_ref[...] + g2_ref[...] * f


def _inproj_kernel(*refs, lb_is_zero, n_a_tiles, n_x):
    if n_x == 5:
        x = _moe_residual(*refs[:5])
    else:
        x = _stream_tile(refs[0], refs[1] if n_x == 2 else None, n_a_tiles)
    (sh_ref, sc_ref, n1_ref, w_ref, lb_ref, sgn_ref,
     q_ref, i_ref, lff_ref, lfb_ref, sg_ref, gu_ref, vn_ref, *xo_ref) = refs[n_x:]
    if xo_ref:
        xo_ref[0][...] = x
    h = _rms(x) * n1_ref[...]
    hb = (h * (1.0 + sc_ref[...]) + sh_ref[...]).astype(BF16)

    def proj(j):
        return _dot(hb, w_ref[:, j * HW:(j + 1) * HW])

    vn_ref[...] = (_rms(jax.nn.gelu(proj(6))) * sgn_ref[...]).astype(BF16)
    lff_ref[...] = _log_forget(proj(1), lb_ref[0:1, :], lb_is_zero)
    lfb_ref[...] = _log_forget(proj(2), lb_ref[1:2, :], lb_is_zero)
    gu_ref[...] = jax.nn.gelu(proj(5)).astype(BF16)
    sg_ref[...] = _silu(proj(4)).astype(BF16)
    q_ref[...] = proj(0).astype(BF16)
    i_ref[...] = proj(3).astype(BF16)


def _inproj(xa, xb, t_a, t_b, mod3, tokens_per_mod_row, n1, w_in, lb, sgu_gain, lb_is_zero, moe=None):
    d = xa.shape[1]
    tm = PROJ_ROWS
    n_a, n_b = t_a // tm, t_b // tm
    t = t_a + t_b
    nd = d // LANES
    tok = pl.BlockSpec((tm, HW), lambda i: (i, 0))
    wide = pl.BlockSpec((tm, d), lambda i: (i, 0))
    row = functools.partial(_mod_row, d, tokens_per_mod_row // tm, mod3.shape[0])
    out_shape = [jax.ShapeDtypeStruct((t, HW), dt) for dt in (BF16, BF16, F32, F32, BF16, BF16, BF16)]
    out_specs = [tok] * 7
    if moe is None:
        x_in, x_spec = _stream_inputs(xa, xb, n_a, tm, d)
    else:
        y2, ttot, route, mod3_prev = moe
        b1 = ttot // tm
        x_in = [xa, y2, y2, route, mod3_prev]
        x_spec = [wide,
                  pl.BlockSpec((tm * nd, LANES), lambda i: (i, 0)),
                  pl.BlockSpec((tm * nd, LANES), lambda i: (i + b1, 0)),
                  pl.BlockSpec((tm, ROUTE_LANES), lambda i: (i, 0)),
                  _mod_row(d, tokens_per_mod_row // tm, mod3_prev.shape[0], 5)]
        out_shape.append(jax.ShapeDtypeStruct((t, d), F32))
        out_specs.append(wide)
    return pl.pallas_call(
        functools.partial(_inproj_kernel, lb_is_zero=lb_is_zero, n_a_tiles=n_a, n_x=len(x_in)),
        out_shape=out_shape,
        grid=(n_a + n_b,),
        in_specs=x_spec + [row(0), row(1), _full(n1), _full(w_in), _full(lb), _full(sgu_gain)],
        out_specs=out_specs,
        compiler_params=_cparams("arbitrary"),
        name="inproj",
    )(*x_in, mod3, mod3, n1, w_in, lb, sgu_gain)


def _chunk_cumsum(x, reverse):
    c, w = x.shape
    g = c // SUBLANES
    x3 = x.reshape(g, SUBLANES, w)
    sub = lax.broadcasted_iota(jnp.int32, x3.shape, 1)
    for s in (1, 2, 4):
        if reverse:
            x3 = x3 + jnp.where(sub < SUBLANES - s, pltpu.roll(x3, SUBLANES - s, axis=1), 0.0)
        else:
            x3 = x3 + jnp.where(sub >= s, pltpu.roll(x3, s, axis=1), 0.0)
    edge = 0 if reverse else SUBLANES - 1
    tot = x3[:, edge:edge + 1, :]
    offs = [None] * g
    acc = jnp.zeros((1, w), F32)
    for gi in (reversed(range(g)) if reverse else range(g)):
        offs[gi] = acc
        acc = acc + tot[gi]
    x3 = x3 + jnp.stack(offs, axis=0)
    return x3.reshape(c, w)


def _scan_chunk(q_ref, i_ref, lf_ref, o_ref, st_ref, r0, reverse):
    c = HGRN_CHUNK
    rows = lax.broadcasted_iota(jnp.int32, (c, c), 0)
    cols = lax.broadcasted_iota(jnp.int32, (c, c), 1)
    incl = (cols >= rows) if reverse else (cols <= rows)
    ref_row = c // 2 if reverse else c // 2 - 1
    tot_row = 0 if reverse else c - 1
    lf = lf_ref[pl.ds(r0, c), :] * LOG2_E
    cum = _chunk_cumsum(lf, reverse)
    ref = cum[ref_row:ref_row + 1, :]
    tot = cum[tot_row:tot_row + 1, :]
    k = 1.0 - jnp.exp2(lf)
    qf = q_ref[pl.ds(r0, c), :].astype(F32)
    iv = i_ref[pl.ds(r0, c), :]
    q_in = (qf * jnp.exp2(cum - ref)).astype(BF16)
    k_in = (k * jnp.exp2(ref - cum)).astype(BF16)
    k_st = (k * jnp.exp2(tot - cum)).astype(BF16)
    q_st = (qf * jnp.exp2(cum)).astype(BF16)
    dec = jnp.exp2(tot)
    for h in range(HEADS):
        sl = slice(h * HEAD_DIM, (h + 1) * HEAD_DIM)
        sc = _dot_nt(q_in[:, sl], k_in[:, sl])
        sc = jnp.where(incl, sc, 0.0).astype(BF16)
        st = st_ref[h]
        o_ref[pl.ds(r0, c), sl] = (_dot(sc, iv[:, sl]) + _dot_nt(q_st[:, sl], st.astype(BF16))).astype(BF16)
        st_ref[h] = st * dec[:, sl] + _dot_tn(iv[:, sl], k_st[:, sl])


def _hgrn_kernel(qf_ref, if_ref, lff_ref, qb_ref, ib_ref, lfb_ref, of_ref, ob_ref, stf_ref, stb_ref, *, tt):
    @pl.when(pl.program_id(1) == 0)
    def _():
        stf_ref[...] = jnp.zeros_like(stf_ref)
        stb_ref[...] = jnp.zeros_like(stb_ref)

    nchunks = tt // HGRN_CHUNK
    for ci in range(nchunks):
        _scan_chunk(qf_ref, if_ref, lff_ref, of_ref, stf_ref, ci * HGRN_CHUNK, False)
        _scan_chunk(qb_ref, ib_ref, lfb_ref, ob_ref, stb_ref, (nchunks - 1 - ci) * HGRN_CHUNK, True)


def _hgrn(q, i, lf_f, lf_b, batch, seq, ctx_len):
    t = q.shape[0]
    tt = SCAN_ROWS
    nt, nc = seq // tt, ctx_len // tt
    ctx0 = batch * nt

    def fwd(b, s):
        return (jnp.where(s < nc, ctx0 + b * nc + s, b * nt + (s - nc)), 0)

    def bwd(b, s):
        return (jnp.where(s < nc, ctx0 + b * nc + (nc - 1 - s), b * nt + (nt - 1 - (s - nc))), 0)

    tf, tb = pl.BlockSpec((tt, HW), fwd), pl.BlockSpec((tt, HW), bwd)
    return pl.pallas_call(
        functools.partial(_hgrn_kernel, tt=tt),
        out_shape=[jax.ShapeDtypeStruct((t, HW), BF16)] * 2,
        grid=(batch, nc + nt),
        in_specs=[tf, tf, tf, tb, tb, tb],
        out_specs=[tf, tb],
        scratch_shapes=[pltpu.VMEM((HEADS, HEAD_DIM, HEAD_DIM), F32)] * 2,
        compiler_params=_cparams("arbitrary", "arbitrary"),
        name="hgrn",
    )(q, i, lf_f, q, i, lf_b)


def _route(logits):
    lane = lax.broadcasted_iota(jnp.int32, logits.shape, 1).astype(F32)
    neg = -jnp.inf
    is_group = lane < N_GROUPS
    gl = jnp.where(is_group, logits, neg)
    gmax = jnp.max(gl, axis=-1, keepdims=True)
    g_sel = jnp.min(jnp.where(gl == gmax, lane, float(ROUTE_LANES)), axis=-1, keepdims=True)
    den = jnp.sum(jnp.where(is_group, jnp.exp(logits - gmax), 0.0), axis=-1, keepdims=True)
    p_sel = 1.0 / den
    first = N_GROUPS + EXPERTS_PER_GROUP * g_sel
    el = jnp.where((lane >= first) & (lane < first + EXPERTS_PER_GROUP), logits, neg)
    t1 = jnp.max(el, axis=-1, keepdims=True)
    i1 = jnp.min(jnp.where(el == t1, lane, float(ROUTE_LANES)), axis=-1, keepdims=True)
    el2 = jnp.where(lane == i1, neg, el)
    t2 = jnp.max(el2, axis=-1, keepdims=True)
    i2 = jnp.min(jnp.where(el2 == t2, lane, float(ROUTE_LANES)), axis=-1, keepdims=True)
    e2 = jnp.exp(t2 - t1)
    w1 = p_sel / (1.0 + e2)
    w2 = p_sel * e2 / (1.0 + e2)
    rec = jnp.where(lane == 0.0, i1 - N_GROUPS, 0.0)
    rec = jnp.where(lane == 1.0, i2 - N_GROUPS, rec)
    rec = jnp.where(lane == 2.0, w1, rec)
    return jnp.where(lane == 3.0, w2, rec)


def _mixer_kernel(of_ref, ob_ref, sg_ref, gu_ref, vn_ref, *refs, tm, n_a_tiles, two_inputs):
    xa_ref, xb_ref = (refs[0], refs[1]) if two_inputs else (refs[0], None)
    (hgain_ref, ws_ref, bs_ref, wo_ref, g1_ref, n2_ref, sh2_ref, sc2_ref, wrc_ref, br_ref,
     xo_ref, h2_ref, rt_ref, rtt_ref, cat_ref) = refs[2 if two_inputs else 1:]
    o = of_ref[...].astype(F32) + ob_ref[...].astype(F32)
    for h in range(HEADS):
        sl = slice(h * HEAD_DIM, (h + 1) * HEAD_DIM)
        hg = _rms(o[:, sl]) * hgain_ref[:, sl] * sg_ref[:, sl].astype(F32)
        cat_ref[:, sl] = hg.astype(BF16)
    for cc in range(tm // SGU_CHUNK):
        rows = slice(cc * SGU_CHUNK, (cc + 1) * SGU_CHUNK)
        for h in range(HEADS):
            sl = slice(h * HEAD_DIM, (h + 1) * HEAD_DIM)
            mixed = _dot(ws_ref[h], vn_ref[rows, sl]) + bs_ref[h]
            cat_ref[rows, HW + h * HEAD_DIM:HW + (h + 1) * HEAD_DIM] = (
                gu_ref[rows, sl].astype(F32) * mixed).astype(BF16)
    xn = _stream_tile(xa_ref, xb_ref, n_a_tiles) + g1_ref[...] * _dot(cat_ref[...], wo_ref[...])
    xo_ref[...] = xn
    h2 = _rms(xn) * n2_ref[...]
    h2 = h2 * (1.0 + sc2_ref[...]) + sh2_ref[...]
    _store_token_tiles(h2_ref, h2)
    hi, lo = _split2(h2)
    both = _dot(hi, wrc_ref[...])
    logits = (both[:, :ROUTE_LANES] + both[:, ROUTE_LANES:]) + _dot(lo, wrc_ref[:, :ROUTE_LANES]) + br_ref[...]
    rec = _route(logits)
    rt_ref[...] = rec
    rtt_ref[...] = rec.T[:SUBLANES, :]


def _mixer(o_f, o_b, sg, gu, vn, xa, xb, t_a, t_b, mod3, tokens_per_mod_row, hgain, w_s, b_s, w_out, n2,
           wr_cat, br):
    d = xa.shape[1]
    tm = MIX_ROWS
    n_a, n_b = t_a // tm, t_b // tm
    t = t_a + t_b
    nd = d // LANES
    row = functools.partial(_mod_row, d, tokens_per_mod_row // tm, mod3.shape[0])
    tok = pl.BlockSpec((tm, HW), lambda i: (i, 0))
    wide = pl.BlockSpec((tm, d), lambda i: (i, 0))
    x_in, x_spec = _stream_inputs(xa, xb, n_a, tm, d)
    return pl.pallas_call(
        functools.partial(_mixer_kernel, tm=tm, n_a_tiles=n_a, two_inputs=xb is not None),
        out_shape=[jax.ShapeDtypeStruct((t, d), F32), jax.ShapeDtypeStruct((t * nd, LANES), F32),
                   jax.ShapeDtypeStruct((t, ROUTE_LANES), F32), jax.ShapeDtypeStruct((SUBLANES, t), F32)],
        grid=(n_a + n_b,),
        in_specs=[tok, tok, tok, tok, tok] + x_spec + [_full(hgain), _full(w_s), _full(b_s), _full(w_out),
                  row(2), _full(n2), row(3), row(4), _full(wr_cat), _full(br)],
        out_specs=[wide, pl.BlockSpec((tm * nd, LANES), lambda i: (i, 0)),
                   pl.BlockSpec((tm, ROUTE_LANES), lambda i: (i, 0)), pl.BlockSpec((SUBLANES, tm), lambda i: (0, i))],
        scratch_shapes=[pltpu.VMEM((tm, 2 * HW), BF16)],
        compiler_params=_cparams("arbitrary"),
        name="mixer",
    )(o_f, o_b, sg, gu, vn, *x_in, hgain, w_s, b_s, w_out, mod3, n2, mod3, mod3, wr_cat, br)


MOE_ROWS = 256
MOE_LAG = 3
RING = 3
DMA_GROUPS = 8


def _moe_kernel(last_ref, sexp_ref, tab_hbm, h_hbm, wg_ref, wu_ref, wd_ref, y_hbm,
                idx_ref, xbuf, ybuf, zbuf, xb_ref, hm_ref, wgb, wub, wdb, sem_idx, sem_g, sem_s, sem_z):
    i = pl.program_id(0)
    last = last_ref[0]
    bm = MOE_ROWS
    de = wgb.shape[1]
    d = wgb.shape[0]
    nd = d // LANES

    def idx_copy(step, slot):
        return pltpu.make_async_copy(tab_hbm.at[step], idx_ref.at[slot], sem_idx.at[slot])

    def gathered(slot):
        return pltpu.make_async_copy(h_hbm.at[pl.ds(0, bm * nd)], xbuf.at[slot], sem_g.at[slot])

    def scattered(slot):
        return pltpu.make_async_copy(ybuf.at[slot], y_hbm.at[pl.ds(0, bm * nd)], sem_s.at[slot])

    def step(k):
        gslot = k
        cslot = (k + 1) % RING
        sslot = k

        if k == 0:
            @pl.when(i == 0)
            def _():
                xbuf[...] = jnp.zeros_like(xbuf)
                ybuf[...] = jnp.zeros_like(ybuf)
                zbuf[...] = jnp.zeros_like(zbuf)
                idx_copy(0, 0).start()

        idx_copy(i, k).wait()

        @pl.when(i < last)
        def _():
            idx_copy(i + 1, (k + 1) % RING).start()

        @pl.when(i >= 2)
        def _():
            gathered(cslot).wait()
            scattered(cslot).wait()

        @pl.when((i == 0) | (sexp_ref[i] != sexp_ref[jnp.maximum(i - 1, 0)]))
        def _():
            wgb[...] = wg_ref[...].astype(BF16)
            wub[...] = wu_ref[...].astype(BF16)
            wdb[...] = wd_ref[...].astype(BF16)

        per = bm // (DMA_GROUPS // 2)

        def scatter_group(g):
            for r in range(g * per, (g + 1) * per):
                dst = pl.multiple_of(idx_ref[k, 1, r], nd)
                pltpu.make_async_copy(ybuf.at[sslot, pl.ds(r * nd, nd)], y_hbm.at[pl.ds(dst, nd)],
                                      sem_s.at[sslot]).start(priority=r % 2)

        def gather_group(g):
            for r in range(g * per, (g + 1) * per):
                src = pl.multiple_of(idx_ref[k, 0, r], nd)
                pltpu.make_async_copy(h_hbm.at[pl.ds(src, nd)], xbuf.at[gslot, pl.ds(r * nd, nd)],
                                      sem_g.at[gslot]).start(priority=r % 2)

        dma_groups = [functools.partial(scatter_group, g) for g in range(DMA_GROUPS // 2)]
        dma_groups += [functools.partial(gather_group, g) for g in range(DMA_GROUPS // 2)]

        def issue_some():
            if dma_groups:
                dma_groups.pop(0)()

        xsrc = xbuf.at[cslot]
        for j in range(nd):
            xb_ref[:, j * LANES:(j + 1) * LANES] = xsrc[pl.ds(j, bm, stride=nd), :].astype(BF16)
        nh = 2
        for j in range(nh):
            cs = slice(j * de // nh, (j + 1) * de // nh)
            issue_some()
            gate = _dot(xb_ref[...], wgb[:, cs])
            issue_some()
            hm_ref[:, cs] = (_silu(gate) * _dot(xb_ref[...], wub[:, cs])).astype(BF16)
        ydst = ybuf.at[cslot]
        n_down = min(4, nd)
        for j in range(n_down):
            issue_some()
            _store_token_tiles_cols(ydst, _dot(hm_ref[...], wdb[:, j * d // n_down:(j + 1) * d // n_down]),
                                    j * nd // n_down, nd)
        while dma_groups:
            issue_some()

        @pl.when(i == last)
        def _():
            gathered(gslot).wait()
            gathered((k + 2) % RING).wait()
            scattered(sslot).wait()
            scattered((k + 2) % RING).wait()

    for k in range(RING):
        pl.when((i <= last) & (i % RING == k))(functools.partial(step, k))

    @pl.when(i > last)
    def _():
        fill = pltpu.make_async_copy(zbuf, y_hbm.at[pl.ds((i - MOE_LAG) * (bm * nd), bm * nd)], sem_z.at[0])
        fill.start()
        fill.wait()


def _moe(h2, tab, last, sexp, n_out_rows, w_gate, w_up, w_down, layer):
    d, de = w_gate.shape[-2:]
    nd = d // LANES
    n_steps = tab.shape[0]
    grid_spec = pltpu.PrefetchScalarGridSpec(
        num_scalar_prefetch=2,
        grid=(n_steps,),
        in_specs=[
            pl.BlockSpec(memory_space=pl.ANY),
            pl.BlockSpec(memory_space=pl.ANY),
            pl.BlockSpec((None, None, d, de), lambda i, la, se: (layer, se[i], 0, 0)),
            pl.BlockSpec((None, None, d, de), lambda i, la, se: (layer, se[i], 0, 0)),
            pl.BlockSpec((None, None, de, d), lambda i, la, se: (layer, se[i], 0, 0)),
        ],
        out_specs=pl.BlockSpec(memory_space=pl.ANY),
        scratch_shapes=[
            pltpu.SMEM((RING, 2, MOE_ROWS), jnp.int32),
            pltpu.VMEM((RING, MOE_ROWS * nd, LANES), F32),
            pltpu.VMEM((RING, MOE_ROWS * nd, LANES), F32),
            pltpu.VMEM((MOE_ROWS * nd, LANES), F32),
            pltpu.VMEM((MOE_ROWS, d), BF16),
            pltpu.VMEM((MOE_ROWS, de), BF16),
            pltpu.VMEM((d, de), BF16),
            pltpu.VMEM((d, de), BF16),
            pltpu.VMEM((de, d), BF16),
            pltpu.SemaphoreType.DMA((RING,)),
            pltpu.SemaphoreType.DMA((RING,)),
            pltpu.SemaphoreType.DMA((RING,)),
            pltpu.SemaphoreType.DMA((1,)),
        ],
    )
    return pl.pallas_call(
        _moe_kernel,
        out_shape=jax.ShapeDtypeStruct((n_out_rows * nd, LANES), F32),
        grid_spec=grid_spec,
        compiler_params=_cparams("arbitrary"),
        name="moe",
    )(last, sexp, tab, h2, w_gate, w_up, w_down)


def _dispatch_tables(expert_flat, ttot, nd):
    bm = MOE_ROWS
    n_slots = expert_flat.shape[0]
    n_blocks = -(-n_slots // bm) + N_EXPERTS
    n_steps = n_blocks + MOE_LAG
    n_main = 1 << (n_slots.bit_length() - 1)
    pieces = []
    for lo, n in ((0, n_main), (n_main, n_slots - n_main)):
        if n:
            e = expert_flat[lo:lo + n]
            _, order = lax.sort_key_val(e, lax.iota(jnp.int32, n))
            cnt = jnp.sum(e[:, None] == jnp.arange(N_EXPERTS, dtype=jnp.int32)[None, :], axis=0, dtype=jnp.int32)
            pieces.append((order + lo, cnt))
    counts = sum(cnt for _, cnt in pieces)
    padded = (counts + bm - 1) // bm * bm
    pad_end = jnp.cumsum(padded)
    pad_start = pad_end - padded
    start = jnp.cumsum(counts) - counts
    blk_row0 = jnp.arange(n_blocks, dtype=jnp.int32) * bm
    bexp = jnp.minimum(jnp.sum(pad_end[None, :] <= blk_row0[:, None], axis=1), N_EXPERTS - 1).astype(jnp.int32)
    is_exp = bexp[:, None] == jnp.arange(N_EXPERTS, dtype=jnp.int32)[None, :]

    def of_block(per_expert):
        return jnp.sum(jnp.where(is_exp, per_expert[None, :], 0), axis=1, dtype=jnp.int32)

    blk_off, blk_cnt, blk_pad0 = of_block(-pad_start) + blk_row0, of_block(counts), of_block(start + counts)
    blk_piece = [(of_block(jnp.cumsum(cnt) - cnt), of_block(cnt)) for _, cnt in pieces]
    blk_off, blk_cnt, blk_pad0, blk_piece = lax.optimization_barrier((blk_off, blk_cnt, blk_pad0, blk_piece))
    lane = jnp.arange(bm, dtype=jnp.int32)[None, :]
    off = blk_off[:, None] + lane
    valid = off < blk_cnt[:, None]
    src, rem, base = jnp.zeros_like(off), off, 0
    for (order, _), (first, cnt_here) in zip(pieces, blk_piece):
        here = (rem >= 0) & (rem < cnt_here[:, None])
        src = jnp.where(here, base + first[:, None] + rem, src)
        rem = rem - cnt_here[:, None]
        base += order.shape[0]
    slot = jnp.concatenate([order for order, _ in pieces])[jnp.clip(src, 0, n_slots - 1)]
    pad_rank = blk_row0[:, None] + lane - blk_pad0[:, None]
    gsrc = jnp.where(valid, slot % ttot, 0)
    sdst = jnp.where(valid, slot, n_slots + pad_rank)
    spare = n_blocks * bm + jnp.arange(MOE_LAG * bm, dtype=jnp.int32).reshape(MOE_LAG, bm)
    gtab = jnp.concatenate([gsrc, jnp.zeros((MOE_LAG, bm), jnp.int32)], axis=0)
    stab = jnp.concatenate([spare, sdst], axis=0)
    tab = (jnp.stack([gtab, stab], axis=1) * nd).astype(jnp.int32)
    n_used = jnp.sum(padded) // bm
    last = (n_used + MOE_LAG - 1).astype(jnp.int32).reshape(1)
    sexp = bexp[jnp.clip(jnp.arange(n_steps) - (MOE_LAG - 1), 0, n_blocks - 1)]
    return tab, last, sexp, n_steps * bm


def _final_kernel(x_ref, y0_ref, y1_ref, rt_ref, g2_ref, fn_ref, o_ref):
    o_ref[...] = _rms(_moe_residual(x_ref, y0_ref, y1_ref, rt_ref, g2_ref)) * fn_ref[...]


def _final(x2d, y2, ttot, route, mod3, tokens_per_mod_row, final_norm):
    t, d = x2d.shape
    tm = MIX_ROWS
    b1 = ttot // tm
    nd = d // LANES
    return pl.pallas_call(
        _final_kernel,
        out_shape=jax.ShapeDtypeStruct((t, d), F32),
        grid=(t // tm,),
        in_specs=[pl.BlockSpec((tm, d), lambda i: (i, 0)),
                  pl.BlockSpec((tm * nd, LANES), lambda i: (i, 0)),
                  pl.BlockSpec((tm * nd, LANES), lambda i: (i + b1, 0)),
                  pl.BlockSpec((tm, ROUTE_LANES), lambda i: (i, 0)),
                  _mod_row(d, tokens_per_mod_row // tm, mod3.shape[0], 5),
                  _full(final_norm)],
        out_specs=pl.BlockSpec((tm, d), lambda i: (i, 0)),
        compiler_params=_cparams("arbitrary"),
        name="final",
    )(x2d, y2, y2, route, mod3, final_norm)


def kernel(x, c, ctx, c_ctx, norm1, norm2, w_mod, b_mod, w_in, lb_logits, hgrn_norm, sgu_norm, sgu_w, sgu_b,
           w_out, w_group, b_group, w_router, b_router, w_gate, w_up, w_down, final_norm):
    b, l, d = x.shape
    lc = ctx.shape[1]
    depth = w_mod.shape[0]
    t_lat, t_ctx = b * l, b * lc
    nd = d // LANES
    assert d % LANES == 0 and l % max(PROJ_ROWS, MIX_ROWS, SCAN_ROWS) == 0 and lc % SCAN_ROWS == 0, (d, l, lc)
    assert t_ctx % max(PROJ_ROWS, MIX_ROWS) == 0 and b < MOD_ROWS, (b, lc)
    assert w_in.shape[-1] == 7 * HW and w_out.shape[-2] == 2 * HW, (w_in.shape, w_out.shape)

    lb_cum = jnp.cumsum(jax.nn.softmax(lb_logits.astype(F32), axis=0), axis=0)
    lower_bound = jnp.maximum(lb_cum - lb_cum[0:1], 0.0)

    cc = jnp.zeros((MOD_ROWS, d), F32).at[:b].set(c).at[b].set(c_ctx)
    mod = _modulation(cc, w_mod, b_mod)

    w_route = jnp.concatenate([w_group, w_router], axis=-1)
    w_route = jnp.pad(w_route, ((0, 0), (0, 0), (0, ROUTE_LANES - w_route.shape[-1])))
    wr_hi = w_route.astype(BF16)
    wr_cat = jnp.concatenate([wr_hi, (w_route - wr_hi.astype(F32)).astype(BF16)], axis=-1)
    b_route = jnp.concatenate([b_group, b_router], axis=-1)
    b_route = jnp.pad(b_route, ((0, 0), (0, ROUTE_LANES - b_route.shape[-1])))[:, None, :]
    b_s = jnp.broadcast_to(sgu_b[..., None], sgu_b.shape + (HEAD_DIM,)).astype(F32)

    w_in_b, w_out_b, sgu_w_b = w_in.astype(BF16), w_out.astype(BF16), sgu_w.astype(BF16)

    xa, xb, t_a, t_b = x.reshape(t_lat, d), ctx.reshape(t_ctx, d), t_lat, t_ctx
    fn = final_norm.reshape(1, d)
    moe = None

    for layer in range(depth):
        last = layer == depth - 1
        mod3 = mod[layer, :b + 1].reshape(b + 1, 1, N_MOD * d)
        n1 = norm1[layer].reshape(1, d)
        n2 = norm2[layer].reshape(1, d)
        sgn = sgu_norm[layer].reshape(1, HW)
        hgain = hgrn_norm[layer].reshape(1, HW)

        q, i, lf_f, lf_b, sg, gu, vn, *x_new = _inproj(xa, xb, t_a, t_b, mod3, l, n1, w_in_b[layer],
                                                       lower_bound[layer], sgn, layer == 0, moe)
        if x_new:
            xa = x_new[0]
        o_f, o_b = _hgrn(q, i, lf_f, lf_b, b, l, lc)

        if last:
            xb, t_a, t_b = None, t_lat, 0
        ttot = t_a + t_b
        xs, h2, route, route_t = _mixer(o_f, o_b, sg, gu, vn, xa, xb, t_a, t_b, mod3, l, hgain, sgu_w_b[layer],
                                        b_s[layer], w_out_b[layer], n2, wr_cat[layer], b_route[layer])

        expert_flat = route_t[:TOP_K].astype(jnp.int32).reshape(-1)
        tab, last_step, sexp, n_out_rows = _dispatch_tables(expert_flat, ttot, nd)
        y2 = _moe(h2, tab, last_step, sexp, n_out_rows, w_gate, w_up, w_down, layer)

        xa, xb, t_a, t_b = xs, None, ttot, 0
        moe = (y2, ttot, route, mod3)

    return _final(xs, y2, ttot, route, mod3, l, fn).reshape(b, l, d)
```
